```python
import math
import jax
import jax.numpy as jnp
from jax import lax
import numpy as np

D_MODEL = 2048
BATCH = 2
SEQ = 8192
DEPTH = 2

CTX_LEN = 256
GRID_W = 64
EPS = 1e-6
F32 = jnp.float32

GM_CHUNK = 128
GM_GROUPS = 8
GM_GROUP_DIM = 128
GM_WIDTH = GM_GROUPS * GM_GROUP_DIM

DN_HEADS = 8
DN_HEAD_DIM = 128
DN_WIDTH = DN_HEADS * DN_HEAD_DIM
DN_CONV = 5
DN_CHUNK = 64

MLA_HEADS = 8
MLA_NOPE = 128
MLA_ROPE = 64
MLA_V = 128
MLA_QK = MLA_NOPE + MLA_ROPE
MLA_WIDTH = MLA_HEADS * MLA_V
Q_LORA = 512
KV_LORA = 512
ROPE_BASE = 10000.0
ATTN_BLOCK = 128

N_BRANCH = 3
BRANCH_WIDTH = 1024

IN_WIDTHS = (GM_WIDTH, GM_WIDTH, DN_WIDTH, DN_WIDTH, DN_WIDTH, DN_WIDTH,
             2 * DN_HEADS, 2 * DN_HEADS, Q_LORA, KV_LORA, MLA_ROPE, N_BRANCH * D_MODEL)
D_IN = 2 * GM_WIDTH + 4 * DN_WIDTH + 4 * DN_HEADS + Q_LORA + KV_LORA + MLA_ROPE + N_BRANCH * D_MODEL

N_EXPERTS = 64
TOP_K = 8
N_GROUPS = 8
TOPK_GROUPS = 4
EXPERT_DIM = 512
SHARED_DIM = 512
ROUTED_SCALE = 2.5
MOE_BLOCK = 128

kernel_name = 'hybrid_gmlp_deltanet_mla_moe_dit'


def rmsnorm(x, g):
    xf = x.astype(F32)
    y = xf * lax.rsqrt(jnp.mean(xf * xf, axis=-1, keepdims=True) + EPS)
    return (y * g.astype(F32)).astype(x.dtype)


def l2norm(x):
    xf = x.astype(F32)
    return xf * lax.rsqrt(jnp.sum(xf * xf, axis=-1, keepdims=True) + EPS)


def rope_1d(x, pos):
    half = x.shape[-1] // 2
    freq = ROPE_BASE ** (-jnp.arange(half, dtype=F32) / half)
    ang = pos.astype(F32)[:, None] * freq
    cos = jnp.cos(ang)[:, None, :]
    sin = jnp.sin(ang)[:, None, :]
    xf = x.astype(F32)
    x1, x2 = xf[..., :half], xf[..., half:]
    return jnp.concatenate([x1 * cos - x2 * sin, x2 * cos + x1 * sin], axis=-1).astype(x.dtype)


def rope_2d(x, row_pos, col_pos):
    h = x.shape[-1] // 2
    return jnp.concatenate([rope_1d(x[..., :h], row_pos), rope_1d(x[..., h:], col_pos)], axis=-1)


def spatial_gating(u, v, gm_norm, gm_ws, gm_bs):
    bsz, n = u.shape[:2]
    v = rmsnorm(v, gm_norm).reshape(bsz, n // GM_CHUNK, GM_CHUNK, GM_GROUPS, GM_GROUP_DIM)
    mixed = jnp.einsum('gpq,bnqgc->bnpgc', gm_ws, v) + gm_bs.T[:, :, None]
    return u * mixed.reshape(bsz, n, GM_WIDTH)


def short_conv(x, w):
    ch = x.shape[-1]
    y = lax.conv_general_dilated(x, w[:, None, :].astype(x.dtype), window_strides=(1,),
                                 padding=[(DN_CONV // 2, DN_CONV // 2)],
                                 dimension_numbers=('NWC', 'WIO', 'NWC'),
                                 feature_group_count=ch)
    return jax.nn.silu(y)


def gated_delta_rule(q, k, v, g, beta, s0):
    bsz, seq_len, h, dk = q.shape
    nc = seq_len // DN_CHUNK

    def chunked(t):
        t = t.reshape(bsz, nc, DN_CHUNK, h, *t.shape[3:])
        return jnp.moveaxis(t, (1, 3), (0, 2))

    q = chunked(q) * dk ** -0.5
    k = chunked(k)
    v = chunked(v)
    gam = jnp.cumsum(chunked(g), axis=-1)
    bet = chunked(beta)[..., None]
    lower = jnp.tril(jnp.ones((DN_CHUNK, DN_CHUNK), bool))
    diff = gam[..., :, None] - gam[..., None, :]
    decay = jnp.where(lower, jnp.exp(jnp.where(lower, diff, 0.0)), 0.0)
    kb = k * bet
    a_mat = jnp.eye(DN_CHUNK, dtype=F32) + jnp.tril(jnp.einsum('nbhrd,nbhsd->nbhrs', kb, k) * decay, -1)
    rhs = jnp.concatenate([kb * jnp.exp(gam)[..., None], v * bet], axis=-1)
    sol = lax.linalg.triangular_solve(a_mat, rhs, left_side=True, lower=True, unit_diagonal=True)
    w, u = sol[..., :dk], sol[..., dk:]
    attn = jnp.einsum('nbhrd,nbhsd->nbhrs', q, k) * decay
    q_dec = q * jnp.exp(gam)[..., None]
    k_dec = k * jnp.exp(gam[..., -1:] - gam)[..., None]
    g_tot = jnp.exp(gam[..., -1])[..., None, None]

    def step(s, xs):
        w_c, u_c, attn_c, qd_c, kd_c, gt_c = xs
        v_new = u_c - w_c @ s
        o = qd_c @ s + attn_c @ v_new
        s = s * gt_c + jnp.swapaxes(kd_c, -1, -2) @ v_new
        return s, o

    s, o = lax.scan(step, s0, (w, u, attn, q_dec, k_dec, g_tot))
    o = jnp.moveaxis(o, (0, 2), (1, 3)).reshape(bsz, seq_len, h, -1)
    return o, s


def gated_delta_rule_reversed(q, k, v, g, beta, s0):
    flip = lambda t: jnp.flip(t, axis=1)
    o, s = gated_delta_rule(flip(q), flip(k), flip(v), flip(g), flip(beta), s0)
    return flip(o), s


def mla_kv(ckv, kr, kv_norm, w_ukv, row_pos=None, col_pos=None):
    bsz, n = ckv.shape[:2]
    kv = (rmsnorm(ckv, kv_norm) @ w_ukv).reshape(bsz, n, MLA_HEADS, MLA_NOPE + MLA_V)
    k_nope, v = kv[..., :MLA_NOPE], kv[..., MLA_NOPE:]
    k_rope = kr[:, :, None, :]
    if row_pos is not None:
        k_rope = rope_2d(k_rope, row_pos, col_pos)
    k = jnp.concatenate([k_nope, jnp.broadcast_to(k_rope, (bsz, n, MLA_HEADS, MLA_ROPE))], axis=-1)
    return k, v


def mla_q(cq, q_norm, w_uq, row_pos=None, col_pos=None):
    bsz, n = cq.shape[:2]
    q = (rmsnorm(cq, q_norm) @ w_uq).reshape(bsz, n, MLA_HEADS, MLA_QK)
    q_nope, q_rope = q[..., :MLA_NOPE], q[..., MLA_NOPE:]
    if row_pos is not None:
        q_rope = rope_2d(q_rope, row_pos, col_pos)
    return jnp.concatenate([q_nope, q_rope], axis=-1)


def attend(q, k, v):
    s = jnp.einsum('bqhd,bkhd->bhqk', q, k).astype(F32) * MLA_QK ** -0.5
    p = jax.nn.softmax(s, axis=-1).astype(v.dtype)
    return jnp.einsum('bhqk,bkhd->bqhd', p, v)


def blocked_attend(q, k, v):
    bsz, n = q.shape[:2]
    qb = q.reshape(bsz, n // ATTN_BLOCK, ATTN_BLOCK, MLA_HEADS, MLA_QK).swapaxes(0, 1)
    ob = lax.map(lambda qi: attend(qi, k, v), qb)
    return ob.swapaxes(0, 1).reshape(bsz, n, MLA_WIDTH)


def mixer_sublayer(hl, hc, row_pos, col_pos, with_ctx_out, w_in, gm_norm, gm_ws, gm_bs,
                   dn_conv, dn_a_log, dn_dt_bias, dn_norm, mla_q_norm, mla_kv_norm,
                   mla_w_uq, mla_w_ukv, w_branch, w_out):
    offsets = [int(o) for o in np.cumsum(IN_WIDTHS)[:-1]]
    zl = jnp.split(hl @ w_in, offsets, axis=-1)
    zc = jnp.split(hc @ w_in, offsets, axis=-1)

    def gmlp(z):
        return spatial_gating(jax.nn.gelu(z[0]), jax.nn.gelu(z[1]), gm_norm, gm_ws, gm_bs)

    def dn_prep(z):
        bsz, seq_len = z[0].shape[:2]
        qkv = short_conv(jnp.concatenate(z[2:5], axis=-1), dn_conv)
        q, k, v = [t.reshape(bsz, seq_len, DN_HEADS, DN_HEAD_DIM) for t in jnp.split(qkv, 3, axis=-1)]
        a = z[6].astype(F32).reshape(bsz, seq_len, 2, DN_HEADS)
        g = -jnp.exp(dn_a_log.astype(F32)) * jax.nn.softplus(a + dn_dt_bias.astype(F32))
        beta = jax.nn.sigmoid(z[7].astype(F32).reshape(bsz, seq_len, 2, DN_HEADS))
        return l2norm(q), l2norm(k), v.astype(F32), g, beta

    def dn_out(o, gate):
        bsz, seq_len = o.shape[:2]
        y = rmsnorm(o, dn_norm) * jax.nn.silu(gate.astype(F32).reshape(bsz, seq_len, DN_HEADS, DN_HEAD_DIM))
        return y.reshape(bsz, seq_len, DN_WIDTH).astype(hl.dtype)

    def merge(br_a, br_b, br_c, gate_logits):
        gates = jax.nn.sigmoid(gate_logits.astype(F32)).astype(hl.dtype)
        ga, gb, gcm = jnp.split(gates, N_BRANCH, axis=-1)
        y = ga * (br_a @ w_branch[0]) + gb * (br_b @ w_branch[1]) + gcm * (br_c @ w_branch[2])
        return y @ w_out

    qc, kc, vc, gcd, bc = dn_prep(zc)
    ql, kl, vl, gld, bl = dn_prep(zl)
    s0 = jnp.zeros((hl.shape[0], DN_HEADS, DN_HEAD_DIM, DN_HEAD_DIM), F32)
    oc_f, s_f = gated_delta_rule(qc, kc, vc, gcd[:, :, 0], bc[:, :, 0], s0)
    oc_b, s_b = gated_delta_rule_reversed(qc, kc, vc, gcd[:, :, 1], bc[:, :, 1], s0)
    ol_f, _ = gated_delta_rule(ql, kl, vl, gld[:, :, 0], bl[:, :, 0], s_f)
    ol_b, _ = gated_delta_rule_reversed(ql, kl, vl, gld[:, :, 1], bl[:, :, 1], s_b)

    kc_a, vc_a = mla_kv(zc[9], zc[10], mla_kv_norm, mla_w_ukv)
    kl_a, vl_a = mla_kv(zl[9], zl[10], mla_kv_norm, mla_w_ukv, row_pos, col_pos)
    ql_a = mla_q(zl[8], mla_q_norm, mla_w_uq, row_pos, col_pos)
    attn_l = blocked_attend(ql_a, jnp.concatenate([kc_a, kl_a], axis=1), jnp.concatenate([vc_a, vl_a], axis=1))

    out_l = merge(gmlp(zl), dn_out(ol_f + ol_b, zl[5]), attn_l, zl[11])
    out_c = None
    if with_ctx_out:
        qc_a = mla_q(zc[8], mla_q_norm, mla_w_uq)
        attn_c = attend(qc_a, kc_a, vc_a).reshape(hc.shape[0], hc.shape[1], MLA_WIDTH)
        out_c = merge(gmlp(zc), dn_out(oc_f + oc_b, zc[5]), attn_c, zc[11])
    return out_l, out_c


def route(h, w_router, bias):
    t = h.shape[0]
    scores = jax.nn.sigmoid((h @ w_router).astype(F32))
    choice = scores + bias.astype(F32)
    grp = choice.reshape(t, N_GROUPS, N_EXPERTS // N_GROUPS)
    grp_score = lax.top_k(grp, 2)[0].sum(-1)
    _, top_g = lax.top_k(grp_score, TOPK_GROUPS)
    gmask = jax.nn.one_hot(top_g, N_GROUPS, dtype=F32).sum(1) > 0
    masked = jnp.where(jnp.repeat(gmask, N_EXPERTS // N_GROUPS, axis=1), choice, -jnp.inf)
    _, idx = lax.top_k(masked, TOP_K)
    wts = jnp.take_along_axis(scores, idx, axis=1)
    wts = wts / jnp.sum(wts, axis=-1, keepdims=True) * ROUTED_SCALE
    return idx, wts


def moe_ffn(h, w_router, bias, w1, w3, w2, sw1, sw3, sw2):
    t, d = h.shape
    idx, wts = route(h, w_router, bias)
    m = t * TOP_K
    n_blocks = -(-m // MOE_BLOCK) + N_EXPERTS
    flat_e = idx.reshape(-1)
    flat_tok = jnp.repeat(jnp.arange(t, dtype=jnp.int32), TOP_K)
    flat_w = wts.reshape(-1)
    order = jnp.argsort(flat_e)
    se = flat_e[order]
    counts = jnp.bincount(flat_e, length=N_EXPERTS)
    padded = (counts + MOE_BLOCK - 1) // MOE_BLOCK * MOE_BLOCK
    pend = jnp.cumsum(padded)
    pstart = pend - padded
    ustart = jnp.cumsum(counts) - counts
    dest = pstart[se] + jnp.arange(m, dtype=jnp.int32) - ustart[se]
    n_rows = n_blocks * MOE_BLOCK
    row_tok = jnp.full((n_rows,), t, jnp.int32).at[dest].set(flat_tok[order])
    row_w = jnp.zeros((n_rows,), h.dtype).at[dest].set(flat_w[order].astype(h.dtype))
    blk_e = jnp.minimum(jnp.searchsorted(pend, jnp.arange(n_blocks, dtype=jnp.int32) * MOE_BLOCK, side='right'),
                        N_EXPERTS - 1)
    h_pad = jnp.concatenate([h, jnp.zeros((1, d), h.dtype)], axis=0)

    def block(acc, xs):
        tok, wt, e = xs
        xb = h_pad[tok]
        a = jax.nn.silu(xb @ w1[e]) * (xb @ w3[e])
        return acc.at[tok].add((a @ w2[e]) * wt[:, None]), None

    acc, _ = lax.scan(block, jnp.zeros_like(h_pad),
                      (row_tok.reshape(n_blocks, MOE_BLOCK), row_w.reshape(n_blocks, MOE_BLOCK), blk_e))
    shared = (jax.nn.silu(h @ sw1) * (h @ sw3)) @ sw2
    return acc[:t] + shared


def setup_inputs(seed: int = 0) -> dict:
    key = jax.random.key(seed)
    ks = iter(jax.random.split(key, 48))

    def normal(shape, scale=1.0):
        return jax.random.normal(next(ks), shape, F32) * scale

    def gain(shape):
        return 1.0 + 0.02 * normal(shape)

    L = DEPTH
    dt = jnp.exp(jax.random.uniform(next(ks), (L, 2, DN_HEADS), F32, math.log(1e-3), math.log(1e-1)))
    a_init = jax.random.uniform(next(ks), (L, 2, DN_HEADS), F32, 1.0, 16.0)
    return {
        'x': normal((BATCH, SEQ, D_MODEL)),
        'c': normal((BATCH, D_MODEL)),
        'ctx': normal((BATCH, CTX_LEN, D_MODEL)),
        'c_ctx': normal((D_MODEL,)),
        'w_mod': normal((L, D_MODEL, 6 * D_MODEL), 0.5 * D_MODEL ** -0.5),
        'b_mod': normal((L, 6 * D_MODEL), 0.02),
        'norm1': gain((L, D_MODEL)),
        'norm2': gain((L, D_MODEL)),
        'w_in': normal((L, D_MODEL, D_IN), D_MODEL ** -0.5),
        'gm_norm': gain((L, GM_WIDTH)),
        'gm_ws': normal((L, GM_GROUPS, GM_CHUNK, GM_CHUNK), GM_CHUNK ** -0.5),
        'gm_bs': gain((L, GM_GROUPS, GM_CHUNK)),
        'dn_conv': normal((L, DN_CONV, 3 * DN_WIDTH), DN_CONV ** -0.5),
        'dn_a_log': jnp.log(a_init),
        'dn_dt_bias': dt + jnp.log(-jnp.expm1(-dt)),
        'dn_norm': gain((L, DN_HEAD_DIM)),
        'mla_q_norm': gain((L, Q_LORA)),
        'mla_kv_norm': gain((L, KV_LORA)),
        'mla_w_uq': normal((L, Q_LORA, MLA_HEADS * MLA_QK), Q_LORA ** -0.5),
        'mla_w_ukv': normal((L, KV_LORA, MLA_HEADS * (MLA_NOPE + MLA_V)), KV_LORA ** -0.5),
        'w_branch': normal((L, N_BRANCH, BRANCH_WIDTH, D_MODEL), BRANCH_WIDTH ** -0.5),
        'w_out': normal((L, D_MODEL, D_MODEL), D_MODEL ** -0.5),
        'moe_router': normal((L, D_MODEL, N_EXPERTS), D_MODEL ** -0.5),
        'moe_bias': normal((L, N_EXPERTS), 0.01),
        'moe_w1': normal((L, N_EXPERTS, D_MODEL, EXPERT_DIM), D_MODEL ** -0.5),
        'moe_w3': normal((L, N_EXPERTS, D_MODEL, EXPERT_DIM), D_MODEL ** -0.5),
        'moe_w2': normal((L, N_EXPERTS, EXPERT_DIM, D_MODEL), EXPERT_DIM ** -0.5),
        'shared_w1': normal((L, D_MODEL, SHARED_DIM), D_MODEL ** -0.5),
        'shared_w3': normal((L, D_MODEL, SHARED_DIM), D_MODEL ** -0.5),
        'shared_w2': normal((L, SHARED_DIM, D_MODEL), SHARED_DIM ** -0.5),
        'final_norm': gain((D_MODEL,)),
    }


def reference(x, c, ctx, c_ctx, w_mod, b_mod, norm1, norm2, w_in, gm_norm, gm_ws, gm_bs,
              dn_conv, dn_a_log, dn_dt_bias, dn_norm, mla_q_norm, mla_kv_norm, mla_w_uq,
              mla_w_ukv, w_branch, w_out, moe_router, moe_bias, moe_w1, moe_w3, moe_w2,
              shared_w1, shared_w3, shared_w2, final_norm):
    bsz, n, d = x.shape
    ctx_len = ctx.shape[1]
    rows = n // GRID_W
    row_pos = jnp.repeat(jnp.arange(rows), GRID_W)
    col_pos = jnp.tile(jnp.arange(GRID_W), rows)
    silu_c = jax.nn.silu(c)
    silu_cc = jax.nn.silu(c_ctx)
    xl, xc = x, ctx
    for i in range(DEPTH):
        with_ctx_out = i < DEPTH - 1
        mod_l = (silu_c @ w_mod[i] + b_mod[i])[:, None, :]
        mod_c = silu_cc @ w_mod[i] + b_mod[i]
        sh1, sc1, g1, sh2, sc2, g2 = jnp.split(mod_l, 6, axis=-1)
        csh1, csc1, cg1, csh2, csc2, cg2 = jnp.split(mod_c, 6, axis=-1)
        hl = rmsnorm(xl, norm1[i]) * (1.0 + sc1) + sh1
        hc = rmsnorm(xc, norm1[i]) * (1.0 + csc1) + csh1
        mix_l, mix_c = mixer_sublayer(hl, hc, row_pos, col_pos, with_ctx_out, w_in[i], gm_norm[i],
                                      gm_ws[i], gm_bs[i], dn_conv[i], dn_a_log[i], dn_dt_bias[i],
                                      dn_norm[i], mla_q_norm[i], mla_kv_norm[i], mla_w_uq[i],
                                      mla_w_ukv[i], w_branch[i], w_out[i])
        xl = xl + g1 * mix_l
        h2l = rmsnorm(xl, norm2[i]) * (1.0 + sc2) + sh2
        moe_w = (moe_router[i], moe_bias[i], moe_w1[i], moe_w3[i], moe_w2[i],
                 shared_w1[i], shared_w3[i], shared_w2[i])
        if with_ctx_out:
            xc = xc + cg1 * mix_c
            h2c = rmsnorm(xc, norm2[i]) * (1.0 + csc2) + csh2
            n_ctx_tok = bsz * ctx_len
            f = moe_ffn(jnp.concatenate([h2c.reshape(-1, d), h2l.reshape(-1, d)], axis=0), *moe_w)
            xc = xc + cg2 * f[:n_ctx_tok].reshape(bsz, ctx_len, d)
            xl = xl + g2 * f[n_ctx_tok:].reshape(bsz, n, d)
        else:
            xl = xl + g2 * moe_ffn(h2l.reshape(-1, d), *moe_w).reshape(bsz, n, d)
    return rmsnorm(xl, final_norm)
```

```python
import functools
import math

import numpy as np
import jax
import jax.numpy as jnp
from jax import lax
from jax.experimental import pallas as pl
from jax.experimental.pallas import tpu as pltpu

F32 = jnp.float32
BF16 = jnp.bfloat16

GRID_W = 64
EPS = 1e-6
GM_CHUNK = 128
GM_GROUPS = 8
DN_HEADS = 8
DN_HEAD_DIM = 128
DN_CHUNK = 64
DN_CONV = 5
MLA_HEADS = 8
MLA_NOPE = 128
MLA_ROPE = 64
MLA_V = 128
MLA_QK = MLA_NOPE + MLA_ROPE
Q_LORA = 512
KV_LORA = 512
ROPE_BASE = 10000.0
BRANCH_W = 1024
N_EXPERTS = 64
TOP_K = 8
N_GROUPS = 8
TOPK_GROUPS = 4
ROUTED_SCALE = 2.5
LOG2E = 1.4426950408889634

LANES = 128
SUBLANES = 8
BF16_SUBLANES = 16
VMEM_LIMIT_MB = 56

ROW_TILE = 512
RES_ROW_TILE = 256
CONV_ROW_TILE = 256
COL_TILE = 1024
MOE_ROW_TILE = 256
ATTN_Q_TILE = 512
ATTN_KV_TILE = 1024


def _pick(n, pref, mult=LANES):
    if n <= pref:
        return n
    for t in range(pref - pref % mult, 0, -mult):
        if n % t == 0:
            return t
    return n


def _params(sem, mb=VMEM_LIMIT_MB):
    return pltpu.CompilerParams(dimension_semantics=sem, vmem_limit_bytes=mb * 1024 * 1024)


def _silu(x):
    return x * jax.nn.sigmoid(x)


def _gelu(x):
    return 0.5 * x * (1.0 + jnp.tanh(0.7978845608028654 * (x + 0.044715 * x * x * x)))


def _dot(a, b):
    return jnp.dot(a, b, preferred_element_type=F32)


def _dot_nt(a, b):
    return lax.dot_general(a, b, (((1,), (1,)), ((), ())), preferred_element_type=F32)


def _dot_tn(a, b):
    return lax.dot_general(a, b, (((0,), (0,)), ((), ())), preferred_element_type=F32)


def _mod_kernel(c_ref, w_ref, b_ref, o_ref):
    s = _silu(c_ref[...])
    o_ref[...] = _dot(s.astype(BF16), w_ref[...].astype(BF16)) + b_ref[...]


def modulation(cvec, w_mod, b_mod):
    nl, d, n6 = w_mod.shape
    tn = _pick(n6, COL_TILE)
    out = pl.pallas_call(
        _mod_kernel,
        grid=(nl, n6 // tn),
        in_specs=[
            pl.BlockSpec((SUBLANES, d), lambda l, n: (0, 0)),
            pl.BlockSpec((None, d, tn), lambda l, n: (l, 0, n)),
            pl.BlockSpec((None, 1, tn), lambda l, n: (l, 0, n)),
        ],
        out_specs=pl.BlockSpec((None, SUBLANES, tn), lambda l, n: (l, 0, n)),
        out_shape=jax.ShapeDtypeStruct((nl, SUBLANES, n6), F32),
        compiler_params=_params(("arbitrary", "arbitrary")),
        name="modulation",
    )(cvec, w_mod, b_mod.reshape(nl, 1, n6))
    return out.reshape(nl, SUBLANES, 1, n6)


def _seg_of_block(i, tm, lay):
    nctx_blk = lay["n_ctx"] // tm
    lat_bps = lay["S"] // tm
    return jnp.where(i < nctx_blk, 0, 1 + (i - nctx_blk) // lat_bps)


def _mod_spec(layer, tm, lay, row0_blk=0):
    n6 = 6 * lay["D"]
    return pl.BlockSpec((None, None, 1, n6), lambda i: (layer, _seg_of_block(i + row0_blk, tm, lay), 0, 0))


def _mod_part(m, k, d):
    return m[:, k * d:(k + 1) * d]


def _rms(x, g):
    return x * lax.rsqrt(jnp.mean(x * x, axis=-1, keepdims=True) + EPS) * g


def _prenorm_kernel(x_ref, g_ref, mod_ref, o_ref, *, d):
    m = mod_ref[...]
    y = _rms(x_ref[...], g_ref[...])
    o_ref[...] = (y * (1.0 + _mod_part(m, 1, d)) + _mod_part(m, 0, d)).astype(o_ref.dtype)


def prenorm(x, g, mod, layer, lay):
    t, d = x.shape
    tm = _pick(lay["n_ctx"], ROW_TILE, SUBLANES)
    return pl.pallas_call(
        functools.partial(_prenorm_kernel, d=d),
        grid=(t // tm,),
        in_specs=[
            pl.BlockSpec((tm, d), lambda i: (i, 0)),
            pl.BlockSpec((1, d), lambda i: (0, 0)),
            _mod_spec(layer, tm, lay),
        ],
        out_specs=pl.BlockSpec((tm, d), lambda i: (i, 0)),
        out_shape=jax.ShapeDtypeStruct((t, d), BF16),
        compiler_params=_params(("arbitrary",)),
        name="prenorm",
    )(x, g.reshape(1, d), mod)


def _mm_kernel(x_ref, w_ref, o_ref):
    o_ref[...] = _dot(x_ref[...], w_ref[...]).astype(o_ref.dtype)


def matmul(x, w, out_dtype, name):
    t, k = x.shape
    n = w.shape[1]
    tm = _pick(t, ROW_TILE, BF16_SUBLANES)
    tn = _pick(n, COL_TILE)
    return pl.pallas_call(
        _mm_kernel,
        grid=(n // tn, t // tm),
        in_specs=[
            pl.BlockSpec((tm, k), lambda j, i: (i, 0)),
            pl.BlockSpec((k, tn), lambda j, i: (0, j)),
        ],
        out_specs=pl.BlockSpec((tm, tn), lambda j, i: (i, j)),
        out_shape=jax.ShapeDtypeStruct((t, n), out_dtype),
        compiler_params=_params(("arbitrary", "arbitrary")),
        name=name,
    )(x, w)


def _gmlp_kernel(u_ref, v_ref, gn_ref, ws_ref, bias_ref, o_ref, *, nchunk):
    for j in range(nchunk):
        rs = slice(j * GM_CHUNK, (j + 1) * GM_CHUNK)
        v = _gelu(v_ref[rs, :].astype(F32))
        vb = _rms(v, gn_ref[...]).astype(BF16)
        u = _gelu(u_ref[rs, :].astype(F32))
        for g in range(GM_GROUPS):
            cs = slice(g * LANES, (g + 1) * LANES)
            mixed = _dot(ws_ref[g], vb[:, cs]) + bias_ref[:, cs]
            o_ref[rs, cs] = (u[:, cs] * mixed).astype(o_ref.dtype)


def gmlp(z_main, gm_norm, gm_ws, gm_bs, lay):
    t = z_main.shape[0]
    w = BRANCH_W
    tm = _pick(lay["n_ctx"], ROW_TILE, GM_CHUNK)
    bias = jnp.repeat(gm_bs.T.astype(F32), w // GM_GROUPS, axis=1)
    return pl.pallas_call(
        functools.partial(_gmlp_kernel, nchunk=tm // GM_CHUNK),
        grid=(t // tm,),
        in_specs=[
            pl.BlockSpec((tm, w), lambda i: (i, 0)),
            pl.BlockSpec((tm, w), lambda i: (i, 1)),
            pl.BlockSpec((1, w), lambda i: (0, 0)),
            pl.BlockSpec((GM_GROUPS, GM_CHUNK, GM_CHUNK), lambda i: (0, 0, 0)),
            pl.BlockSpec((GM_CHUNK, w), lambda i: (0, 0)),
        ],
        out_specs=pl.BlockSpec((tm, w), lambda i: (i, 0)),
        out_shape=jax.ShapeDtypeStruct((t, w), BF16),
        compiler_params=_params(("arbitrary",)),
        name="gmlp",
    )(z_main, z_main, gm_norm.reshape(1, w), gm_ws.astype(BF16), bias)


def _conv_kernel(x_ref, hp_ref, hn_ref, w_ref, o_ref, xe_ref, *, tm, nctx_blk, ctx_bps, lat_bps):
    j = pl.program_id(0)
    i = pl.program_id(1)
    li = i - nctx_blk
    seg_start = jnp.where(i < nctx_blk, (i % ctx_bps) == 0, (li % lat_bps) == 0)
    seg_end = jnp.where(i < nctx_blk, ((i + 1) % ctx_bps) == 0, ((li + 1) % lat_bps) == 0)
    halo = BF16_SUBLANES
    xe_ref[0:halo, :] = jnp.where(seg_start, 0.0, hp_ref[...].astype(F32))
    xe_ref[halo:halo + tm, :] = x_ref[...].astype(F32)
    xe_ref[halo + tm:2 * halo + tm, :] = jnp.where(seg_end, 0.0, hn_ref[...].astype(F32))
    base = halo - DN_CONV // 2
    acc = w_ref[0:1, :] * xe_ref[base:base + tm, :]
    for tap in range(1, DN_CONV):
        acc = acc + w_ref[tap:tap + 1, :] * xe_ref[base + tap:base + tap + tm, :]
    y = _silu(acc)
    unit = j < 2
    for h in range(DN_HEADS):
        cs = slice(h * DN_HEAD_DIM, (h + 1) * DN_HEAD_DIM)
        yh = y[:, cs]
        nrm = yh * lax.rsqrt(jnp.sum(yh * yh, axis=-1, keepdims=True) + EPS)
        o_ref[:, cs] = jnp.where(unit, nrm, yh).astype(o_ref.dtype)


def dn_short_conv(z_main, conv_w, lay):
    t = z_main.shape[0]
    w = BRANCH_W
    tm = _pick(lay["C"], CONV_ROW_TILE, BF16_SUBLANES)
    halo = BF16_SUBLANES
    hb = tm // halo
    nhalo = t // halo
    wpad = jnp.zeros((SUBLANES, 3 * w), F32).at[:DN_CONV].set(conv_w.astype(F32))
    kern = functools.partial(_conv_kernel, tm=tm, nctx_blk=lay["n_ctx"] // tm, ctx_bps=lay["C"] // tm,
                             lat_bps=lay["S"] // tm)
    return pl.pallas_call(
        kern,
        grid=(3, t // tm),
        in_specs=[
            pl.BlockSpec((tm, w), lambda j, i: (i, 2 + j)),
            pl.BlockSpec((halo, w), lambda j, i: (jnp.maximum(i * hb - 1, 0), 2 + j)),
            pl.BlockSpec((halo, w), lambda j, i: (jnp.minimum((i + 1) * hb, nhalo - 1), 2 + j)),
            pl.BlockSpec((SUBLANES, w), lambda j, i: (0, j)),
        ],
        out_specs=pl.BlockSpec((tm, w), lambda j, i: (i, j)),
        out_shape=jax.ShapeDtypeStruct((t, 3 * w), BF16),
        scratch_shapes=[pltpu.VMEM((tm + 2 * halo, w), F32)],
        compiler_params=_params(("arbitrary", "arbitrary")),
        name="dn_conv",
    )(z_main, z_main, z_main, wpad)


def _dn_masks():
    c = DN_CHUNK
    r = np.arange(c)[:, None]
    s = np.arange(c)[None, :]
    tri = np.stack([r >= s, r <= s]).astype(np.float32)
    strict = np.stack([r > s, r < s]).astype(np.float32)
    lv = []
    b = 1
    while b < c:
        same = (r // (2 * b)) == (s // (2 * b))
        lo = same & ((r // b) % 2 == 1) & ((s // b) % 2 == 0)
        up = same & ((r // b) % 2 == 0) & ((s // b) % 2 == 1)
        lv.append(np.stack([lo, up]))
        b *= 2
    lvl = np.stack(lv, axis=1).astype(np.float32)
    return tri, strict, lvl, np.eye(c, dtype=np.float32)


def _dn_kernel(qf_ref, qb_ref, abf_ref, abb_ref, alog_ref, dt_ref, tri_ref, strict_ref, lvl_ref, eye_ref,
               of_ref, ob_ref, s_ref, *, nlevels):
    step = pl.program_id(1)

    @pl.when(step == 0)
    def _():
        s_ref[...] = jnp.zeros(s_ref.shape, F32)

    hd = DN_HEAD_DIM
    w = DN_HEADS * hd
    nh = DN_HEADS
    eye = eye_ref[...]
    for d, (x_ref, ab_ref, o_ref) in enumerate(((qf_ref, abf_ref, of_ref), (qb_ref, abb_ref, ob_ref))):
        tri = tri_ref[d]
        strict = strict_ref[d]
        ab = ab_ref[...]
        g_all = -jnp.exp(alog_ref[...]) * jax.nn.softplus(ab + dt_ref[...])
        beta_all = jax.nn.sigmoid(ab)
        gam_c = jnp.dot(tri, g_all, precision=lax.Precision.HIGHEST, preferred_element_type=F32)
        gam_r = gam_c.T
        last = DN_CHUNK - 1 if d == 0 else 0
        for h in range(nh):
            lane = d * nh + h
            q = x_ref[:, h * hd:(h + 1) * hd].astype(F32)
            k = x_ref[:, w + h * hd:w + (h + 1) * hd].astype(F32)
            v = x_ref[:, 2 * w + h * hd:2 * w + (h + 1) * hd].astype(F32)
            gc = gam_c[:, lane:lane + 1]
            gr = gam_r[lane:lane + 1, :]
            bc = beta_all[:, 2 * nh + lane:2 * nh + lane + 1]
            diff = gc - gr
            dec = jnp.where(tri > 0, jnp.exp(jnp.where(tri > 0, diff, 0.0)), 0.0)
            kb = k * bc
            kbf = k.astype(BF16)
            a_mat = _dot_nt(kb.astype(BF16), kbf) * dec * strict
            qs = q * (hd ** -0.5)
            attn = _dot_nt(qs.astype(BF16), kbf) * dec
            x_inv = eye - a_mat * lvl_ref[d, 0]
            for lv in range(1, nlevels):
                m_b = (a_mat * lvl_ref[d, lv]).astype(BF16)
                xb = x_inv.astype(BF16)
                x_inv = x_inv - _dot(_dot(xb, m_b).astype(BF16), xb)
            eg = jnp.exp(gc)
            rhs = jnp.concatenate([kb * eg, v * bc], axis=1).astype(BF16)
            sol = _dot(x_inv.astype(BF16), rhs)
            w_c = sol[:, :hd]
            u_c = sol[:, hd:]
            g_last = gc[last:last + 1, :]
            q_dec = qs * eg
            k_dec = k * jnp.exp(g_last - gc)
            st = s_ref[lane]
            stb = st.astype(BF16)
            wq = jnp.concatenate([w_c, q_dec], axis=0).astype(BF16)
            r_ = _dot(wq, stb)
            v_new = u_c - r_[:DN_CHUNK]
            vnb = v_new.astype(BF16)
            o = r_[DN_CHUNK:] + _dot(attn.astype(BF16), vnb)
            s_ref[lane] = st * jnp.exp(g_last) + _dot_tn(k_dec.astype(BF16), vnb)
            o_ref[:, h * hd:(h + 1) * hd] = o


def dn_scan(qkv_c, z_misc, a_log, dt_bias, lay):
    t = qkv_c.shape[0]
    w = BRANCH_W
    c = DN_CHUNK
    nb = lay["B"]
    ncc = lay["C"] // c
    ncl = lay["S"] // c
    nsteps = ncc + ncl

    def fidx(b, s):
        return jnp.where(s < ncc, b * ncc + s, nb * ncc + b * ncl + (s - ncc))

    def bidx(b, s):
        return jnp.where(s < ncc, b * ncc + (ncc - 1 - s), nb * ncc + b * ncl + (ncl - 1 - (s - ncc)))

    tri, strict, lvl, eye = _dn_masks()
    nlevels = lvl.shape[1]
    nh = DN_HEADS
    alog_row = jnp.zeros((1, LANES), F32).at[0, :2 * nh].set(a_log.reshape(-1).astype(F32))
    dt_row = jnp.zeros((1, LANES), F32).at[0, :2 * nh].set(dt_bias.reshape(-1).astype(F32))
    const2 = lambda b, s: (0, 0)
    const3 = lambda b, s: (0, 0, 0)
    const4 = lambda b, s: (0, 0, 0, 0)
    return pl.pallas_call(
        functools.partial(_dn_kernel, nlevels=nlevels),
        grid=(nb, nsteps),
        in_specs=[
            pl.BlockSpec((c, 3 * w), lambda b, s: (fidx(b, s), 0)),
            pl.BlockSpec((c, 3 * w), lambda b, s: (bidx(b, s), 0)),
            pl.BlockSpec((c, LANES), lambda b, s: (fidx(b, s), 2)),
            pl.BlockSpec((c, LANES), lambda b, s: (bidx(b, s), 2)),
            pl.BlockSpec((1, LANES), const2),
            pl.BlockSpec((1, LANES), const2),
            pl.BlockSpec((2, c, c), const3),
            pl.BlockSpec((2, c, c), const3),
            pl.BlockSpec((2, nlevels, c, c), const4),
            pl.BlockSpec((c, c), const2),
        ],
        out_specs=[
            pl.BlockSpec((c, w), lambda b, s: (fidx(b, s), 0)),
            pl.BlockSpec((c, w), lambda b, s: (bidx(b, s), 0)),
        ],
        out_shape=[jax.ShapeDtypeStruct((t, w), F32), jax.ShapeDtypeStruct((t, w), F32)],
        scratch_shapes=[pltpu.VMEM((2 * nh, DN_HEAD_DIM, DN_HEAD_DIM), F32)],
        compiler_params=_params(("arbitrary", "arbitrary")),
        name="dn_scan",
    )(qkv_c, qkv_c, z_misc, z_misc, alog_row, dt_row, jnp.asarray(tri), jnp.asarray(strict), jnp.asarray(lvl),
      jnp.asarray(eye))


def _mla_q_kernel(z_ref, qn_ref, w_ref, cos_ref, sin_ref, o_ref, *, scale):
    nh = MLA_HEADS
    xn = _rms(z_ref[...].astype(F32), qn_ref[...]).astype(BF16)
    na = nh * MLA_NOPE
    nr = nh * MLA_ROPE
    qa = _dot(xn, w_ref[:, 0:na])
    qr = _dot(xn, w_ref[:, na:na + nr])
    qt = _dot(xn, w_ref[:, na + nr:na + 2 * nr])
    reps = nr // LANES
    cos = jnp.concatenate([cos_ref[...]] * reps, axis=1)
    sin = jnp.concatenate([sin_ref[...]] * reps, axis=1)
    qrr = qr * cos + qt * sin
    for h in range(nh):
        o_ref[h, :, 0:MLA_NOPE] = (qa[:, h * MLA_NOPE:(h + 1) * MLA_NOPE] * scale).astype(o_ref.dtype)
        o_ref[h, :, MLA_NOPE:MLA_QK] = (qrr[:, h * MLA_ROPE:(h + 1) * MLA_ROPE] * scale).astype(o_ref.dtype)


def mla_q(z_mla, q_norm, wq_ext, cos2, sin2, row0, nrows, scale):
    tm = _pick(nrows, ROW_TILE, BF16_SUBLANES)
    assert row0 % tm == 0
    r0 = row0 // tm
    return pl.pallas_call(
        functools.partial(_mla_q_kernel, scale=scale),
        grid=(nrows // tm,),
        in_specs=[
            pl.BlockSpec((tm, Q_LORA), lambda i: (i + r0, 0)),
            pl.BlockSpec((1, Q_LORA), lambda i: (0, 0)),
            pl.BlockSpec(wq_ext.shape, lambda i: (0, 0)),
            pl.BlockSpec((tm, LANES), lambda i: (i + r0, 0)),
            pl.BlockSpec((tm, LANES), lambda i: (i + r0, 0)),
        ],
        out_specs=pl.BlockSpec((MLA_HEADS, tm, MLA_QK), lambda i: (0, i, 0)),
        out_shape=jax.ShapeDtypeStruct((MLA_HEADS, nrows, MLA_QK), BF16),
        compiler_params=_params(("arbitrary",)),
        name="mla_q",
    )(z_mla, q_norm.reshape(1, Q_LORA), wq_ext, cos2, sin2)


def _mla_kv_kernel(z_ref, kvn_ref, w_ref, kr_ref, kt_ref, cos_ref, sin_ref, k_ref, v_ref):
    nh = MLA_HEADS
    xn = _rms(z_ref[...].astype(F32), kvn_ref[...]).astype(BF16)
    kv = _dot(xn, w_ref[...])
    r = MLA_ROPE
    kr = (kr_ref[:, 0:r] * cos_ref[:, 0:r] + kt_ref[:, 0:r] * sin_ref[:, 0:r]).astype(k_ref.dtype)
    per = MLA_NOPE + MLA_V
    for h in range(nh):
        k_ref[h, :, 0:MLA_NOPE] = kv[:, h * per:h * per + MLA_NOPE].astype(k_ref.dtype)
        k_ref[h, :, MLA_NOPE:MLA_QK] = kr
        v_ref[h] = kv[:, h * per + MLA_NOPE:(h + 1) * per].astype(v_ref.dtype)


def mla_kv(z_mla, z_misc, kv_norm, w_ukv, cos2, sin2, row0, nrows):
    tm = _pick(nrows, ROW_TILE, BF16_SUBLANES)
    assert row0 % tm == 0
    r0 = row0 // tm
    return pl.pallas_call(
        _mla_kv_kernel,
        grid=(nrows // tm,),
        in_specs=[
            pl.BlockSpec((tm, KV_LORA), lambda i: (i + r0, 1)),
            pl.BlockSpec((1, KV_LORA), lambda i: (0, 0)),
            pl.BlockSpec(w_ukv.shape, lambda i: (0, 0)),
            pl.BlockSpec((tm, LANES), lambda i: (i + r0, 0)),
            pl.BlockSpec((tm, LANES), lambda i: (i + r0, 1)),
            pl.BlockSpec((tm, LANES), lambda i: (i + r0, 0)),
            pl.BlockSpec((tm, LANES), lambda i: (i + r0, 0)),
        ],
        out_specs=[
            pl.BlockSpec((MLA_HEADS, tm, MLA_QK), lambda i: (0, i, 0)),
            pl.BlockSpec((MLA_HEADS, tm, MLA_V), lambda i: (0, i, 0)),
        ],
        out_shape=[jax.ShapeDtypeStruct((MLA_HEADS, nrows, MLA_QK), BF16),
                   jax.ShapeDtypeStruct((MLA_HEADS, nrows, MLA_V), BF16)],
        compiler_params=_params(("arbitrary",)),
        name="mla_kv",
    )(z_mla, kv_norm.reshape(1, KV_LORA), w_ukv, z_misc, z_misc, cos2, sin2)


def _softmax_update(carry, q, k, v):
    m, l, acc = carry
    s = _dot_nt(q, k)
    m_new = jnp.maximum(m, jnp.max(s, axis=-1, keepdims=True))
    p = jnp.exp2(s - m_new)
    a = jnp.exp2(m - m_new)
    l = a * l + jnp.sum(p, axis=-1, keepdims=True)
    acc = a * acc + _dot(p.astype(BF16), v)
    return m_new, l, acc


def _attn_kernel(q_ref, kc_ref, vc_ref, *rest, tk, nkl):
    if nkl:
        kl_ref, vl_ref, o_ref = rest
    else:
        (o_ref,) = rest
    q = q_ref[...]
    tq = q.shape[0]
    init = (jnp.full((tq, 1), -jnp.inf, F32), jnp.zeros((tq, 1), F32), jnp.zeros((tq, MLA_V), F32))
    carry = _softmax_update(init, q, kc_ref[...], vc_ref[...])
    if nkl:
        def body(j, c):
            off = pl.multiple_of(j * tk, tk)
            return _softmax_update(c, q, kl_ref[pl.ds(off, tk), :], vl_ref[pl.ds(off, tk), :])
        carry = lax.fori_loop(0, nkl, body, carry)
    _, l, acc = carry
    o_ref[...] = (acc / l).astype(o_ref.dtype)


def attention(q, kc, vc, kl, vl, nq_per_batch, out_rows, out_row0, lay):
    nb, c, s = lay["B"], lay["C"], lay["S"]
    tq = _pick(nq_per_batch, ATTN_Q_TILE, BF16_SUBLANES)
    nqb = nq_per_batch // tq
    assert out_row0 % tq == 0
    ob0 = out_row0 // tq
    in_specs = [
        pl.BlockSpec((None, tq, MLA_QK), lambda b, h, i: (h, b * nqb + i, 0)),
        pl.BlockSpec((None, c, MLA_QK), lambda b, h, i: (h, b, 0)),
        pl.BlockSpec((None, c, MLA_V), lambda b, h, i: (h, b, 0)),
    ]
    args = [q, kc, vc]
    tk = 0
    nkl = 0
    if kl is not None:
        tk = _pick(s, ATTN_KV_TILE, BF16_SUBLANES)
        nkl = s // tk
        in_specs += [
            pl.BlockSpec((None, s, MLA_QK), lambda b, h, i: (h, b, 0)),
            pl.BlockSpec((None, s, MLA_V), lambda b, h, i: (h, b, 0)),
        ]
        args += [kl, vl]
    return pl.pallas_call(
        functools.partial(_attn_kernel, tk=tk, nkl=nkl),
        grid=(nb, MLA_HEADS, nqb),
        in_specs=in_specs,
        out_specs=pl.BlockSpec((tq, MLA_V), lambda b, h, i: (ob0 + b * nqb + i, h)),
        out_shape=jax.ShapeDtypeStruct((out_rows, MLA_HEADS * MLA_V), BF16),
        compiler_params=_params(("arbitrary", "arbitrary", "arbitrary")),
        name="attention_lat" if kl is not None else "attention_ctx",
    )(*args)


def _merge1_kernel(bra_ref, of_ref, ob_ref, gate_ref, att_ref, g0_ref, g1_ref, g2_ref, dnn_ref, wb_ref, y_ref,
                   brb_ref):
    hd = DN_HEAD_DIM
    o = of_ref[...] + ob_ref[...]
    for h in range(DN_HEADS):
        cs = slice(h * hd, (h + 1) * hd)
        oh = o[:, cs]
        yn = oh * lax.rsqrt(jnp.mean(oh * oh, axis=-1, keepdims=True) + EPS) * dnn_ref[:, cs]
        brb_ref[:, cs] = (yn * _silu(gate_ref[:, cs].astype(F32))).astype(brb_ref.dtype)
    y = jax.nn.sigmoid(g0_ref[...].astype(F32)) * _dot(bra_ref[...], wb_ref[0])
    y = y + jax.nn.sigmoid(g1_ref[...].astype(F32)) * _dot(brb_ref[...], wb_ref[1])
    y = y + jax.nn.sigmoid(g2_ref[...].astype(F32)) * _dot(att_ref[...], wb_ref[2])
    y_ref[...] = y.astype(y_ref.dtype)


def merge_branches(br_a, o_f, o_b, z_main, attn, dn_norm, w_branch, lay):
    t = br_a.shape[0]
    d = lay["D"]
    w = BRANCH_W
    tm = _pick(lay["n_ctx"], ROW_TILE, BF16_SUBLANES)
    tn = _pick(d, COL_TILE)
    nn = d // tn
    gate0 = 6 * w // tn

    def gspec(j):
        return pl.BlockSpec((tm, tn), lambda n, i: (i, gate0 + j * nn + n))

    row = lambda n, i: (i, 0)
    return pl.pallas_call(
        _merge1_kernel,
        grid=(nn, t // tm),
        in_specs=[
            pl.BlockSpec((tm, w), row),
            pl.BlockSpec((tm, w), row),
            pl.BlockSpec((tm, w), row),
            pl.BlockSpec((tm, w), lambda n, i: (i, 5)),
            pl.BlockSpec((tm, w), row),
            gspec(0), gspec(1), gspec(2),
            pl.BlockSpec((1, w), lambda n, i: (0, 0)),
            pl.BlockSpec((3, w, tn), lambda n, i: (0, 0, n)),
        ],
        out_specs=pl.BlockSpec((tm, tn), lambda n, i: (i, n)),
        out_shape=jax.ShapeDtypeStruct((t, d), BF16),
        scratch_shapes=[pltpu.VMEM((tm, w), BF16)],
        compiler_params=_params(("arbitrary", "arbitrary")),
        name="merge_branches",
    )(br_a, o_f, o_b, z_main, attn, z_main, z_main, z_main, jnp.tile(dn_norm.astype(F32), DN_HEADS).reshape(1, w),
      w_branch)


def _merge2_kernel(y_ref, wo_ref, x_ref, mod_ref, n2_ref, wr_ref, xo_ref, h2_ref, lg_ref, *, d):
    m = mod_ref[...]
    mix = _dot(y_ref[...], wo_ref[...])
    xn = x_ref[...] + _mod_part(m, 2, d) * mix
    xo_ref[...] = xn
    h2 = _rms(xn, n2_ref[...]) * (1.0 + _mod_part(m, 4, d)) + _mod_part(m, 3, d)
    h2_ref[...] = h2
    lg_ref[...] = jnp.dot(h2, wr_ref[...], precision=lax.Precision.HIGHEST, preferred_element_type=F32)


def out_proj_residual(y, w_out, x, mod, layer, norm2, w_router_pad, lay):
    t, d = x.shape
    tm = _pick(lay["n_ctx"], RES_ROW_TILE, BF16_SUBLANES)
    row = lambda i: (i, 0)
    const = lambda i: (0, 0)
    return pl.pallas_call(
        functools.partial(_merge2_kernel, d=d),
        grid=(t // tm,),
        in_specs=[
            pl.BlockSpec((tm, d), row),
            pl.BlockSpec((d, d), const),
            pl.BlockSpec((tm, d), row),
            _mod_spec(layer, tm, lay),
            pl.BlockSpec((1, d), const),
            pl.BlockSpec((d, LANES), const),
        ],
        out_specs=[pl.BlockSpec((tm, d), row), pl.BlockSpec((tm, d), row), pl.BlockSpec((tm, LANES), row)],
        out_shape=[jax.ShapeDtypeStruct((t, d), F32), jax.ShapeDtypeStruct((t, d), F32),
                   jax.ShapeDtypeStruct((t, LANES), F32)],
        compiler_params=_params(("arbitrary",)),
        name="out_proj_residual",
    )(y, w_out, x, mod, norm2.reshape(1, d), w_router_pad)


def _topk_kernel(lg_ref, bias_ref, idx_ref, wt_ref):
    ne, ng = N_EXPERTS, N_GROUPS
    per = ne // ng
    lt = lg_ref[...].T
    tm = lt.shape[1]
    sc = jax.nn.sigmoid(lt[0:ne])
    ch = sc + bias_ref[...]
    ch3 = ch.reshape(ng, per, tm)
    neg = -jnp.inf
    sub = lax.broadcasted_iota(jnp.int32, (ng, per, tm), 1)
    m1 = jnp.max(ch3, axis=1, keepdims=True)
    i1 = jnp.min(jnp.where(ch3 == m1, sub, per), axis=1, keepdims=True)
    m2 = jnp.max(jnp.where(sub == i1, neg, ch3), axis=1, keepdims=True)
    gs = (m1 + m2).reshape(ng, tm)
    giota = lax.broadcasted_iota(jnp.int32, (ng, tm), 0)
    sel = jnp.zeros((ng, tm), F32)
    cur = gs
    for _ in range(TOPK_GROUPS):
        m = jnp.max(cur, axis=0, keepdims=True)
        ix = jnp.min(jnp.where(cur == m, giota, ng), axis=0, keepdims=True)
        hit = giota == ix
        sel = jnp.where(hit, 1.0, sel)
        cur = jnp.where(hit, neg, cur)
    masked = jnp.where(sel.reshape(ng, 1, tm) > 0, ch3, neg).reshape(ne, tm)
    eiota = lax.broadcasted_iota(jnp.int32, (ne, tm), 0)
    idxs, ws = [], []
    for _ in range(TOP_K):
        m = jnp.max(masked, axis=0, keepdims=True)
        ix = jnp.min(jnp.where(masked == m, eiota, ne), axis=0, keepdims=True)
        hit = eiota == ix
        ws.append(jnp.sum(jnp.where(hit, sc, 0.0), axis=0, keepdims=True))
        idxs.append(ix)
        masked = jnp.where(hit, neg, masked)
    wall = jnp.concatenate(ws, axis=0)
    idx_ref[...] = jnp.concatenate(idxs, axis=0)
    wt_ref[...] = wall / jnp.sum(wall, axis=0, keepdims=True) * ROUTED_SCALE


def route_topk(logits, bias):
    t = logits.shape[0]
    tm = _pick(t, ROW_TILE)
    return pl.pallas_call(
        _topk_kernel,
        grid=(t // tm,),
        in_specs=[pl.BlockSpec((tm, LANES), lambda i: (i, 0)), pl.BlockSpec((N_EXPERTS, 1), lambda i: (0, 0))],
        out_specs=[pl.BlockSpec((TOP_K, tm), lambda i: (0, i)), pl.BlockSpec((TOP_K, tm), lambda i: (0, i))],
        out_shape=[jax.ShapeDtypeStruct((TOP_K, t), jnp.int32), jax.ShapeDtypeStruct((TOP_K, t), F32)],
        compiler_params=_params(("arbitrary",)),
        name="route_topk",
    )(logits, bias.astype(F32).reshape(N_EXPERTS, 1))


def moe_layout(idx_t, bm):
    k, t = idx_t.shape
    m = t * k
    n_blocks = -(-m // bm) + N_EXPERTS
    flat_e = idx_t.T.reshape(-1)
    order = jnp.argsort(flat_e).astype(jnp.int32)
    se = flat_e[order]
    counts = jnp.bincount(flat_e, length=N_EXPERTS).astype(jnp.int32)
    padded = (counts + bm - 1) // bm * bm
    pend = jnp.cumsum(padded)
    pstart = pend - padded
    ustart = jnp.cumsum(counts) - counts
    dest = pstart[se] + jnp.arange(m, dtype=jnp.int32) - ustart[se]
    row_tok = jnp.zeros((n_blocks * bm,), jnp.int32).at[dest].set(order // k)
    pos = jnp.zeros((m,), jnp.int32).at[order].set(dest).reshape(t, k)
    n_used = (pend[-1] // bm).astype(jnp.int32)
    blk = jnp.minimum(jnp.arange(n_blocks, dtype=jnp.int32), n_used - 1) * bm
    blk_e = jnp.minimum(jnp.searchsorted(pend, blk, side="right"), N_EXPERTS - 1).astype(jnp.int32)
    return row_tok, pos, blk_e, n_used.reshape(1), n_blocks


def _moe_kernel(be_ref, nu_ref, x_ref, w1_ref, w3_ref, w2_ref, y_ref, w13_s, w2_s, *, ed):
    i = pl.program_id(0)
    prev = be_ref[jnp.maximum(i - 1, 0)]
    fresh = jnp.logical_or(i == 0, be_ref[i] != prev)

    @pl.when(fresh)
    def _():
        w13_s[:, 0:ed] = w1_ref[...].astype(BF16)
        w13_s[:, ed:2 * ed] = w3_ref[...].astype(BF16)
        w2_s[...] = w2_ref[...].astype(BF16)

    @pl.when(i < nu_ref[0])
    def _():
        a = _dot(x_ref[...], w13_s[...])
        act = (_silu(a[:, 0:ed]) * a[:, ed:2 * ed]).astype(BF16)
        y_ref[...] = _dot(act, w2_s[...]).astype(y_ref.dtype)


def moe_experts(xs, blk_e, n_used, w1, w3, w2, bm):
    n_rows, d = xs.shape
    ed = w1.shape[-1]
    n_blocks = n_rows // bm
    grid_spec = pltpu.PrefetchScalarGridSpec(
        num_scalar_prefetch=2,
        grid=(n_blocks,),
        in_specs=[
            pl.BlockSpec((bm, d), lambda i, be, nu: (i, 0)),
            pl.BlockSpec((None, d, ed), lambda i, be, nu: (be[i], 0, 0)),
            pl.BlockSpec((None, d, ed), lambda i, be, nu: (be[i], 0, 0)),
            pl.BlockSpec((None, ed, d), lambda i, be, nu: (be[i], 0, 0)),
        ],
        out_specs=pl.BlockSpec((bm, d), lambda i, be, nu: (i, 0)),
        scratch_shapes=[pltpu.VMEM((d, 2 * ed), BF16), pltpu.VMEM((ed, d), BF16)],
    )
    return pl.pallas_call(
        functools.partial(_moe_kernel, ed=ed),
        grid_spec=grid_spec,
        out_shape=jax.ShapeDtypeStruct((n_rows, d), BF16),
        compiler_params=_params(("arbitrary",)),
        name="moe_experts",
    )(blk_e, n_used, xs, w1, w3, w2)


def _shared_kernel(h2_ref, w13_ref, w2_ref, routed_ref, x_ref, mod_ref, ng_ref, nmod_ref, xo_ref, ho_ref, *, d, ed,
                   final):
    m = mod_ref[...]
    a = _dot(h2_ref[...].astype(BF16), w13_ref[...])
    act = (_silu(a[:, 0:ed]) * a[:, ed:2 * ed]).astype(BF16)
    f = routed_ref[...] + _dot(act, w2_ref[...])
    xn = x_ref[...] + _mod_part(m, 5, d) * f
    xo_ref[...] = xn
    y = _rms(xn, ng_ref[...])
    if not final:
        nm = nmod_ref[...]
        y = y * (1.0 + _mod_part(nm, 1, d)) + _mod_part(nm, 0, d)
    ho_ref[...] = y.astype(ho_ref.dtype)


def shared_residual(h2, sw13, sw2, routed, x, mod, layer, next_g, final, lay):
    t, d = x.shape
    ed = sw2.shape[0]
    tm = _pick(lay["n_ctx"], RES_ROW_TILE, BF16_SUBLANES)
    row = lambda i: (i, 0)
    const = lambda i: (0, 0)
    next_layer = layer if final else layer + 1
    return pl.pallas_call(
        functools.partial(_shared_kernel, d=d, ed=ed, final=final),
        grid=(t // tm,),
        in_specs=[
            pl.BlockSpec((tm, d), row),
            pl.BlockSpec((d, 2 * ed), const),
            pl.BlockSpec((ed, d), const),
            pl.BlockSpec((tm, d), row),
            pl.BlockSpec((tm, d), row),
            _mod_spec(layer, tm, lay),
            pl.BlockSpec((1, d), const),
            _mod_spec(next_layer, tm, lay),
        ],
        out_specs=[pl.BlockSpec((tm, d), row), pl.BlockSpec((tm, d), row)],
        out_shape=[jax.ShapeDtypeStruct((t, d), F32), jax.ShapeDtypeStruct((t, d), F32 if final else BF16)],
        compiler_params=_params(("arbitrary",)),
        name="shared_residual",
    )(h2, sw13, sw2, routed, x, mod, next_g.reshape(1, d), mod)


def _rope_tables(lay):
    half = MLA_ROPE // 4
    s = lay["S"]
    freq = ROPE_BASE ** (-np.arange(half, dtype=np.float64) / half)
    tpos = np.arange(s)
    ang_r = (tpos // GRID_W)[:, None] * freq
    ang_c = (tpos % GRID_W)[:, None] * freq
    ang = np.concatenate([ang_r, ang_r, ang_c, ang_c], axis=1)
    cos_l = np.tile(np.cos(ang), (lay["B"], 2))
    sin_l = np.tile(np.sin(ang), (lay["B"], 2))
    cos = np.concatenate([np.ones((lay["n_ctx"], 2 * MLA_ROPE)), cos_l], axis=0).astype(np.float32)
    sin = np.concatenate([np.zeros((lay["n_ctx"], 2 * MLA_ROPE)), sin_l], axis=0).astype(np.float32)
    return jnp.asarray(cos), jnp.asarray(sin)


def _rot_cols(w):
    q = MLA_ROPE // 4
    a, b, c, e = w[..., 0:q], w[..., q:2 * q], w[..., 2 * q:3 * q], w[..., 3 * q:4 * q]
    return jnp.concatenate([-b, a, -e, c], axis=-1)


def _prep_layer_weights(w_in, w_uq, w_ukv, w_branch, w_out, w_router, sw1, sw3, sw2, d):
    w = BRANCH_W
    o_small = 6 * w
    o_cq = o_small + 4 * DN_HEADS
    o_kr = o_cq + Q_LORA + KV_LORA
    o_gate = o_kr + MLA_ROPE
    w_main = jnp.concatenate([w_in[:, :o_small], w_in[:, o_gate:]], axis=1).astype(BF16)
    w_mla = w_in[:, o_cq:o_kr].astype(BF16)
    w_kr = w_in[:, o_kr:o_gate]
    zpad = jnp.zeros((d, LANES - MLA_ROPE), F32)
    w_misc = jnp.concatenate(
        [w_kr, zpad, _rot_cols(w_kr), zpad, w_in[:, o_small:o_cq], jnp.zeros((d, LANES - 4 * DN_HEADS), F32)],
        axis=1).astype(BF16)
    uq = w_uq.reshape(Q_LORA, MLA_HEADS, MLA_QK)
    uq_r = uq[..., MLA_NOPE:]
    wq_ext = jnp.concatenate(
        [uq[..., :MLA_NOPE].reshape(Q_LORA, -1), uq_r.reshape(Q_LORA, -1), _rot_cols(uq_r).reshape(Q_LORA, -1)],
        axis=1).astype(BF16)
    wr_pad = jnp.zeros((d, LANES), F32).at[:, :N_EXPERTS].set(w_router.astype(F32))
    return dict(w_main=w_main, w_mla=w_mla, w_misc=w_misc, wq_ext=wq_ext, w_ukv=w_ukv.astype(BF16),
                w_branch=w_branch.astype(BF16), w_out=w_out.astype(BF16), wr_pad=wr_pad,
                sw13=jnp.concatenate([sw1, sw3], axis=1).astype(BF16), sw2=sw2.astype(BF16))


def kernel(x, c, ctx, c_ctx, w_mod, b_mod, norm1, norm2, w_in, gm_norm, gm_ws, gm_bs, dn_conv, dn_a_log,
           dn_dt_bias, dn_norm, mla_q_norm, mla_kv_norm, mla_w_uq, mla_w_ukv, w_branch, w_out, moe_router,
           moe_bias, moe_w1, moe_w3, moe_w2, shared_w1, shared_w3, shared_w2, final_norm):
    nb, s, d = x.shape
    cl = ctx.shape[1]
    depth = w_mod.shape[0]
    n_ctx = nb * cl
    t = n_ctx + nb * s
    lay = dict(B=nb, S=s, C=cl, D=d, n_ctx=n_ctx, T=t)
    assert nb + 1 <= SUBLANES and s % GRID_W == 0

    xs = jnp.concatenate([ctx.reshape(n_ctx, d), x.reshape(nb * s, d)], axis=0).astype(F32)
    cvec = jnp.zeros((SUBLANES, d), F32).at[0].set(c_ctx.astype(F32)).at[1:1 + nb].set(c.astype(F32))
    mod = modulation(cvec, w_mod, b_mod)
    cos2, sin2 = _rope_tables(lay)
    scale = MLA_QK ** -0.5 * LOG2E

    h = prenorm(xs, norm1[0], mod, 0, lay)
    out = None
    for i in range(depth):
        last = i == depth - 1
        wts = _prep_layer_weights(w_in[i], mla_w_uq[i], mla_w_ukv[i], w_branch[i], w_out[i], moe_router[i],
                                  shared_w1[i], shared_w3[i], shared_w2[i], d)
        z_main = matmul(h, wts["w_main"], BF16, "in_proj_main")
        z_mla = matmul(h, wts["w_mla"], BF16, "in_proj_mla")
        z_misc = matmul(h, wts["w_misc"], F32, "in_proj_misc")

        br_a = gmlp(z_main, gm_norm[i], gm_ws[i], gm_bs[i], lay)

        qkv_c = dn_short_conv(z_main, dn_conv[i], lay)
        o_f, o_b = dn_scan(qkv_c, z_misc, dn_a_log[i], dn_dt_bias[i], lay)

        kc, vc = mla_kv(z_mla, z_misc, mla_kv_norm[i], wts["w_ukv"], cos2, sin2, 0, n_ctx)
        kl, vl = mla_kv(z_mla, z_misc, mla_kv_norm[i], wts["w_ukv"], cos2, sin2, n_ctx, nb * s)
        ql = mla_q(z_mla, mla_q_norm[i], wts["wq_ext"], cos2, sin2, n_ctx, nb * s, scale)
        attn = attention(ql, kc, vc, kl, vl, s, t, n_ctx, lay)
        qc = mla_q(z_mla, mla_q_norm[i], wts["wq_ext"], cos2, sin2, 0, n_ctx, scale)
        attn_c = attention(qc, kc, vc, None, None, cl, n_ctx, 0, lay)
        attn = lax.dynamic_update_slice(attn, attn_c, (0, 0))

        y = merge_branches(br_a, o_f, o_b, z_main, attn, dn_norm[i], wts["w_branch"], lay)
        xs, h2, logits = out_proj_residual(y, wts["w_out"], xs, mod, i, norm2[i], wts["wr_pad"], lay)

        idx_t, wt_t = route_topk(logits, moe_bias[i])
        bm = MOE_ROW_TILE
        row_tok, pos, blk_e, n_used, _ = moe_layout(idx_t, bm)
        x_sorted = h2[row_tok].astype(BF16)
        y_sorted = moe_experts(x_sorted, blk_e, n_used, moe_w1[i], moe_w3[i], moe_w2[i], bm)
        routed = jnp.sum(y_sorted[pos].astype(F32) * wt_t.T[:, :, None], axis=1)

        next_g = final_norm if last else norm1[i + 1]
        xs, h = shared_residual(h2, wts["sw13"], wts["sw2"], routed, xs, mod, i, next_g, last, lay)
        out = h
    return out[n_ctx:].reshape(nb, s, d).astype(x.dtype)
```

```python
import functools
import math

import numpy as np
import jax
import jax.numpy as jnp
from jax import lax
from jax.experimental import pallas as pl
from jax.experimental.pallas import tpu as pltpu

F32 = jnp.float32
BF16 = jnp.bfloat16

GRID_W = 64
EPS = 1e-6
GM_CHUNK = 128
GM_GROUPS = 8
DN_HEADS = 8
DN_HEAD_DIM = 128
DN_CHUNK = 64
DN_CONV = 5
MLA_HEADS = 8
MLA_NOPE = 128
MLA_ROPE = 64
MLA_V = 128
MLA_QK = MLA_NOPE + MLA_ROPE
Q_LORA = 512
KV_LORA = 512
ROPE_BASE = 10000.0
BRANCH_W = 1024
N_EXPERTS = 64
TOP_K = 8
N_GROUPS = 8
TOPK_GROUPS = 4
ROUTED_SCALE = 2.5
LOG2E = 1.4426950408889634

LANES = 128
SUBLANES = 8
BF16_SUBLANES = 16
VMEM_LIMIT_MB = 56

ROW_TILE = 512
RES_ROW_TILE = 256
CONV_ROW_TILE = 256
COL_TILE = 1024
MOE_ROW_TILE = 256
ATTN_Q_TILE = 512
ATTN_KV_TILE = 1024


def _pick(n, pref, mult=LANES):
    if n <= pref:
        return n
    for t in range(pref - pref % mult, 0, -mult):
        if n % t == 0:
            return t
    return n


def _params(sem, mb=VMEM_LIMIT_MB):
    return pltpu.CompilerParams(dimension_semantics=sem, vmem_limit_bytes=mb * 1024 * 1024)


def _silu(x):
    return x * jax.nn.sigmoid(x)


def _gelu(x):
    return 0.5 * x * (1.0 + jnp.tanh(0.7978845608028654 * (x + 0.044715 * x * x * x)))


def _dot(a, b):
    return jnp.dot(a, b, preferred_element_type=F32)


def _dot_nt(a, b):
    return lax.dot_general(a, b, (((1,), (1,)), ((), ())), preferred_element_type=F32)


def _dot_tn(a, b):
    return lax.dot_general(a, b, (((0,), (0,)), ((), ())), preferred_element_type=F32)


def _mod_kernel(c_ref, w_ref, b_ref, o_ref):
    s = _silu(c_ref[...])
    o_ref[...] = _dot(s.astype(BF16), w_ref[...].astype(BF16)) + b_ref[...]


def modulation(cvec, w_mod, b_mod):
    nl, d, n6 = w_mod.shape
    tn = _pick(n6, COL_TILE)
    out = pl.pallas_call(
        _mod_kernel,
        grid=(nl, n6 // tn),
        in_specs=[
            pl.BlockSpec((SUBLANES, d), lambda l, n: (0, 0)),
            pl.BlockSpec((None, d, tn), lambda l, n: (l, 0, n)),
            pl.BlockSpec((None, 1, tn), lambda l, n: (l, 0, n)),
        ],
        out_specs=pl.BlockSpec((None, SUBLANES, tn), lambda l, n: (l, 0, n)),
        out_shape=jax.ShapeDtypeStruct((nl, SUBLANES, n6), F32),
        compiler_params=_params(("arbitrary", "arbitrary")),
        name="modulation",
    )(cvec, w_mod, b_mod.reshape(nl, 1, n6))
    return out.reshape(nl, SUBLANES, 1, n6)


def _seg_of_block(i, tm, lay):
    nctx_blk = lay["n_ctx"] // tm
    lat_bps = lay["S"] // tm
    return jnp.where(i < nctx_blk, 0, 1 + (i - nctx_blk) // lat_bps)


def _mod_spec(layer, tm, lay, row0_blk=0):
    n6 = 6 * lay["D"]
    return pl.BlockSpec((None, None, 1, n6), lambda i: (layer, _seg_of_block(i + row0_blk, tm, lay), 0, 0))


def _mod_part(m, k, d):
    return m[:, k * d:(k + 1) * d]


def _rms(x, g):
    return x * lax.rsqrt(jnp.mean(x * x, axis=-1, keepdims=True) + EPS) * g


def _prenorm_kernel(x_ref, g_ref, mod_ref, o_ref, *, d):
    m = mod_ref[...]
    y = _rms(x_ref[...], g_ref[...])
    o_ref[...] = (y * (1.0 + _mod_part(m, 1, d)) + _mod_part(m, 0, d)).astype(o_ref.dtype)


def prenorm(x, g, mod, layer, lay):
    t, d = x.shape
    tm = _pick(lay["n_ctx"], ROW_TILE, SUBLANES)
    return pl.pallas_call(
        functools.partial(_prenorm_kernel, d=d),
        grid=(t // tm,),
        in_specs=[
            pl.BlockSpec((tm, d), lambda i: (i, 0)),
            pl.BlockSpec((1, d), lambda i: (0, 0)),
            _mod_spec(layer, tm, lay),
        ],
        out_specs=pl.BlockSpec((tm, d), lambda i: (i, 0)),
        out_shape=jax.ShapeDtypeStruct((t, d), BF16),
        compiler_params=_params(("arbitrary",)),
        name="prenorm",
    )(x, g.reshape(1, d), mod)


def _mm_kernel(x_ref, w_ref, o_ref):
    o_ref[...] = _dot(x_ref[...], w_ref[...]).astype(o_ref.dtype)


def matmul(x, w, out_dtype, name):
    t, k = x.shape
    n = w.shape[1]
    tm = _pick(t, ROW_TILE, BF16_SUBLANES)
    tn = _pick(n, COL_TILE)
    return pl.pallas_call(
        _mm_kernel,
        grid=(n // tn, t // tm),
        in_specs=[
            pl.BlockSpec((tm, k), lambda j, i: (i, 0)),
            pl.BlockSpec((k, tn), lambda j, i: (0, j)),
        ],
        out_specs=pl.BlockSpec((tm, tn), lambda j, i: (i, j)),
        out_shape=jax.ShapeDtypeStruct((t, n), out_dtype),
        compiler_params=_params(("arbitrary", "arbitrary")),
        name=name,
    )(x, w)


def _gmlp_kernel(u_ref, v_ref, gn_ref, ws_ref, bias_ref, o_ref, *, nchunk):
    for j in range(nchunk):
        rs = slice(j * GM_CHUNK, (j + 1) * GM_CHUNK)
        v = _gelu(v_ref[rs, :].astype(F32))
        vb = _rms(v, gn_ref[...]).astype(BF16)
        u = _gelu(u_ref[rs, :].astype(F32))
        for g in range(GM_GROUPS):
            cs = slice(g * LANES, (g + 1) * LANES)
            mixed = _dot(ws_ref[g], vb[:, cs]) + bias_ref[:, cs]
            o_ref[rs, cs] = (u[:, cs] * mixed).astype(o_ref.dtype)


def gmlp(z_main, gm_norm, gm_ws, gm_bs, lay):
    t = z_main.shape[0]
    w = BRANCH_W
    tm = _pick(lay["n_ctx"], ROW_TILE, GM_CHUNK)
    bias = jnp.repeat(gm_bs.T.astype(F32), w // GM_GROUPS, axis=1)
    return pl.pallas_call(
        functools.partial(_gmlp_kernel, nchunk=tm // GM_CHUNK),
        grid=(t // tm,),
        in_specs=[
            pl.BlockSpec((tm, w), lambda i: (i, 0)),
            pl.BlockSpec((tm, w), lambda i: (i, 1)),
            pl.BlockSpec((1, w), lambda i: (0, 0)),
            pl.BlockSpec((GM_GROUPS, GM_CHUNK, GM_CHUNK), lambda i: (0, 0, 0)),
            pl.BlockSpec((GM_CHUNK, w), lambda i: (0, 0)),
        ],
        out_specs=pl.BlockSpec((tm, w), lambda i: (i, 0)),
        out_shape=jax.ShapeDtypeStruct((t, w), BF16),
        compiler_params=_params(("arbitrary",)),
        name="gmlp",
    )(z_main, z_main, gm_norm.reshape(1, w), gm_ws.astype(BF16), bias)


def _conv_kernel(x_ref, hp_ref, hn_ref, w_ref, o_ref, xe_ref, *, tm, nctx_blk, ctx_bps, lat_bps):
    j = pl.program_id(0)
    i = pl.program_id(1)
    li = i - nctx_blk
    seg_start = jnp.where(i < nctx_blk, (i % ctx_bps) == 0, (li % lat_bps) == 0)
    seg_end = jnp.where(i < nctx_blk, ((i + 1) % ctx_bps) == 0, ((li + 1) % lat_bps) == 0)
    halo = BF16_SUBLANES
    xe_ref[0:halo, :] = jnp.where(seg_start, 0.0, hp_ref[...].astype(F32))
    xe_ref[halo:halo + tm, :] = x_ref[...].astype(F32)
    xe_ref[halo + tm:2 * halo + tm, :] = jnp.where(seg_end, 0.0, hn_ref[...].astype(F32))
    base = halo - DN_CONV // 2
    acc = w_ref[0:1, :] * xe_ref[base:base + tm, :]
    for tap in range(1, DN_CONV):
        acc = acc + w_ref[tap:tap + 1, :] * xe_ref[base + tap:base + tap + tm, :]
    y = _silu(acc)
    unit = j < 2
    for h in range(DN_HEADS):
        cs = slice(h * DN_HEAD_DIM, (h + 1) * DN_HEAD_DIM)
        yh = y[:, cs]
        nrm = yh * lax.rsqrt(jnp.sum(yh * yh, axis=-1, keepdims=True) + EPS)
        o_ref[:, cs] = jnp.where(unit, nrm, yh).astype(o_ref.dtype)


def dn_short_conv(z_main, conv_w, lay):
    t = z_main.shape[0]
    w = BRANCH_W
    tm = _pick(lay["C"], CONV_ROW_TILE, BF16_SUBLANES)
    halo = BF16_SUBLANES
    hb = tm // halo
    nhalo = t // halo
    wpad = jnp.zeros((SUBLANES, 3 * w), F32).at[:DN_CONV].set(conv_w.astype(F32))
    kern = functools.partial(_conv_kernel, tm=tm, nctx_blk=lay["n_ctx"] // tm, ctx_bps=lay["C"] // tm,
                             lat_bps=lay["S"] // tm)
    return pl.pallas_call(
        kern,
        grid=(3, t // tm),
        in_specs=[
            pl.BlockSpec((tm, w), lambda j, i: (i, 2 + j)),
            pl.BlockSpec((halo, w), lambda j, i: (jnp.maximum(i * hb - 1, 0), 2 + j)),
            pl.BlockSpec((halo, w), lambda j, i: (jnp.minimum((i + 1) * hb, nhalo - 1), 2 + j)),
            pl.BlockSpec((SUBLANES, w), lambda j, i: (0, j)),
        ],
        out_specs=pl.BlockSpec((tm, w), lambda j, i: (i, j)),
        out_shape=jax.ShapeDtypeStruct((t, 3 * w), BF16),
        scratch_shapes=[pltpu.VMEM((tm + 2 * halo, w), F32)],
        compiler_params=_params(("arbitrary", "arbitrary")),
        name="dn_conv",
    )(z_main, z_main, z_main, wpad)


def _dn_masks():
    c = DN_CHUNK
    r = np.arange(c)[:, None]
    s = np.arange(c)[None, :]
    tri = np.stack([r >= s, r <= s]).astype(np.float32)
    strict = np.stack([r > s, r < s]).astype(np.float32)
    lv = []
    b = 1
    while b < c:
        same = (r // (2 * b)) == (s // (2 * b))
        lo = same & ((r // b) % 2 == 1) & ((s // b) % 2 == 0)
        up = same & ((r // b) % 2 == 0) & ((s // b) % 2 == 1)
        lv.append(np.stack([lo, up]))
        b *= 2
    lvl = np.stack(lv, axis=1).astype(np.float32)
    return tri, strict, lvl, np.eye(c, dtype=np.float32)


def _dn_kernel(qf_ref, qb_ref, abf_ref, abb_ref, alog_ref, dt_ref, tri_ref, strict_ref, lvl_ref, eye_ref,
               of_ref, ob_ref, s_ref, *, nlevels):
    step = pl.program_id(1)

    @pl.when(step == 0)
    def _():
        s_ref[...] = jnp.zeros(s_ref.shape, F32)

    hd = DN_HEAD_DIM
    w = DN_HEADS * hd
    nh = DN_HEADS
    cc = DN_CHUNK
    eye = eye_ref[...]
    inst = []
    for d, (x_ref, ab_ref) in enumerate(((qf_ref, abf_ref), (qb_ref, abb_ref))):
        ab = ab_ref[...]
        g_all = -jnp.exp(alog_ref[...]) * jax.nn.softplus(ab + dt_ref[...])
        beta_all = jax.nn.sigmoid(ab)
        gam_c = jnp.dot(tri_ref[d], g_all, precision=lax.Precision.HIGHEST, preferred_element_type=F32)
        gam_r = gam_c.T
        for h in range(nh):
            lane = d * nh + h
            inst.append(dict(
                d=d, lane=lane,
                q=x_ref[:, h * hd:(h + 1) * hd].astype(F32),
                k=x_ref[:, w + h * hd:w + (h + 1) * hd].astype(F32),
                v=x_ref[:, 2 * w + h * hd:2 * w + (h + 1) * hd].astype(F32),
                gc=gam_c[:, lane:lane + 1],
                gr=gam_r[lane:lane + 1, :],
                bc=beta_all[:, 2 * nh + lane:2 * nh + lane + 1],
                st=s_ref[lane]))
    for it in inst:
        tri = tri_ref[it["d"]]
        diff = it["gc"] - it["gr"]
        it["dec"] = jnp.where(tri > 0, jnp.exp(jnp.where(tri > 0, diff, 0.0)), 0.0)
        it["kb"] = it["k"] * it["bc"]
        it["qs"] = it["q"] * (hd ** -0.5)
    for it in inst:
        kq = _dot_nt(jnp.concatenate([it["kb"], it["qs"]], axis=0).astype(BF16), it["k"].astype(BF16))
        it["a"] = kq[:cc] * it["dec"] * strict_ref[it["d"]]
        it["attn"] = (kq[cc:] * it["dec"]).astype(BF16)
    for it in inst:
        it["x"] = eye - it["a"] * lvl_ref[it["d"], 0]
    for lv in range(1, nlevels):
        for it in inst:
            it["xb"] = it["x"].astype(BF16)
            it["p"] = _dot(it["xb"], (it["a"] * lvl_ref[it["d"], lv]).astype(BF16)).astype(BF16)
        for it in inst:
            it["x"] = it["x"] - _dot(it["p"], it["xb"])
    for it in inst:
        eg = jnp.exp(it["gc"])
        rhs = jnp.concatenate([it["kb"] * eg, it["v"] * it["bc"]], axis=1).astype(BF16)
        it["sol"] = _dot(it["x"].astype(BF16), rhs)
        last = cc - 1 if it["d"] == 0 else 0
        g_last = it["gc"][last:last + 1, :]
        it["gtot"] = jnp.exp(g_last)
        it["kd"] = (it["k"] * jnp.exp(g_last - it["gc"])).astype(BF16)
        it["qd"] = it["qs"] * eg
    for it in inst:
        wq = jnp.concatenate([it["sol"][:, :hd], it["qd"]], axis=0).astype(BF16)
        it["r"] = _dot(wq, it["st"].astype(BF16))
    for it in inst:
        it["vn"] = (it["sol"][:, hd:] - it["r"][:cc]).astype(BF16)
    for it in inst:
        it["o"] = it["r"][cc:] + _dot(it["attn"], it["vn"])
        it["sn"] = it["st"] * it["gtot"] + _dot_tn(it["kd"], it["vn"])
    for it in inst:
        h = it["lane"] % nh
        o_ref = of_ref if it["d"] == 0 else ob_ref
        o_ref[:, h * hd:(h + 1) * hd] = it["o"]
        s_ref[it["lane"]] = it["sn"]


def dn_scan(qkv_c, z_misc, a_log, dt_bias, lay):
    t = qkv_c.shape[0]
    w = BRANCH_W
    c = DN_CHUNK
    nb = lay["B"]
    ncc = lay["C"] // c
    ncl = lay["S"] // c
    nsteps = ncc + ncl

    def fidx(b, s):
        return jnp.where(s < ncc, b * ncc + s, nb * ncc + b * ncl + (s - ncc))

    def bidx(b, s):
        return jnp.where(s < ncc, b * ncc + (ncc - 1 - s), nb * ncc + b * ncl + (ncl - 1 - (s - ncc)))

    tri, strict, lvl, eye = _dn_masks()
    nlevels = lvl.shape[1]
    nh = DN_HEADS
    alog_row = jnp.zeros((1, LANES), F32).at[0, :2 * nh].set(a_log.reshape(-1).astype(F32))
    dt_row = jnp.zeros((1, LANES), F32).at[0, :2 * nh].set(dt_bias.reshape(-1).astype(F32))
    const2 = lambda b, s: (0, 0)
    const3 = lambda b, s: (0, 0, 0)
    const4 = lambda b, s: (0, 0, 0, 0)
    return pl.pallas_call(
        functools.partial(_dn_kernel, nlevels=nlevels),
        grid=(nb, nsteps),
        in_specs=[
            pl.BlockSpec((c, 3 * w), lambda b, s: (fidx(b, s), 0)),
            pl.BlockSpec((c, 3 * w), lambda b, s: (bidx(b, s), 0)),
            pl.BlockSpec((c, LANES), lambda b, s: (fidx(b, s), 2)),
            pl.BlockSpec((c, LANES), lambda b, s: (bidx(b, s), 2)),
            pl.BlockSpec((1, LANES), const2),
            pl.BlockSpec((1, LANES), const2),
            pl.BlockSpec((2, c, c), const3),
            pl.BlockSpec((2, c, c), const3),
            pl.BlockSpec((2, nlevels, c, c), const4),
            pl.BlockSpec((c, c), const2),
        ],
        out_specs=[
            pl.BlockSpec((c, w), lambda b, s: (fidx(b, s), 0)),
            pl.BlockSpec((c, w), lambda b, s: (bidx(b, s), 0)),
        ],
        out_shape=[jax.ShapeDtypeStruct((t, w), F32), jax.ShapeDtypeStruct((t, w), F32)],
        scratch_shapes=[pltpu.VMEM((2 * nh, DN_HEAD_DIM, DN_HEAD_DIM), F32)],
        compiler_params=_params(("arbitrary", "arbitrary")),
        name="dn_scan",
    )(qkv_c, qkv_c, z_misc, z_misc, alog_row, dt_row, jnp.asarray(tri), jnp.asarray(strict), jnp.asarray(lvl),
      jnp.asarray(eye))


def _mla_q_kernel(z_ref, qn_ref, w_ref, cos_ref, sin_ref, o_ref, *, scale):
    nh = MLA_HEADS
    xn = _rms(z_ref[...].astype(F32), qn_ref[...]).astype(BF16)
    na = nh * MLA_NOPE
    nr = nh * MLA_ROPE
    qa = _dot(xn, w_ref[:, 0:na])
    qr = _dot(xn, w_ref[:, na:na + nr])
    qt = _dot(xn, w_ref[:, na + nr:na + 2 * nr])
    reps = nr // LANES
    cos = jnp.concatenate([cos_ref[...]] * reps, axis=1)
    sin = jnp.concatenate([sin_ref[...]] * reps, axis=1)
    qrr = qr * cos + qt * sin
    for h in range(nh):
        o_ref[h, :, 0:MLA_NOPE] = (qa[:, h * MLA_NOPE:(h + 1) * MLA_NOPE] * scale).astype(o_ref.dtype)
        o_ref[h, :, MLA_NOPE:MLA_QK] = (qrr[:, h * MLA_ROPE:(h + 1) * MLA_ROPE] * scale).astype(o_ref.dtype)


def mla_q(z_mla, q_norm, wq_ext, cos2, sin2, row0, nrows, scale):
    tm = _pick(nrows, ROW_TILE, BF16_SUBLANES)
    assert row0 % tm == 0
    r0 = row0 // tm
    return pl.pallas_call(
        functools.partial(_mla_q_kernel, scale=scale),
        grid=(nrows // tm,),
        in_specs=[
            pl.BlockSpec((tm, Q_LORA), lambda i: (i + r0, 0)),
            pl.BlockSpec((1, Q_LORA), lambda i: (0, 0)),
            pl.BlockSpec(wq_ext.shape, lambda i: (0, 0)),
            pl.BlockSpec((tm, LANES), lambda i: (i + r0, 0)),
            pl.BlockSpec((tm, LANES), lambda i: (i + r0, 0)),
        ],
        out_specs=pl.BlockSpec((MLA_HEADS, tm, MLA_QK), lambda i: (0, i, 0)),
        out_shape=jax.ShapeDtypeStruct((MLA_HEADS, nrows, MLA_QK), BF16),
        compiler_params=_params(("arbitrary",)),
        name="mla_q",
    )(z_mla, q_norm.reshape(1, Q_LORA), wq_ext, cos2, sin2)


def _mla_kv_kernel(z_ref, kvn_ref, w_ref, kr_ref, kt_ref, cos_ref, sin_ref, k_ref, v_ref):
    nh = MLA_HEADS
    xn = _rms(z_ref[...].astype(F32), kvn_ref[...]).astype(BF16)
    kv = _dot(xn, w_ref[...])
    r = MLA_ROPE
    kr = (kr_ref[:, 0:r] * cos_ref[:, 0:r] + kt_ref[:, 0:r] * sin_ref[:, 0:r]).astype(k_ref.dtype)
    per = MLA_NOPE + MLA_V
    for h in range(nh):
        k_ref[h, :, 0:MLA_NOPE] = kv[:, h * per:h * per + MLA_NOPE].astype(k_ref.dtype)
        k_ref[h, :, MLA_NOPE:MLA_QK] = kr
        v_ref[h] = kv[:, h * per + MLA_NOPE:(h + 1) * per].astype(v_ref.dtype)


def mla_kv(z_mla, z_misc, kv_norm, w_ukv, cos2, sin2, row0, nrows):
    tm = _pick(nrows, ROW_TILE, BF16_SUBLANES)
    assert row0 % tm == 0
    r0 = row0 // tm
    return pl.pallas_call(
        _mla_kv_kernel,
        grid=(nrows // tm,),
        in_specs=[
            pl.BlockSpec((tm, KV_LORA), lambda i: (i + r0, 1)),
            pl.BlockSpec((1, KV_LORA), lambda i: (0, 0)),
            pl.BlockSpec(w_ukv.shape, lambda i: (0, 0)),
            pl.BlockSpec((tm, LANES), lambda i: (i + r0, 0)),
            pl.BlockSpec((tm, LANES), lambda i: (i + r0, 1)),
            pl.BlockSpec((tm, LANES), lambda i: (i + r0, 0)),
            pl.BlockSpec((tm, LANES), lambda i: (i + r0, 0)),
        ],
        out_specs=[
            pl.BlockSpec((MLA_HEADS, tm, MLA_QK), lambda i: (0, i, 0)),
            pl.BlockSpec((MLA_HEADS, tm, MLA_V), lambda i: (0, i, 0)),
        ],
        out_shape=[jax.ShapeDtypeStruct((MLA_HEADS, nrows, MLA_QK), BF16),
                   jax.ShapeDtypeStruct((MLA_HEADS, nrows, MLA_V), BF16)],
        compiler_params=_params(("arbitrary",)),
        name="mla_kv",
    )(z_mla, kv_norm.reshape(1, KV_LORA), w_ukv, z_misc, z_misc, cos2, sin2)


def _softmax_update(carry, q, k, v):
    m, l, acc = carry
    s = _dot_nt(q, k)
    m_new = jnp.maximum(m, jnp.max(s, axis=-1, keepdims=True))
    p = jnp.exp2(s - m_new)
    a = jnp.exp2(m - m_new)
    l = a * l + jnp.sum(p, axis=-1, keepdims=True)
    acc = a * acc + _dot(p.astype(BF16), v)
    return m_new, l, acc


def _attn_kernel(q_ref, kc_ref, vc_ref, *rest, tk, nkl):
    if nkl:
        kl_ref, vl_ref, o_ref = rest
    else:
        (o_ref,) = rest
    q = q_ref[...]
    tq = q.shape[0]
    init = (jnp.full((tq, 1), -jnp.inf, F32), jnp.zeros((tq, 1), F32), jnp.zeros((tq, MLA_V), F32))
    carry = _softmax_update(init, q, kc_ref[...], vc_ref[...])
    if nkl:
        def body(j, c):
            off = pl.multiple_of(j * tk, tk)
            return _softmax_update(c, q, kl_ref[pl.ds(off, tk), :], vl_ref[pl.ds(off, tk), :])
        carry = lax.fori_loop(0, nkl, body, carry, unroll=True)
    _, l, acc = carry
    o_ref[...] = (acc / l).astype(o_ref.dtype)


def attention(q, kc, vc, kl, vl, nq_per_batch, lay):
    nb, c, s = lay["B"], lay["C"], lay["S"]
    tq = _pick(nq_per_batch, ATTN_Q_TILE, BF16_SUBLANES)
    nqb = nq_per_batch // tq
    in_specs = [
        pl.BlockSpec((None, tq, MLA_QK), lambda b, h, i: (h, b * nqb + i, 0)),
        pl.BlockSpec((None, c, MLA_QK), lambda b, h, i: (h, b, 0)),
        pl.BlockSpec((None, c, MLA_V), lambda b, h, i: (h, b, 0)),
    ]
    args = [q, kc, vc]
    tk = 0
    nkl = 0
    if kl is not None:
        tk = _pick(s, ATTN_KV_TILE, BF16_SUBLANES)
        nkl = s // tk
        in_specs += [
            pl.BlockSpec((None, s, MLA_QK), lambda b, h, i: (h, b, 0)),
            pl.BlockSpec((None, s, MLA_V), lambda b, h, i: (h, b, 0)),
        ]
        args += [kl, vl]
    return pl.pallas_call(
        functools.partial(_attn_kernel, tk=tk, nkl=nkl),
        grid=(nb, MLA_HEADS, nqb),
        in_specs=in_specs,
        out_specs=pl.BlockSpec((tq, MLA_V), lambda b, h, i: (b * nqb + i, h)),
        out_shape=jax.ShapeDtypeStruct((nb * nq_per_batch, MLA_HEADS * MLA_V), BF16),
        compiler_params=_params(("arbitrary", "arbitrary", "arbitrary")),
        name="attention_lat" if kl is not None else "attention_ctx",
    )(*args)


def _merge1_kernel(bra_ref, of_ref, ob_ref, gate_ref, attc_ref, attl_ref, g0_ref, g1_ref, g2_ref, dnn_ref, wb_ref,
                   y_ref, brb_ref, *, nctx_blk):
    hd = DN_HEAD_DIM
    att = jnp.where(pl.program_id(1) < nctx_blk, attc_ref[...], attl_ref[...])
    o = of_ref[...] + ob_ref[...]
    for h in range(DN_HEADS):
        cs = slice(h * hd, (h + 1) * hd)
        oh = o[:, cs]
        yn = oh * lax.rsqrt(jnp.mean(oh * oh, axis=-1, keepdims=True) + EPS) * dnn_ref[:, cs]
        brb_ref[:, cs] = (yn * _silu(gate_ref[:, cs].astype(F32))).astype(brb_ref.dtype)
    y = jax.nn.sigmoid(g0_ref[...].astype(F32)) * _dot(bra_ref[...], wb_ref[0])
    y = y + jax.nn.sigmoid(g1_ref[...].astype(F32)) * _dot(brb_ref[...], wb_ref[1])
    y = y + jax.nn.sigmoid(g2_ref[...].astype(F32)) * _dot(att, wb_ref[2])
    y_ref[...] = y.astype(y_ref.dtype)


def merge_branches(br_a, o_f, o_b, z_main, attn_c, attn_l, dn_norm, w_branch, lay):
    t = br_a.shape[0]
    d = lay["D"]
    w = BRANCH_W
    tm = _pick(lay["n_ctx"], ROW_TILE, BF16_SUBLANES)
    tn = _pick(d, COL_TILE)
    nn = d // tn
    gate0 = 6 * w // tn
    nctx_blk = lay["n_ctx"] // tm

    def gspec(j):
        return pl.BlockSpec((tm, tn), lambda n, i: (i, gate0 + j * nn + n))

    row = lambda n, i: (i, 0)
    return pl.pallas_call(
        functools.partial(_merge1_kernel, nctx_blk=nctx_blk),
        grid=(nn, t // tm),
        in_specs=[
            pl.BlockSpec((tm, w), row),
            pl.BlockSpec((tm, w), row),
            pl.BlockSpec((tm, w), row),
            pl.BlockSpec((tm, w), lambda n, i: (i, 5)),
            pl.BlockSpec((tm, w), lambda n, i: (jnp.minimum(i, nctx_blk - 1), 0)),
            pl.BlockSpec((tm, w), lambda n, i: (jnp.maximum(i - nctx_blk, 0), 0)),
            gspec(0), gspec(1), gspec(2),
            pl.BlockSpec((1, w), lambda n, i: (0, 0)),
            pl.BlockSpec((3, w, tn), lambda n, i: (0, 0, n)),
        ],
        out_specs=pl.BlockSpec((tm, tn), lambda n, i: (i, n)),
        out_shape=jax.ShapeDtypeStruct((t, d), BF16),
        scratch_shapes=[pltpu.VMEM((tm, w), BF16)],
        compiler_params=_params(("arbitrary", "arbitrary")),
        name="merge_branches",
    )(br_a, o_f, o_b, z_main, attn_c, attn_l, z_main, z_main, z_main,
      jnp.tile(dn_norm.astype(F32), DN_HEADS).reshape(1, w), w_branch)


def _merge2_kernel(y_ref, wo_ref, x_ref, mod_ref, n2_ref, wr_ref, xo_ref, h2_ref, lg_ref, *, d):
    m = mod_ref[...]
    mix = _dot(y_ref[...], wo_ref[...])
    xn = x_ref[...] + _mod_part(m, 2, d) * mix
    xo_ref[...] = xn
    h2 = _rms(xn, n2_ref[...]) * (1.0 + _mod_part(m, 4, d)) + _mod_part(m, 3, d)
    h2_ref[...] = h2
    lg_ref[...] = jnp.dot(h2, wr_ref[...], precision=lax.Precision.HIGHEST, preferred_element_type=F32)


def out_proj_residual(y, w_out, x, mod, layer, norm2, w_router_pad, lay):
    t, d = x.shape
    tm = _pick(lay["n_ctx"], RES_ROW_TILE, BF16_SUBLANES)
    row = lambda i: (i, 0)
    const = lambda i: (0, 0)
    return pl.pallas_call(
        functools.partial(_merge2_kernel, d=d),
        grid=(t // tm,),
        in_specs=[
            pl.BlockSpec((tm, d), row),
            pl.BlockSpec((d, d), const),
            pl.BlockSpec((tm, d), row),
            _mod_spec(layer, tm, lay),
            pl.BlockSpec((1, d), const),
            pl.BlockSpec((d, LANES), const),
        ],
        out_specs=[pl.BlockSpec((tm, d), row), pl.BlockSpec((tm, d), row), pl.BlockSpec((tm, LANES), row)],
        out_shape=[jax.ShapeDtypeStruct((t, d), F32), jax.ShapeDtypeStruct((t, d), F32),
                   jax.ShapeDtypeStruct((t, LANES), F32)],
        compiler_params=_params(("arbitrary",)),
        name="out_proj_residual",
    )(y, w_out, x, mod, norm2.reshape(1, d), w_router_pad)


def _topk_kernel(lg_ref, bias_ref, tri_ref, idx_ref, wt_ref, rank_ref, cnt_ref, run_ref):
    ne, ng = N_EXPERTS, N_GROUPS
    per = ne // ng
    lt = lg_ref[...].T
    tm = lt.shape[1]
    sc = jax.nn.sigmoid(lt[0:ne])
    ch = sc + bias_ref[...]
    ch3 = ch.reshape(ng, per, tm)
    neg = -jnp.inf
    sub = lax.broadcasted_iota(jnp.int32, (ng, per, tm), 1)
    m1 = jnp.max(ch3, axis=1, keepdims=True)
    i1 = jnp.min(jnp.where(ch3 == m1, sub, per), axis=1, keepdims=True)
    m2 = jnp.max(jnp.where(sub == i1, neg, ch3), axis=1, keepdims=True)
    gs = (m1 + m2).reshape(ng, tm)
    giota = lax.broadcasted_iota(jnp.int32, (ng, tm), 0)
    sel = jnp.zeros((ng, tm), F32)
    cur = gs
    for _ in range(TOPK_GROUPS):
        m = jnp.max(cur, axis=0, keepdims=True)
        ix = jnp.min(jnp.where(cur == m, giota, ng), axis=0, keepdims=True)
        hit = giota == ix
        sel = jnp.where(hit, 1.0, sel)
        cur = jnp.where(hit, neg, cur)
    masked = jnp.where(sel.reshape(ng, 1, tm) > 0, ch3, neg).reshape(ne, tm)
    eiota = lax.broadcasted_iota(jnp.int32, (ne, tm), 0)
    idxs, ws, hits = [], [], []
    for _ in range(TOP_K):
        m = jnp.max(masked, axis=0, keepdims=True)
        ix = jnp.min(jnp.where(masked == m, eiota, ne), axis=0, keepdims=True)
        hit = eiota == ix
        ws.append(jnp.sum(jnp.where(hit, sc, 0.0), axis=0, keepdims=True))
        idxs.append(ix)
        hits.append(hit)
        masked = jnp.where(hit, neg, masked)
    wall = jnp.concatenate(ws, axis=0)
    idx_ref[...] = jnp.concatenate(idxs, axis=0)
    wt_ref[...] = wall / jnp.sum(wall, axis=0, keepdims=True) * ROUTED_SCALE

    @pl.when(pl.program_id(0) == 0)
    def _():
        run_ref[...] = jnp.zeros(run_ref.shape, F32)

    chosen = hits[0].astype(F32)
    for hit in hits[1:]:
        chosen = chosen + hit.astype(F32)
    before = run_ref[:, 0:1] + _dot(chosen.astype(BF16), tri_ref[...])
    ranks = [jnp.sum(jnp.where(hit, before, 0.0), axis=0, keepdims=True) for hit in hits]
    rank_ref[...] = jnp.concatenate(ranks, axis=0).astype(jnp.int32)
    run_ref[...] = run_ref[...] + jnp.sum(chosen, axis=1, keepdims=True)
    cnt_ref[...] = run_ref[...]


def route_topk(logits, bias):
    t = logits.shape[0]
    tm = _pick(t, ROW_TILE)
    strict_upper = jnp.asarray(np.triu(np.ones((tm, tm), np.float32), 1), BF16)
    col = lambda i: (0, i)
    idx_t, wt_t, rank_t, cnt = pl.pallas_call(
        _topk_kernel,
        grid=(t // tm,),
        in_specs=[pl.BlockSpec((tm, LANES), lambda i: (i, 0)), pl.BlockSpec((N_EXPERTS, 1), lambda i: (0, 0)),
                  pl.BlockSpec((tm, tm), lambda i: (0, 0))],
        out_specs=[pl.BlockSpec((TOP_K, tm), col), pl.BlockSpec((TOP_K, tm), col), pl.BlockSpec((TOP_K, tm), col),
                   pl.BlockSpec((N_EXPERTS, LANES), lambda i: (0, 0))],
        out_shape=[jax.ShapeDtypeStruct((TOP_K, t), jnp.int32), jax.ShapeDtypeStruct((TOP_K, t), F32),
                   jax.ShapeDtypeStruct((TOP_K, t), jnp.int32), jax.ShapeDtypeStruct((N_EXPERTS, LANES), F32)],
        scratch_shapes=[pltpu.VMEM((N_EXPERTS, LANES), F32)],
        compiler_params=_params(("arbitrary",)),
        name="route_topk",
    )(logits, bias.astype(F32).reshape(N_EXPERTS, 1), strict_upper)
    return idx_t, wt_t, rank_t, cnt[:, 0].astype(jnp.int32)


def moe_layout(idx_t, rank_t, counts, bm):
    k, t = idx_t.shape
    n_blocks = -(-(t * k) // bm) + N_EXPERTS
    padded = (counts + bm - 1) // bm * bm
    pend = jnp.cumsum(padded)
    pstart = pend - padded
    pos_t = (pstart[idx_t] + rank_t).astype(jnp.int32)
    n_used = (pend[-1] // bm).astype(jnp.int32)
    blk = jnp.minimum(jnp.arange(n_blocks, dtype=jnp.int32), n_used - 1) * bm
    blk_e = jnp.minimum(jnp.searchsorted(pend, blk, side="right"), N_EXPERTS - 1).astype(jnp.int32)
    return pos_t, blk_e, n_used.reshape(1), n_blocks


def _pack_pairs(x):
    n = x.shape[1] // 2
    lo = lax.bitcast_convert_type(x[:, :n].astype(BF16).astype(F32), jnp.uint32)
    hi = lax.bitcast_convert_type(x[:, n:].astype(BF16).astype(F32), jnp.uint32)
    return (lo >> 16) | (hi & jnp.uint32(0xFFFF0000))


def _unpack_pairs(w):
    lo = lax.bitcast_convert_type(w << 16, F32)
    hi = lax.bitcast_convert_type(w & jnp.uint32(0xFFFF0000), F32)
    return lo, hi


def _row_copy(src_ref, src_row, dst_ref, dst_row, sem):
    return pltpu.make_async_copy(src_ref.at[pl.ds(src_row, 1)], dst_ref.at[pl.ds(dst_row, 1)], sem)


def _dispatch_kernel(pos_ref, h_ref, xs_in_ref, xs_ref, buf_ref, sem_ref, *, tm, nsteps):
    del xs_in_ref
    i = pl.program_id(0)
    slot = i % 2

    def wait_slot(s):
        for _ in range(TOP_K):
            pltpu.make_async_copy(buf_ref.at[s], xs_ref.at[pl.ds(0, tm)], sem_ref.at[s]).wait()

    @pl.when(i >= 2)
    def _():
        wait_slot(slot)

    buf_ref[slot] = _pack_pairs(h_ref[...])

    def body(tok, carry):
        for k in range(TOP_K):
            _row_copy(buf_ref.at[slot], tok, xs_ref, pos_ref[k, tok], sem_ref.at[slot]).start()
        return carry

    lax.fori_loop(0, tm, body, 0)

    @pl.when(i == nsteps - 1)
    def _():
        if nsteps > 1:
            wait_slot(1 - slot)
        wait_slot(slot)


def moe_dispatch(h2, pos_t, xs_buf):
    t, d = h2.shape
    tm = _pick(t, MOE_ROW_TILE)
    nsteps = t // tm
    return pl.pallas_call(
        functools.partial(_dispatch_kernel, tm=tm, nsteps=nsteps),
        grid=(nsteps,),
        in_specs=[
            pl.BlockSpec((TOP_K, tm), lambda i: (0, i), memory_space=pltpu.SMEM),
            pl.BlockSpec((tm, d), lambda i: (i, 0)),
            pl.BlockSpec(memory_space=pl.ANY),
        ],
        out_specs=pl.BlockSpec(memory_space=pl.ANY),
        out_shape=jax.ShapeDtypeStruct(xs_buf.shape, xs_buf.dtype),
        input_output_aliases={2: 0},
        scratch_shapes=[pltpu.VMEM((2, tm, d // 2), jnp.uint32), pltpu.SemaphoreType.DMA((2,))],
        compiler_params=_params(("arbitrary",)),
        name="moe_dispatch",
    )(pos_t, h2, xs_buf)


def _moe_kernel(be_ref, nu_ref, x_ref, w1_ref, w3_ref, w2_ref, y_ref, w13_s, w2_s, *, ed, half):
    i = pl.program_id(0)
    prev = be_ref[jnp.maximum(i - 1, 0)]
    fresh = jnp.logical_or(i == 0, be_ref[i] != prev)

    @pl.when(fresh)
    def _():
        w13_s[:, 0:ed] = w1_ref[...].astype(BF16)
        w13_s[:, ed:2 * ed] = w3_ref[...].astype(BF16)
        w2_s[...] = w2_ref[...].astype(BF16)

    @pl.when(i < nu_ref[0])
    def _():
        lo, hi = _unpack_pairs(x_ref[...])
        a = _dot(lo.astype(BF16), w13_s[0:half, :]) + _dot(hi.astype(BF16), w13_s[half:2 * half, :])
        act = (_silu(a[:, 0:ed]) * a[:, ed:2 * ed]).astype(BF16)
        y_ref[...] = _pack_pairs(_dot(act, w2_s[...]))

    @pl.when(i >= nu_ref[0])
    def _():
        y_ref[...] = jnp.zeros(y_ref.shape, y_ref.dtype)


def moe_experts(xs, blk_e, n_used, w1, w3, w2, bm):
    n_rows, half = xs.shape
    d = 2 * half
    ed = w1.shape[-1]
    n_blocks = n_rows // bm
    grid_spec = pltpu.PrefetchScalarGridSpec(
        num_scalar_prefetch=2,
        grid=(n_blocks,),
        in_specs=[
            pl.BlockSpec((bm, half), lambda i, be, nu: (i, 0)),
            pl.BlockSpec((None, d, ed), lambda i, be, nu: (be[i], 0, 0)),
            pl.BlockSpec((None, d, ed), lambda i, be, nu: (be[i], 0, 0)),
            pl.BlockSpec((None, ed, d), lambda i, be, nu: (be[i], 0, 0)),
        ],
        out_specs=pl.BlockSpec((bm, half), lambda i, be, nu: (i, 0)),
        scratch_shapes=[pltpu.VMEM((d, 2 * ed), BF16), pltpu.VMEM((ed, d), BF16)],
    )
    return pl.pallas_call(
        functools.partial(_moe_kernel, ed=ed, half=half),
        grid_spec=grid_spec,
        out_shape=jax.ShapeDtypeStruct((n_rows, half), jnp.uint32),
        compiler_params=_params(("arbitrary",)),
        name="moe_experts",
    )(blk_e, n_used, xs, w1, w3, w2)


def _shared_kernel(pos_ref, posn_ref, wt_ref, h2_ref, w13_ref, w2_ref, ys_ref, x_ref, mod_ref, ng_ref, nmod_ref,
                   xo_ref, ho_ref, ybuf_ref, sem_ref, *, d, ed, tm, nsteps, final):
    i = pl.program_id(0)
    slot = i % 2

    def issue(p_ref, s):
        def body(tok, carry):
            for k in range(TOP_K):
                _row_copy(ys_ref, p_ref[k, tok], ybuf_ref.at[s, k], tok, sem_ref.at[s]).start()
            return carry
        lax.fori_loop(0, tm, body, 0)

    @pl.when(i == 0)
    def _():
        issue(pos_ref, 0)

    @pl.when(i + 1 < nsteps)
    def _():
        issue(posn_ref, 1 - slot)

    m = mod_ref[...]
    a = _dot(h2_ref[...].astype(BF16), w13_ref[...])
    act = (_silu(a[:, 0:ed]) * a[:, ed:2 * ed]).astype(BF16)
    f = _dot(act, w2_ref[...])

    for k in range(TOP_K):
        pltpu.make_async_copy(ys_ref.at[pl.ds(0, tm)], ybuf_ref.at[slot, k], sem_ref.at[slot]).wait()
    half = d // 2
    r_lo = jnp.zeros((tm, half), F32)
    r_hi = jnp.zeros((tm, half), F32)
    for k in range(TOP_K):
        lo, hi = _unpack_pairs(ybuf_ref[slot, k])
        wk = wt_ref[:, k:k + 1]
        r_lo = r_lo + wk * lo
        r_hi = r_hi + wk * hi
    f = f + jnp.concatenate([r_lo, r_hi], axis=1)
    xn = x_ref[...] + _mod_part(m, 5, d) * f
    xo_ref[...] = xn
    y = _rms(xn, ng_ref[...])
    if not final:
        nm = nmod_ref[...]
        y = y * (1.0 + _mod_part(nm, 1, d)) + _mod_part(nm, 0, d)
    ho_ref[...] = y.astype(ho_ref.dtype)


def shared_residual(h2, sw13, sw2, ys, pos_t, wts, x, mod, layer, next_g, final, lay):
    t, d = x.shape
    ed = sw2.shape[0]
    tm = _pick(lay["n_ctx"], RES_ROW_TILE, BF16_SUBLANES)
    nsteps = t // tm
    row = lambda i: (i, 0)
    const = lambda i: (0, 0)
    next_layer = layer if final else layer + 1
    return pl.pallas_call(
        functools.partial(_shared_kernel, d=d, ed=ed, tm=tm, nsteps=nsteps, final=final),
        grid=(nsteps,),
        in_specs=[
            pl.BlockSpec((TOP_K, tm), lambda i: (0, i), memory_space=pltpu.SMEM),
            pl.BlockSpec((TOP_K, tm), lambda i: (0, jnp.minimum(i + 1, nsteps - 1)), memory_space=pltpu.SMEM),
            pl.BlockSpec((tm, TOP_K), row),
            pl.BlockSpec((tm, d), row),
            pl.BlockSpec((d, 2 * ed), const),
            pl.BlockSpec((ed, d), const),
            pl.BlockSpec(memory_space=pl.ANY),
            pl.BlockSpec((tm, d), row),
            _mod_spec(layer, tm, lay),
            pl.BlockSpec((1, d), const),
            _mod_spec(next_layer, tm, lay),
        ],
        out_specs=[pl.BlockSpec((tm, d), row), pl.BlockSpec((tm, d), row)],
        out_shape=[jax.ShapeDtypeStruct((t, d), F32), jax.ShapeDtypeStruct((t, d), F32 if final else BF16)],
        scratch_shapes=[pltpu.VMEM((2, TOP_K, tm, d // 2), jnp.uint32), pltpu.SemaphoreType.DMA((2,))],
        compiler_params=_params(("arbitrary",)),
        name="shared_residual",
    )(pos_t, pos_t, wts, h2, sw13, sw2, ys, x, mod, next_g.reshape(1, d), mod)


def _rope_tables(lay):
    half = MLA_ROPE // 4
    s = lay["S"]
    freq = ROPE_BASE ** (-np.arange(half, dtype=np.float64) / half)
    tpos = np.arange(s)
    ang_r = (tpos // GRID_W)[:, None] * freq
    ang_c = (tpos % GRID_W)[:, None] * freq
    ang = np.concatenate([ang_r, ang_r, ang_c, ang_c], axis=1)
    cos_l = np.tile(np.cos(ang), (lay["B"], 2))
    sin_l = np.tile(np.sin(ang), (lay["B"], 2))
    cos = np.concatenate([np.ones((lay["n_ctx"], 2 * MLA_ROPE)), cos_l], axis=0).astype(np.float32)
    sin = np.concatenate([np.zeros((lay["n_ctx"], 2 * MLA_ROPE)), sin_l], axis=0).astype(np.float32)
    return jnp.asarray(cos), jnp.asarray(sin)


def _rot_cols(w):
    q = MLA_ROPE // 4
    a, b, c, e = w[..., 0:q], w[..., q:2 * q], w[..., 2 * q:3 * q], w[..., 3 * q:4 * q]
    return jnp.concatenate([-b, a, -e, c], axis=-1)


def _prep_layer_weights(w_in, w_uq, w_ukv, w_branch, w_out, w_router, sw1, sw3, sw2, d):
    w = BRANCH_W
    o_small = 6 * w
    o_cq = o_small + 4 * DN_HEADS
    o_kr = o_cq + Q_LORA + KV_LORA
    o_gate = o_kr + MLA_ROPE
    w_main = jnp.concatenate([w_in[:, :o_small], w_in[:, o_gate:]], axis=1).astype(BF16)
    w_mla = w_in[:, o_cq:o_kr].astype(BF16)
    w_kr = w_in[:, o_kr:o_gate]
    zpad = jnp.zeros((d, LANES - MLA_ROPE), F32)
    w_misc = jnp.concatenate(
        [w_kr, zpad, _rot_cols(w_kr), zpad, w_in[:, o_small:o_cq], jnp.zeros((d, LANES - 4 * DN_HEADS), F32)],
        axis=1).astype(BF16)
    uq = w_uq.reshape(Q_LORA, MLA_HEADS, MLA_QK)
    uq_r = uq[..., MLA_NOPE:]
    wq_ext = jnp.concatenate(
        [uq[..., :MLA_NOPE].reshape(Q_LORA, -1), uq_r.reshape(Q_LORA, -1), _rot_cols(uq_r).reshape(Q_LORA, -1)],
        axis=1).astype(BF16)
    wr_pad = jnp.zeros((d, LANES), F32).at[:, :N_EXPERTS].set(w_router.astype(F32))
    return dict(w_main=w_main, w_mla=w_mla, w_misc=w_misc, wq_ext=wq_ext, w_ukv=w_ukv.astype(BF16),
                w_branch=w_branch.astype(BF16), w_out=w_out.astype(BF16), wr_pad=wr_pad,
                sw13=jnp.concatenate([sw1, sw3], axis=1).astype(BF16), sw2=sw2.astype(BF16))


def kernel(x, c, ctx, c_ctx, w_mod, b_mod, norm1, norm2, w_in, gm_norm, gm_ws, gm_bs, dn_conv, dn_a_log,
           dn_dt_bias, dn_norm, mla_q_norm, mla_kv_norm, mla_w_uq, mla_w_ukv, w_branch, w_out, moe_router,
           moe_bias, moe_w1, moe_w3, moe_w2, shared_w1, shared_w3, shared_w2, final_norm):
    nb, s, d = x.shape
    cl = ctx.shape[1]
    depth = w_mod.shape[0]
    n_ctx = nb * cl
    t = n_ctx + nb * s
    lay = dict(B=nb, S=s, C=cl, D=d, n_ctx=n_ctx, T=t)
    assert nb + 1 <= SUBLANES and s % GRID_W == 0

    xs = jnp.concatenate([ctx.reshape(n_ctx, d), x.reshape(nb * s, d)], axis=0).astype(F32)
    cvec = jnp.zeros((SUBLANES, d), F32).at[0].set(c_ctx.astype(F32)).at[1:1 + nb].set(c.astype(F32))
    mod = modulation(cvec, w_mod, b_mod)
    cos2, sin2 = _rope_tables(lay)
    scale = MLA_QK ** -0.5 * LOG2E

    h = prenorm(xs, norm1[0], mod, 0, lay)
    out = None
    xs_buf = None
    for i in range(depth):
        last = i == depth - 1
        wts = _prep_layer_weights(w_in[i], mla_w_uq[i], mla_w_ukv[i], w_branch[i], w_out[i], moe_router[i],
                                  shared_w1[i], shared_w3[i], shared_w2[i], d)
        z_main = matmul(h, wts["w_main"], BF16, "in_proj_main")
        z_mla = matmul(h, wts["w_mla"], BF16, "in_proj_mla")
        z_misc = matmul(h, wts["w_misc"], F32, "in_proj_misc")

        br_a = gmlp(z_main, gm_norm[i], gm_ws[i], gm_bs[i], lay)

        qkv_c = dn_short_conv(z_main, dn_conv[i], lay)
        o_f, o_b = dn_scan(qkv_c, z_misc, dn_a_log[i], dn_dt_bias[i], lay)

        kc, vc = mla_kv(z_mla, z_misc, mla_kv_norm[i], wts["w_ukv"], cos2, sin2, 0, n_ctx)
        kl, vl = mla_kv(z_mla, z_misc, mla_kv_norm[i], wts["w_ukv"], cos2, sin2, n_ctx, nb * s)
        ql = mla_q(z_mla, mla_q_norm[i], wts["wq_ext"], cos2, sin2, n_ctx, nb * s, scale)
        attn_l = attention(ql, kc, vc, kl, vl, s, lay)
        qc = mla_q(z_mla, mla_q_norm[i], wts["wq_ext"], cos2, sin2, 0, n_ctx, scale)
        attn_c = attention(qc, kc, vc, None, None, cl, lay)

        y = merge_branches(br_a, o_f, o_b, z_main, attn_c, attn_l, dn_norm[i], wts["w_branch"], lay)
        xs, h2, logits = out_proj_residual(y, wts["w_out"], xs, mod, i, norm2[i], wts["wr_pad"], lay)

        idx_t, wt_t, rank_t, counts = route_topk(logits, moe_bias[i])
        bm = MOE_ROW_TILE
        pos_t, blk_e, n_used, n_blocks = moe_layout(idx_t, rank_t, counts, bm)
        if xs_buf is None:
            xs_buf = jnp.zeros((n_blocks * bm, d // 2), jnp.uint32)
        xs_buf = moe_dispatch(h2, pos_t, xs_buf)
        y_sorted = moe_experts(xs_buf, blk_e, n_used, moe_w1[i], moe_w3[i], moe_w2[i], bm)

        next_g = final_norm if last else norm1[i + 1]
        xs, h = shared_residual(h2, wts["sw13"], wts["sw2"], y_sorted, pos_t, wt_t.T, xs, mod, i, next_g, last, lay)
        out = h
    return out[n_ctx:].reshape(nb, s, d).astype(x.dtype)
```

```python
import functools
import math

import numpy as np
import jax
import jax.numpy as jnp
from jax import lax
from jax.experimental import pallas as pl
from jax.experimental.pallas import tpu as pltpu

F32 = jnp.float32
BF16 = jnp.bfloat16

GRID_W = 64
EPS = 1e-6
GM_CHUNK = 128
GM_GROUPS = 8
DN_HEADS = 8
DN_HEAD_DIM = 128
DN_CHUNK = 64
DN_CONV = 5
MLA_HEADS = 8
MLA_NOPE = 128
MLA_ROPE = 64
MLA_V = 128
MLA_QK = MLA_NOPE + MLA_ROPE
MLA_VE = 2 * MLA_V
Q_LORA = 512
KV_LORA = 512
ROPE_BASE = 10000.0
BRANCH_W = 1024
N_EXPERTS = 64
TOP_K = 8
N_GROUPS = 8
TOPK_GROUPS = 4
ROUTED_SCALE = 2.5
LOG2E = 1.4426950408889634

LANES = 128
SUBLANES = 8
BF16_SUBLANES = 16
VMEM_LIMIT_MB = 56

ROW_TILE = 512
RES_ROW_TILE = 256
CONV_ROW_TILE = 256
COL_TILE = 1024
MOE_ROW_TILE = 256
ATTN_Q_TILE = 512
ATTN_KV_TILE = 1024


def _pick(n, pref, mult=LANES):
    if n <= pref:
        return n
    for t in range(pref - pref % mult, 0, -mult):
        if n % t == 0:
            return t
    return n


def _params(sem, mb=VMEM_LIMIT_MB):
    return pltpu.CompilerParams(dimension_semantics=sem, vmem_limit_bytes=mb * 1024 * 1024)


def _silu(x):
    return x * jax.nn.sigmoid(x)


def _gelu(x):
    return 0.5 * x * (1.0 + jnp.tanh(0.7978845608028654 * (x + 0.044715 * x * x * x)))


def _dot(a, b):
    return jnp.dot(a, b, preferred_element_type=F32)


def _dot_nt(a, b):
    return lax.dot_general(a, b, (((1,), (1,)), ((), ())), preferred_element_type=F32)


def _dot_tn(a, b):
    return lax.dot_general(a, b, (((0,), (0,)), ((), ())), preferred_element_type=F32)


def _mod_kernel(c_ref, w_ref, b_ref, o_ref):
    s = _silu(c_ref[...])
    o_ref[...] = _dot(s.astype(BF16), w_ref[...].astype(BF16)) + b_ref[...]


def modulation(cvec, w_mod, b_mod):
    nl, d, n6 = w_mod.shape
    tn = _pick(n6, COL_TILE)
    out = pl.pallas_call(
        _mod_kernel,
        grid=(nl, n6 // tn),
        in_specs=[
            pl.BlockSpec((SUBLANES, d), lambda l, n: (0, 0)),
            pl.BlockSpec((None, d, tn), lambda l, n: (l, 0, n)),
            pl.BlockSpec((None, 1, tn), lambda l, n: (l, 0, n)),
        ],
        out_specs=pl.BlockSpec((None, SUBLANES, tn), lambda l, n: (l, 0, n)),
        out_shape=jax.ShapeDtypeStruct((nl, SUBLANES, n6), F32),
        compiler_params=_params(("arbitrary", "arbitrary")),
        name="modulation",
    )(cvec, w_mod, b_mod.reshape(nl, 1, n6))
    return out.reshape(nl, SUBLANES, 1, n6)


def _seg_of_block(i, tm, lay):
    nctx_blk = lay["n_ctx"] // tm
    lat_bps = lay["S"] // tm
    return jnp.where(i < nctx_blk, 0, 1 + (i - nctx_blk) // lat_bps)


def _mod_spec(layer, tm, lay, row0_blk=0):
    n6 = 6 * lay["D"]
    return pl.BlockSpec((None, None, 1, n6), lambda i: (layer, _seg_of_block(i + row0_blk, tm, lay), 0, 0))


def _mod_part(m, k, d):
    return m[:, k * d:(k + 1) * d]


def _rms(x, g):
    return x * lax.rsqrt(jnp.mean(x * x, axis=-1, keepdims=True) + EPS) * g


def _prenorm_kernel(x_ref, g_ref, mod_ref, o_ref, *, d):
    m = mod_ref[...]
    y = _rms(x_ref[...], g_ref[...])
    o_ref[...] = (y * (1.0 + _mod_part(m, 1, d)) + _mod_part(m, 0, d)).astype(o_ref.dtype)


def prenorm(x, g, mod, layer, lay):
    t, d = x.shape
    tm = _pick(lay["n_ctx"], ROW_TILE, SUBLANES)
    return pl.pallas_call(
        functools.partial(_prenorm_kernel, d=d),
        grid=(t // tm,),
        in_specs=[
            pl.BlockSpec((tm, d), lambda i: (i, 0)),
            pl.BlockSpec((1, d), lambda i: (0, 0)),
            _mod_spec(layer, tm, lay),
        ],
        out_specs=pl.BlockSpec((tm, d), lambda i: (i, 0)),
        out_shape=jax.ShapeDtypeStruct((t, d), BF16),
        compiler_params=_params(("arbitrary",)),
        name="prenorm",
    )(x, g.reshape(1, d), mod)


def _mm_kernel(x_ref, w_ref, o_ref):
    o_ref[...] = _dot(x_ref[...], w_ref[...]).astype(o_ref.dtype)


def matmul(x, w, out_dtype, name):
    t, k = x.shape
    n = w.shape[1]
    tm = _pick(t, ROW_TILE, BF16_SUBLANES)
    tn = _pick(n, COL_TILE)
    return pl.pallas_call(
        _mm_kernel,
        grid=(n // tn, t // tm),
        in_specs=[
            pl.BlockSpec((tm, k), lambda j, i: (i, 0)),
            pl.BlockSpec((k, tn), lambda j, i: (0, j)),
        ],
        out_specs=pl.BlockSpec((tm, tn), lambda j, i: (i, j)),
        out_shape=jax.ShapeDtypeStruct((t, n), out_dtype),
        compiler_params=_params(("arbitrary", "arbitrary")),
        name=name,
    )(x, w)


def _gmlp_kernel(u_ref, v_ref, gn_ref, ws_ref, bias_ref, o_ref, *, nchunk):
    for j in range(nchunk):
        rs = slice(j * GM_CHUNK, (j + 1) * GM_CHUNK)
        v = _gelu(v_ref[rs, :].astype(F32))
        vb = _rms(v, gn_ref[...]).astype(BF16)
        u = _gelu(u_ref[rs, :].astype(F32))
        for g in range(GM_GROUPS):
            cs = slice(g * LANES, (g + 1) * LANES)
            mixed = _dot(ws_ref[g], vb[:, cs]) + bias_ref[:, cs]
            o_ref[rs, cs] = (u[:, cs] * mixed).astype(o_ref.dtype)


def gmlp(z_main, gm_norm, gm_ws, gm_bs, lay):
    t = z_main.shape[0]
    w = BRANCH_W
    tm = _pick(lay["n_ctx"], ROW_TILE, GM_CHUNK)
    bias = jnp.repeat(gm_bs.T.astype(F32), w // GM_GROUPS, axis=1)
    return pl.pallas_call(
        functools.partial(_gmlp_kernel, nchunk=tm // GM_CHUNK),
        grid=(t // tm,),
        in_specs=[
            pl.BlockSpec((tm, w), lambda i: (i, 0)),
            pl.BlockSpec((tm, w), lambda i: (i, 1)),
            pl.BlockSpec((1, w), lambda i: (0, 0)),
            pl.BlockSpec((GM_GROUPS, GM_CHUNK, GM_CHUNK), lambda i: (0, 0, 0)),
            pl.BlockSpec((GM_CHUNK, w), lambda i: (0, 0)),
        ],
        out_specs=pl.BlockSpec((tm, w), lambda i: (i, 0)),
        out_shape=jax.ShapeDtypeStruct((t, w), BF16),
        compiler_params=_params(("arbitrary",)),
        name="gmlp",
    )(z_main, z_main, gm_norm.reshape(1, w), gm_ws.astype(BF16), bias)


def _conv_kernel(x_ref, hp_ref, hn_ref, w_ref, o_ref, xe_ref, *, tm, nctx_blk, ctx_bps, lat_bps):
    j = pl.program_id(0)
    i = pl.program_id(1)
    li = i - nctx_blk
    seg_start = jnp.where(i < nctx_blk, (i % ctx_bps) == 0, (li % lat_bps) == 0)
    seg_end = jnp.where(i < nctx_blk, ((i + 1) % ctx_bps) == 0, ((li + 1) % lat_bps) == 0)
    halo = BF16_SUBLANES
    xe_ref[0:halo, :] = jnp.where(seg_start, 0.0, hp_ref[...].astype(F32))
    xe_ref[halo:halo + tm, :] = x_ref[...].astype(F32)
    xe_ref[halo + tm:2 * halo + tm, :] = jnp.where(seg_end, 0.0, hn_ref[...].astype(F32))
    base = halo - DN_CONV // 2
    acc = w_ref[0:1, :] * xe_ref[base:base + tm, :]
    for tap in range(1, DN_CONV):
        acc = acc + w_ref[tap:tap + 1, :] * xe_ref[base + tap:base + tap + tm, :]
    y = _silu(acc)
    unit = j < 2
    for h in range(DN_HEADS):
        cs = slice(h * DN_HEAD_DIM, (h + 1) * DN_HEAD_DIM)
        yh = y[:, cs]
        nrm = yh * lax.rsqrt(jnp.sum(yh * yh, axis=-1, keepdims=True) + EPS)
        o_ref[:, cs] = jnp.where(unit, nrm, yh).astype(o_ref.dtype)


def dn_short_conv(z_main, conv_w, lay):
    t = z_main.shape[0]
    w = BRANCH_W
    tm = _pick(lay["C"], CONV_ROW_TILE, BF16_SUBLANES)
    halo = BF16_SUBLANES
    hb = tm // halo
    nhalo = t // halo
    wpad = jnp.zeros((SUBLANES, 3 * w), F32).at[:DN_CONV].set(conv_w.astype(F32))
    kern = functools.partial(_conv_kernel, tm=tm, nctx_blk=lay["n_ctx"] // tm, ctx_bps=lay["C"] // tm,
                             lat_bps=lay["S"] // tm)
    return pl.pallas_call(
        kern,
        grid=(3, t // tm),
        in_specs=[
            pl.BlockSpec((tm, w), lambda j, i: (i, 2 + j)),
            pl.BlockSpec((halo, w), lambda j, i: (jnp.maximum(i * hb - 1, 0), 2 + j)),
            pl.BlockSpec((halo, w), lambda j, i: (jnp.minimum((i + 1) * hb, nhalo - 1), 2 + j)),
            pl.BlockSpec((SUBLANES, w), lambda j, i: (0, j)),
        ],
        out_specs=pl.BlockSpec((tm, w), lambda j, i: (i, j)),
        out_shape=jax.ShapeDtypeStruct((t, 3 * w), BF16),
        scratch_shapes=[pltpu.VMEM((tm + 2 * halo, w), F32)],
        compiler_params=_params(("arbitrary", "arbitrary")),
        name="dn_conv",
    )(z_main, z_main, z_main, wpad)


def _dn_masks():
    c = DN_CHUNK
    r = np.arange(c)[:, None]
    s = np.arange(c)[None, :]
    tri = np.stack([r >= s, r <= s]).astype(np.float32)
    strict = np.stack([r > s, r < s]).astype(np.float32)
    lv = []
    b = 1
    while b < c:
        same = (r // (2 * b)) == (s // (2 * b))
        lo = same & ((r // b) % 2 == 1) & ((s // b) % 2 == 0)
        up = same & ((r // b) % 2 == 0) & ((s // b) % 2 == 1)
        lv.append(np.stack([lo, up]))
        b *= 2
    lvl = np.stack(lv, axis=1).astype(np.float32)
    return tri, strict, lvl, np.eye(c, dtype=np.float32)


def _dn_kernel(qf_ref, qb_ref, abf_ref, abb_ref, alog_ref, dt_ref, tri_ref, strict_ref, lvl_ref, eye_ref,
               of_ref, ob_ref, s_ref, *, nlevels):
    step = pl.program_id(1)

    @pl.when(step == 0)
    def _():
        s_ref[...] = jnp.zeros(s_ref.shape, F32)

    hd = DN_HEAD_DIM
    w = DN_HEADS * hd
    nh = DN_HEADS
    cc = DN_CHUNK
    eye = eye_ref[...]
    inst = []
    for d, (x_ref, ab_ref, o_ref) in enumerate(((qf_ref, abf_ref, of_ref), (qb_ref, abb_ref, ob_ref))):
        ab = ab_ref[...]
        g_all = -jnp.exp(alog_ref[...]) * jax.nn.softplus(ab + dt_ref[...])
        beta_all = jax.nn.sigmoid(ab)
        gam_c = jnp.dot(tri_ref[d], g_all, precision=lax.Precision.HIGHEST, preferred_element_type=F32)
        gam_r = gam_c.T
        for h in range(nh):
            lane = d * nh + h
            inst.append(dict(
                d=d, lane=lane, h=h, slot=lane, o_ref=o_ref,
                q=x_ref[:, h * hd:(h + 1) * hd].astype(F32),
                k=x_ref[:, w + h * hd:w + (h + 1) * hd].astype(F32),
                v=x_ref[:, 2 * w + h * hd:2 * w + (h + 1) * hd].astype(F32),
                gc=gam_c[:, lane:lane + 1],
                gr=gam_r[lane:lane + 1, :],
                bc=beta_all[:, 2 * nh + lane:2 * nh + lane + 1],
                st=s_ref[lane]))
    for it in inst:
        tri = tri_ref[it["d"]]
        diff = it["gc"] - it["gr"]
        it["dec"] = jnp.where(tri > 0, jnp.exp(jnp.where(tri > 0, diff, 0.0)), 0.0)
        it["kb"] = it["k"] * it["bc"]
        it["qs"] = it["q"] * (hd ** -0.5)
    for it in inst:
        kq = _dot_nt(jnp.concatenate([it["kb"], it["qs"]], axis=0).astype(BF16), it["k"].astype(BF16))
        it["a"] = kq[:cc] * it["dec"] * strict_ref[it["d"]]
        it["attn"] = (kq[cc:] * it["dec"]).astype(BF16)
    for it in inst:
        it["x"] = eye - it["a"] * lvl_ref[it["d"], 0]
    for lv in range(1, nlevels):
        for it in inst:
            it["xb"] = it["x"].astype(BF16)
            it["p"] = _dot(it["xb"], (it["a"] * lvl_ref[it["d"], lv]).astype(BF16)).astype(BF16)
        for it in inst:
            it["x"] = it["x"] - _dot(it["p"], it["xb"])
    for it in inst:
        eg = jnp.exp(it["gc"])
        rhs = jnp.concatenate([it["kb"] * eg, it["v"] * it["bc"]], axis=1).astype(BF16)
        it["sol"] = _dot(it["x"].astype(BF16), rhs)
        last = cc - 1 if it["d"] == 0 else 0
        g_last = it["gc"][last:last + 1, :]
        it["gtot"] = jnp.exp(g_last)
        it["kd"] = (it["k"] * jnp.exp(g_last - it["gc"])).astype(BF16)
        it["qd"] = it["qs"] * eg
    for it in inst:
        wq = jnp.concatenate([it["sol"][:, :hd], it["qd"]], axis=0).astype(BF16)
        it["r"] = _dot(wq, it["st"].astype(BF16))
    for it in inst:
        it["vn"] = (it["sol"][:, hd:] - it["r"][:cc]).astype(BF16)
    for it in inst:
        it["o"] = it["r"][cc:] + _dot(it["attn"], it["vn"])
        it["sn"] = it["st"] * it["gtot"] + _dot_tn(it["kd"], it["vn"])
    for it in inst:
        it["o_ref"][:, it["h"] * hd:(it["h"] + 1) * hd] = it["o"]
        s_ref[it["slot"]] = it["sn"]


def dn_scan(qkv_c, z_misc, a_log, dt_bias, lay):
    t = qkv_c.shape[0]
    w = BRANCH_W
    c = DN_CHUNK
    nb = lay["B"]
    ncc = lay["C"] // c
    ncl = lay["S"] // c
    nsteps = ncc + ncl

    def fidx(b, s):
        return jnp.where(s < ncc, b * ncc + s, nb * ncc + b * ncl + (s - ncc))

    def bidx(b, s):
        return jnp.where(s < ncc, b * ncc + (ncc - 1 - s), nb * ncc + b * ncl + (ncl - 1 - (s - ncc)))

    tri, strict, lvl, eye = _dn_masks()
    nlevels = lvl.shape[1]
    nh = DN_HEADS
    alog_row = jnp.zeros((1, LANES), F32).at[0, :2 * nh].set(a_log.reshape(-1).astype(F32))
    dt_row = jnp.zeros((1, LANES), F32).at[0, :2 * nh].set(dt_bias.reshape(-1).astype(F32))
    const2 = lambda b, s: (0, 0)
    const3 = lambda b, s: (0, 0, 0)
    const4 = lambda b, s: (0, 0, 0, 0)
    return pl.pallas_call(
        functools.partial(_dn_kernel, nlevels=nlevels),
        grid=(nb, nsteps),
        in_specs=[
            pl.BlockSpec((c, 3 * w), lambda b, s: (fidx(b, s), 0)),
            pl.BlockSpec((c, 3 * w), lambda b, s: (bidx(b, s), 0)),
            pl.BlockSpec((c, LANES), lambda b, s: (fidx(b, s), 2)),
            pl.BlockSpec((c, LANES), lambda b, s: (bidx(b, s), 2)),
            pl.BlockSpec((1, LANES), const2),
            pl.BlockSpec((1, LANES), const2),
            pl.BlockSpec((2, c, c), const3),
            pl.BlockSpec((2, c, c), const3),
            pl.BlockSpec((2, nlevels, c, c), const4),
            pl.BlockSpec((c, c), const2),
        ],
        out_specs=[
            pl.BlockSpec((c, w), lambda b, s: (fidx(b, s), 0)),
            pl.BlockSpec((c, w), lambda b, s: (bidx(b, s), 0)),
        ],
        out_shape=[jax.ShapeDtypeStruct((t, w), F32), jax.ShapeDtypeStruct((t, w), F32)],
        scratch_shapes=[pltpu.VMEM((2 * nh, DN_HEAD_DIM, DN_HEAD_DIM), F32)],
        compiler_params=_params(("arbitrary", "arbitrary")),
        name="dn_scan",
    )(qkv_c, qkv_c, z_misc, z_misc, alog_row, dt_row, jnp.asarray(tri), jnp.asarray(strict), jnp.asarray(lvl),
      jnp.asarray(eye))


def _mla_q_kernel(z_ref, qn_ref, w_ref, cos_ref, sin_ref, o_ref, *, scale):
    nh = MLA_HEADS
    xn = _rms(z_ref[...].astype(F32), qn_ref[...]).astype(BF16)
    na = nh * MLA_NOPE
    nr = nh * MLA_ROPE
    qa = _dot(xn, w_ref[:, 0:na])
    qr = _dot(xn, w_ref[:, na:na + nr])
    qt = _dot(xn, w_ref[:, na + nr:na + 2 * nr])
    reps = nr // LANES
    cos = jnp.concatenate([cos_ref[...]] * reps, axis=1)
    sin = jnp.concatenate([sin_ref[...]] * reps, axis=1)
    qrr = qr * cos + qt * sin
    for h in range(nh):
        o_ref[h, :, 0:MLA_NOPE] = (qa[:, h * MLA_NOPE:(h + 1) * MLA_NOPE] * scale).astype(o_ref.dtype)
        o_ref[h, :, MLA_NOPE:MLA_QK] = (qrr[:, h * MLA_ROPE:(h + 1) * MLA_ROPE] * scale).astype(o_ref.dtype)


def mla_q(z_mla, q_norm, wq_ext, cos2, sin2, row0, nrows, scale):
    tm = _pick(nrows, ROW_TILE, BF16_SUBLANES)
    assert row0 % tm == 0
    r0 = row0 // tm
    return pl.pallas_call(
        functools.partial(_mla_q_kernel, scale=scale),
        grid=(nrows // tm,),
        in_specs=[
            pl.BlockSpec((tm, Q_LORA), lambda i: (i + r0, 0)),
            pl.BlockSpec((1, Q_LORA), lambda i: (0, 0)),
            pl.BlockSpec(wq_ext.shape, lambda i: (0, 0)),
            pl.BlockSpec((tm, LANES), lambda i: (i + r0, 0)),
            pl.BlockSpec((tm, LANES), lambda i: (i + r0, 0)),
        ],
        out_specs=pl.BlockSpec((MLA_HEADS, tm, MLA_QK), lambda i: (0, i, 0)),
        out_shape=jax.ShapeDtypeStruct((MLA_HEADS, nrows, MLA_QK), BF16),
        compiler_params=_params(("arbitrary",)),
        name="mla_q",
    )(z_mla, q_norm.reshape(1, Q_LORA), wq_ext, cos2, sin2)


def _mla_kv_kernel(z_ref, kvn_ref, w_ref, kr_ref, kt_ref, cos_ref, sin_ref, k_ref, v_ref):
    nh = MLA_HEADS
    xn = _rms(z_ref[...].astype(F32), kvn_ref[...]).astype(BF16)
    kv = _dot(xn, w_ref[...])
    r = MLA_ROPE
    kr = (kr_ref[:, 0:r] * cos_ref[:, 0:r] + kt_ref[:, 0:r] * sin_ref[:, 0:r]).astype(k_ref.dtype)
    per = MLA_NOPE + MLA_V
    ones_col = (lax.broadcasted_iota(jnp.int32, (kv.shape[0], MLA_VE - MLA_V), 1) == 0).astype(v_ref.dtype)
    for h in range(nh):
        k_ref[h, :, 0:MLA_NOPE] = kv[:, h * per:h * per + MLA_NOPE].astype(k_ref.dtype)
        k_ref[h, :, MLA_NOPE:MLA_QK] = kr
        v_ref[h, :, 0:MLA_V] = kv[:, h * per + MLA_NOPE:(h + 1) * per].astype(v_ref.dtype)
        v_ref[h, :, MLA_V:MLA_VE] = ones_col


def mla_kv(z_mla, z_misc, kv_norm, w_ukv, cos2, sin2, row0, nrows):
    tm = _pick(nrows, ROW_TILE, BF16_SUBLANES)
    assert row0 % tm == 0
    r0 = row0 // tm
    return pl.pallas_call(
        _mla_kv_kernel,
        grid=(nrows // tm,),
        in_specs=[
            pl.BlockSpec((tm, KV_LORA), lambda i: (i + r0, 1)),
            pl.BlockSpec((1, KV_LORA), lambda i: (0, 0)),
            pl.BlockSpec(w_ukv.shape, lambda i: (0, 0)),
            pl.BlockSpec((tm, LANES), lambda i: (i + r0, 0)),
            pl.BlockSpec((tm, LANES), lambda i: (i + r0, 1)),
            pl.BlockSpec((tm, LANES), lambda i: (i + r0, 0)),
            pl.BlockSpec((tm, LANES), lambda i: (i + r0, 0)),
        ],
        out_specs=[
            pl.BlockSpec((MLA_HEADS, tm, MLA_QK), lambda i: (0, i, 0)),
            pl.BlockSpec((MLA_HEADS, tm, MLA_VE), lambda i: (0, i, 0)),
        ],
        out_shape=[jax.ShapeDtypeStruct((MLA_HEADS, nrows, MLA_QK), BF16),
                   jax.ShapeDtypeStruct((MLA_HEADS, nrows, MLA_VE), BF16)],
        compiler_params=_params(("arbitrary",)),
        name="mla_kv",
    )(z_mla, kv_norm.reshape(1, KV_LORA), w_ukv, z_misc, z_misc, cos2, sin2)


def _softmax_update(carry, q, k, v):
    m, acc = carry
    s = _dot_nt(q, k)
    m_new = jnp.maximum(m, jnp.max(s, axis=-1, keepdims=True))
    p = jnp.exp2(s - m_new)
    acc = jnp.exp2(m - m_new) * acc + _dot(p.astype(BF16), v)
    return m_new, acc


def _attn_kernel(q_ref, kc_ref, vc_ref, *rest, tk, nkl):
    if nkl:
        kl_ref, vl_ref, o_ref = rest
    else:
        (o_ref,) = rest
    q = q_ref[...]
    tq = q.shape[0]
    init = (jnp.full((tq, 1), -jnp.inf, F32), jnp.zeros((tq, MLA_VE), F32))
    carry = _softmax_update(init, q, kc_ref[...], vc_ref[...])
    if nkl:
        def body(j, c):
            off = pl.multiple_of(j * tk, tk)
            return _softmax_update(c, q, kl_ref[pl.ds(off, tk), :], vl_ref[pl.ds(off, tk), :])
        carry = lax.fori_loop(0, nkl, body, carry, unroll=True)
    _, acc = carry
    o_ref[...] = (acc[:, 0:MLA_V] / acc[:, MLA_V:MLA_V + 1]).astype(o_ref.dtype)


def attention(q, kc, vc, kl, vl, nq_per_batch, lay):
    nb, c, s = lay["B"], lay["C"], lay["S"]
    tq = _pick(nq_per_batch, ATTN_Q_TILE, BF16_SUBLANES)
    nqb = nq_per_batch // tq
    in_specs = [
        pl.BlockSpec((None, tq, MLA_QK), lambda b, h, i: (h, b * nqb + i, 0)),
        pl.BlockSpec((None, c, MLA_QK), lambda b, h, i: (h, b, 0)),
        pl.BlockSpec((None, c, MLA_VE), lambda b, h, i: (h, b, 0)),
    ]
    args = [q, kc, vc]
    tk = 0
    nkl = 0
    if kl is not None:
        tk = _pick(s, ATTN_KV_TILE, BF16_SUBLANES)
        nkl = s // tk
        in_specs += [
            pl.BlockSpec((None, s, MLA_QK), lambda b, h, i: (h, b, 0)),
            pl.BlockSpec((None, s, MLA_VE), lambda b, h, i: (h, b, 0)),
        ]
        args += [kl, vl]
    return pl.pallas_call(
        functools.partial(_attn_kernel, tk=tk, nkl=nkl),
        grid=(nb, MLA_HEADS, nqb),
        in_specs=in_specs,
        out_specs=pl.BlockSpec((tq, MLA_V), lambda b, h, i: (b * nqb + i, h)),
        out_shape=jax.ShapeDtypeStruct((nb * nq_per_batch, MLA_HEADS * MLA_V), BF16),
        compiler_params=_params(("arbitrary", "arbitrary", "arbitrary")),
        name="attention_lat" if kl is not None else "attention_ctx",
    )(*args)


def _merge1_kernel(bra_ref, of_ref, ob_ref, gate_ref, attc_ref, attl_ref, g0_ref, g1_ref, g2_ref, dnn_ref, wb_ref,
                   y_ref, brb_ref, *, nctx_blk):
    hd = DN_HEAD_DIM
    att = jnp.where(pl.program_id(1) < nctx_blk, attc_ref[...], attl_ref[...])
    o = of_ref[...] + ob_ref[...]
    for h in range(DN_HEADS):
        cs = slice(h * hd, (h + 1) * hd)
        oh = o[:, cs]
        yn = oh * lax.rsqrt(jnp.mean(oh * oh, axis=-1, keepdims=True) + EPS) * dnn_ref[:, cs]
        brb_ref[:, cs] = (yn * _silu(gate_ref[:, cs].astype(F32))).astype(brb_ref.dtype)
    y = jax.nn.sigmoid(g0_ref[...].astype(F32)) * _dot(bra_ref[...], wb_ref[0])
    y = y + jax.nn.sigmoid(g1_ref[...].astype(F32)) * _dot(brb_ref[...], wb_ref[1])
    y = y + jax.nn.sigmoid(g2_ref[...].astype(F32)) * _dot(att, wb_ref[2])
    y_ref[...] = y.astype(y_ref.dtype)


def merge_branches(br_a, o_f, o_b, z_main, attn_c, attn_l, dn_norm, w_branch, lay):
    t = br_a.shape[0]
    d = lay["D"]
    w = BRANCH_W
    tm = _pick(lay["n_ctx"], ROW_TILE, BF16_SUBLANES)
    tn = _pick(d, COL_TILE)
    nn = d // tn
    gate0 = 6 * w // tn
    nctx_blk = lay["n_ctx"] // tm

    def gspec(j):
        return pl.BlockSpec((tm, tn), lambda n, i: (i, gate0 + j * nn + n))

    row = lambda n, i: (i, 0)
    return pl.pallas_call(
        functools.partial(_merge1_kernel, nctx_blk=nctx_blk),
        grid=(nn, t // tm),
        in_specs=[
            pl.BlockSpec((tm, w), row),
            pl.BlockSpec((tm, w), row),
            pl.BlockSpec((tm, w), row),
            pl.BlockSpec((tm, w), lambda n, i: (i, 5)),
            pl.BlockSpec((tm, w), lambda n, i: (jnp.minimum(i, nctx_blk - 1), 0)),
            pl.BlockSpec((tm, w), lambda n, i: (jnp.maximum(i - nctx_blk, 0), 0)),
            gspec(0), gspec(1), gspec(2),
            pl.BlockSpec((1, w), lambda n, i: (0, 0)),
            pl.BlockSpec((3, w, tn), lambda n, i: (0, 0, n)),
        ],
        out_specs=pl.BlockSpec((tm, tn), lambda n, i: (i, n)),
        out_shape=jax.ShapeDtypeStruct((t, d), BF16),
        scratch_shapes=[pltpu.VMEM((tm, w), BF16)],
        compiler_params=_params(("arbitrary", "arbitrary")),
        name="merge_branches",
    )(br_a, o_f, o_b, z_main, attn_c, attn_l, z_main, z_main, z_main,
      jnp.tile(dn_norm.astype(F32), DN_HEADS).reshape(1, w), w_branch)


def _merge2_kernel(y_ref, wo_ref, x_ref, mod_ref, n2_ref, wrh_ref, wrl_ref, xo_ref, h2_ref, lg_ref, *, d):
    m = mod_ref[...]
    mix = _dot(y_ref[...], wo_ref[...])
    xn = x_ref[...] + _mod_part(m, 2, d) * mix
    xo_ref[...] = xn
    h2 = _rms(xn, n2_ref[...]) * (1.0 + _mod_part(m, 4, d)) + _mod_part(m, 3, d)
    h2_ref[...] = h2
    hh = h2.astype(BF16)
    hl = (h2 - hh.astype(F32)).astype(BF16)
    lg_ref[...] = _dot(hh, wrh_ref[...]) + (_dot(hl, wrh_ref[...]) + _dot(hh, wrl_ref[...]))


def out_proj_residual(y, w_out, x, mod, layer, norm2, w_router_pad, lay):
    t, d = x.shape
    tm = _pick(lay["n_ctx"], RES_ROW_TILE, BF16_SUBLANES)
    row = lambda i: (i, 0)
    const = lambda i: (0, 0)
    wr_hi = w_router_pad.astype(BF16)
    wr_lo = (w_router_pad - wr_hi.astype(F32)).astype(BF16)
    return pl.pallas_call(
        functools.partial(_merge2_kernel, d=d),
        grid=(t // tm,),
        in_specs=[
            pl.BlockSpec((tm, d), row),
            pl.BlockSpec((d, d), const),
            pl.BlockSpec((tm, d), row),
            _mod_spec(layer, tm, lay),
            pl.BlockSpec((1, d), const),
            pl.BlockSpec((d, LANES), const),
            pl.BlockSpec((d, LANES), const),
        ],
        out_specs=[pl.BlockSpec((tm, d), row), pl.BlockSpec((tm, d), row), pl.BlockSpec((tm, LANES), row)],
        out_shape=[jax.ShapeDtypeStruct((t, d), F32), jax.ShapeDtypeStruct((t, d), F32),
                   jax.ShapeDtypeStruct((t, LANES), F32)],
        compiler_params=_params(("arbitrary",)),
        name="out_proj_residual",
    )(y, w_out, x, mod, norm2.reshape(1, d), wr_hi, wr_lo)


def _topk_kernel(lg_ref, bias_ref, tri_ref, idx_ref, wt_ref, rank_ref, cnt_ref, run_ref):
    ne, ng = N_EXPERTS, N_GROUPS
    per = ne // ng
    lt = lg_ref[...].T
    tm = lt.shape[1]
    sc = jax.nn.sigmoid(lt[0:ne])
    ch = sc + bias_ref[...]
    ch3 = ch.reshape(ng, per, tm)
    neg = -jnp.inf
    sub = lax.broadcasted_iota(jnp.int32, (ng, per, tm), 1)
    m1 = jnp.max(ch3, axis=1, keepdims=True)
    i1 = jnp.min(jnp.where(ch3 == m1, sub, per), axis=1, keepdims=True)
    m2 = jnp.max(jnp.where(sub == i1, neg, ch3), axis=1, keepdims=True)
    gs = (m1 + m2).reshape(ng, tm)
    giota = lax.broadcasted_iota(jnp.int32, (ng, tm), 0)
    sel = jnp.zeros((ng, tm), F32)
    cur = gs
    for _ in range(TOPK_GROUPS):
        m = jnp.max(cur, axis=0, keepdims=True)
        ix = jnp.min(jnp.where(cur == m, giota, ng), axis=0, keepdims=True)
        hit = giota == ix
        sel = jnp.where(hit, 1.0, sel)
        cur = jnp.where(hit, neg, cur)
    masked = jnp.where(sel.reshape(ng, 1, tm) > 0, ch3, neg).reshape(ne, tm)
    eiota = lax.broadcasted_iota(jnp.int32, (ne, tm), 0)
    idxs, ws, hits = [], [], []
    for _ in range(TOP_K):
        m = jnp.max(masked, axis=0, keepdims=True)
        ix = jnp.min(jnp.where(masked == m, eiota, ne), axis=0, keepdims=True)
        hit = eiota == ix
        ws.append(jnp.sum(jnp.where(hit, sc, 0.0), axis=0, keepdims=True))
        idxs.append(ix)
        hits.append(hit)
        masked = jnp.where(hit, neg, masked)
    wall = jnp.concatenate(ws, axis=0)
    idx_ref[...] = jnp.concatenate(idxs, axis=0)
    wt_ref[...] = wall / jnp.sum(wall, axis=0, keepdims=True) * ROUTED_SCALE

    @pl.when(pl.program_id(0) == 0)
    def _():
        run_ref[...] = jnp.zeros(run_ref.shape, F32)

    chosen = hits[0].astype(F32)
    for hit in hits[1:]:
        chosen = chosen + hit.astype(F32)
    before = run_ref[:, 0:1] + _dot(chosen.astype(BF16), tri_ref[...])
    ranks = [jnp.sum(jnp.where(hit, before, 0.0), axis=0, keepdims=True) for hit in hits]
    rank_ref[...] = jnp.concatenate(ranks, axis=0).astype(jnp.int32)
    run_ref[...] = run_ref[...] + jnp.sum(chosen, axis=1, keepdims=True)
    cnt_ref[...] = run_ref[...]


def route_topk(logits, bias):
    t = logits.shape[0]
    tm = _pick(t, ROW_TILE)
    strict_upper = jnp.asarray(np.triu(np.ones((tm, tm), np.float32), 1), BF16)
    col = lambda i: (0, i)
    idx_t, wt_t, rank_t, cnt = pl.pallas_call(
        _topk_kernel,
        grid=(t // tm,),
        in_specs=[pl.BlockSpec((tm, LANES), lambda i: (i, 0)), pl.BlockSpec((N_EXPERTS, 1), lambda i: (0, 0)),
                  pl.BlockSpec((tm, tm), lambda i: (0, 0))],
        out_specs=[pl.BlockSpec((TOP_K, tm), col), pl.BlockSpec((TOP_K, tm), col), pl.BlockSpec((TOP_K, tm), col),
                   pl.BlockSpec((N_EXPERTS, LANES), lambda i: (0, 0))],
        out_shape=[jax.ShapeDtypeStruct((TOP_K, t), jnp.int32), jax.ShapeDtypeStruct((TOP_K, t), F32),
                   jax.ShapeDtypeStruct((TOP_K, t), jnp.int32), jax.ShapeDtypeStruct((N_EXPERTS, LANES), F32)],
        scratch_shapes=[pltpu.VMEM((N_EXPERTS, LANES), F32)],
        compiler_params=_params(("arbitrary",)),
        name="route_topk",
    )(logits, bias.astype(F32).reshape(N_EXPERTS, 1), strict_upper)
    return idx_t, wt_t, rank_t, cnt[:, 0].astype(jnp.int32)


def moe_layout(idx_t, rank_t, counts, bm):
    k, t = idx_t.shape
    n_blocks = -(-(t * k) // bm) + N_EXPERTS
    padded = (counts + bm - 1) // bm * bm
    pend = jnp.cumsum(padded)
    pstart = pend - padded
    experts = jnp.arange(N_EXPERTS, dtype=jnp.int32)[:, None, None]
    pos_t = rank_t + jnp.sum(jnp.where(idx_t[None] == experts, pstart[:, None, None], 0), axis=0)
    n_used = (pend[-1] // bm).astype(jnp.int32)
    blk = jnp.minimum(jnp.arange(n_blocks, dtype=jnp.int32), n_used - 1) * bm
    blk_e = jnp.minimum(jnp.sum((pend[None, :] <= blk[:, None]).astype(jnp.int32), axis=1), N_EXPERTS - 1)
    return pos_t.astype(jnp.int32), blk_e.astype(jnp.int32), n_used.reshape(1), n_blocks


def _pack_pairs(x):
    n = x.shape[1] // 2
    lo = lax.bitcast_convert_type(x[:, :n].astype(BF16).astype(F32), jnp.uint32)
    hi = lax.bitcast_convert_type(x[:, n:].astype(BF16).astype(F32), jnp.uint32)
    return (lo >> 16) | (hi & jnp.uint32(0xFFFF0000))


def _unpack_pairs(w):
    lo = lax.bitcast_convert_type(w << 16, F32)
    hi = lax.bitcast_convert_type(w & jnp.uint32(0xFFFF0000), F32)
    return lo, hi


def _row_copy(src_ref, src_row, dst_ref, dst_row, sem):
    return pltpu.make_async_copy(src_ref.at[pl.ds(src_row, 1)], dst_ref.at[pl.ds(dst_row, 1)], sem)


def _dispatch_kernel(pos_ref, h_ref, xs_in_ref, xs_ref, buf_ref, sem_ref, *, tm, nsteps):
    del xs_in_ref
    i = pl.program_id(0)
    slot = i % 2

    def wait_slot(s):
        for _ in range(TOP_K):
            pltpu.make_async_copy(buf_ref.at[s], xs_ref.at[pl.ds(0, tm)], sem_ref.at[s]).wait()

    @pl.when(i >= 2)
    def _():
        wait_slot(slot)

    buf_ref[slot] = _pack_pairs(h_ref[...])

    def body(tok, carry):
        for k in range(TOP_K):
            _row_copy(buf_ref.at[slot], tok, xs_ref, pos_ref[k, tok], sem_ref.at[slot]).start()
        return carry

    lax.fori_loop(0, tm, body, 0)

    @pl.when(i == nsteps - 1)
    def _():
        if nsteps > 1:
            wait_slot(1 - slot)
        wait_slot(slot)


def moe_dispatch(h2, pos_t, xs_buf):
    t, d = h2.shape
    tm = _pick(t, MOE_ROW_TILE)
    nsteps = t // tm
    return pl.pallas_call(
        functools.partial(_dispatch_kernel, tm=tm, nsteps=nsteps),
        grid=(nsteps,),
        in_specs=[
            pl.BlockSpec((TOP_K, tm), lambda i: (0, i), memory_space=pltpu.SMEM),
            pl.BlockSpec((tm, d), lambda i: (i, 0)),
            pl.BlockSpec(memory_space=pl.ANY),
        ],
        out_specs=pl.BlockSpec(memory_space=pl.ANY),
        out_shape=jax.ShapeDtypeStruct(xs_buf.shape, xs_buf.dtype),
        input_output_aliases={2: 0},
        scratch_shapes=[pltpu.VMEM((2, tm) + xs_buf.shape[1:], jnp.uint32), pltpu.SemaphoreType.DMA((2,))],
        compiler_params=_params(("arbitrary",)),
        name="moe_dispatch",
    )(pos_t, h2, xs_buf)


def _moe_kernel(be_ref, nu_ref, x_ref, w1_ref, w3_ref, w2_ref, y_ref, w13_s, w2_s, *, ed, half):
    i = pl.program_id(0)
    prev = be_ref[jnp.maximum(i - 1, 0)]
    fresh = jnp.logical_or(i == 0, be_ref[i] != prev)

    @pl.when(fresh)
    def _():
        w13_s[:, 0:ed] = w1_ref[...].astype(BF16)
        w13_s[:, ed:2 * ed] = w3_ref[...].astype(BF16)
        w2_s[...] = w2_ref[...].astype(BF16)

    @pl.when(i < nu_ref[0])
    def _():
        lo, hi = _unpack_pairs(x_ref[...])
        a = _dot(lo.astype(BF16), w13_s[0:half, :]) + _dot(hi.astype(BF16), w13_s[half:2 * half, :])
        act = (_silu(a[:, 0:ed]) * a[:, ed:2 * ed]).astype(BF16)
        y_ref[...] = _pack_pairs(_dot(act, w2_s[...]))

    @pl.when(i >= nu_ref[0])
    def _():
        y_ref[...] = jnp.zeros(y_ref.shape, y_ref.dtype)


def moe_experts(xs, blk_e, n_used, w1, w3, w2, bm):
    n_rows, half = xs.shape
    d = 2 * half
    ed = w1.shape[-1]
    n_blocks = n_rows // bm
    grid_spec = pltpu.PrefetchScalarGridSpec(
        num_scalar_prefetch=2,
        grid=(n_blocks,),
        in_specs=[
            pl.BlockSpec((bm, half), lambda i, be, nu: (i, 0)),
            pl.BlockSpec((None, d, ed), lambda i, be, nu: (be[i], 0, 0)),
            pl.BlockSpec((None, d, ed), lambda i, be, nu: (be[i], 0, 0)),
            pl.BlockSpec((None, ed, d), lambda i, be, nu: (be[i], 0, 0)),
        ],
        out_specs=pl.BlockSpec((bm, half), lambda i, be, nu: (i, 0)),
        scratch_shapes=[pltpu.VMEM((d, 2 * ed), BF16), pltpu.VMEM((ed, d), BF16)],
    )
    return pl.pallas_call(
        functools.partial(_moe_kernel, ed=ed, half=half),
        grid_spec=grid_spec,
        out_shape=jax.ShapeDtypeStruct(xs.shape, jnp.uint32),
        compiler_params=_params(("arbitrary",)),
        name="moe_experts",
    )(blk_e, n_used, xs, w1, w3, w2)


def _shared_kernel(pos_ref, posn_ref, wt_ref, h2_ref, w13_ref, w2_ref, ys_ref, x_ref, mod_ref, ng_ref, nmod_ref,
                   xo_ref, ho_ref, ybuf_ref, sem_ref, *, d, ed, tm, nsteps, final):
    i = pl.program_id(0)
    slot = i % 2

    def issue(p_ref, s):
        def body(tok, carry):
            for k in range(TOP_K):
                _row_copy(ys_ref, p_ref[k, tok], ybuf_ref.at[s, k], tok, sem_ref.at[s]).start()
            return carry
        lax.fori_loop(0, tm, body, 0)

    @pl.when(i == 0)
    def _():
        issue(pos_ref, 0)

    @pl.when(i + 1 < nsteps)
    def _():
        issue(posn_ref, 1 - slot)

    m = mod_ref[...]
    a = _dot(h2_ref[...].astype(BF16), w13_ref[...])
    act = (_silu(a[:, 0:ed]) * a[:, ed:2 * ed]).astype(BF16)
    f = _dot(act, w2_ref[...])

    for k in range(TOP_K):
        pltpu.make_async_copy(ys_ref.at[pl.ds(0, tm)], ybuf_ref.at[slot, k], sem_ref.at[slot]).wait()
    half = d // 2
    wks = [wt_ref[:, k:k + 1] for k in range(TOP_K)]
    r_lo, r_hi = [], []
    for c in range(half // LANES):
        cs = slice(c * LANES, (c + 1) * LANES)
        acc_lo = acc_hi = None
        for k in range(TOP_K):
            lo, hi = _unpack_pairs(ybuf_ref[slot, k, :, cs])
            acc_lo = wks[k] * lo if acc_lo is None else acc_lo + wks[k] * lo
            acc_hi = wks[k] * hi if acc_hi is None else acc_hi + wks[k] * hi
        r_lo.append(acc_lo)
        r_hi.append(acc_hi)
    f = f + jnp.concatenate(r_lo + r_hi, axis=1)
    xn = x_ref[...] + _mod_part(m, 5, d) * f
    xo_ref[...] = xn
    y = _rms(xn, ng_ref[...])
    if not final:
        nm = nmod_ref[...]
        y = y * (1.0 + _mod_part(nm, 1, d)) + _mod_part(nm, 0, d)
    ho_ref[...] = y.astype(ho_ref.dtype)


def shared_residual(h2, sw13, sw2, ys, pos_t, wts, x, mod, layer, next_g, final, lay):
    t, d = x.shape
    ed = sw2.shape[0]
    tm = _pick(lay["n_ctx"], RES_ROW_TILE, BF16_SUBLANES)
    nsteps = t // tm
    row = lambda i: (i, 0)
    const = lambda i: (0, 0)
    next_layer = layer if final else layer + 1
    return pl.pallas_call(
        functools.partial(_shared_kernel, d=d, ed=ed, tm=tm, nsteps=nsteps, final=final),
        grid=(nsteps,),
        in_specs=[
            pl.BlockSpec((TOP_K, tm), lambda i: (0, i), memory_space=pltpu.SMEM),
            pl.BlockSpec((TOP_K, tm), lambda i: (0, jnp.minimum(i + 1, nsteps - 1)), memory_space=pltpu.SMEM),
            pl.BlockSpec((tm, TOP_K), row),
            pl.BlockSpec((tm, d), row),
            pl.BlockSpec((d, 2 * ed), const),
            pl.BlockSpec((ed, d), const),
            pl.BlockSpec(memory_space=pl.ANY),
            pl.BlockSpec((tm, d), row),
            _mod_spec(layer, tm, lay),
            pl.BlockSpec((1, d), const),
            _mod_spec(next_layer, tm, lay),
        ],
        out_specs=[pl.BlockSpec((tm, d), row), pl.BlockSpec((tm, d), row)],
        out_shape=[jax.ShapeDtypeStruct((t, d), F32), jax.ShapeDtypeStruct((t, d), F32 if final else BF16)],
        scratch_shapes=[pltpu.VMEM((2, TOP_K, tm, d // 2), jnp.uint32), pltpu.SemaphoreType.DMA((2,))],
        compiler_params=_params(("arbitrary",)),
        name="shared_residual",
    )(pos_t, pos_t, wts, h2, sw13, sw2, ys, x, mod, next_g.reshape(1, d), mod)


def _rope_tables(lay):
    half = MLA_ROPE // 4
    s = lay["S"]
    freq = ROPE_BASE ** (-np.arange(half, dtype=np.float64) / half)
    tpos = np.arange(s)
    ang_r = (tpos // GRID_W)[:, None] * freq
    ang_c = (tpos % GRID_W)[:, None] * freq
    ang = np.concatenate([ang_r, ang_r, ang_c, ang_c], axis=1)
    cos_l = np.tile(np.cos(ang), (lay["B"], 2))
    sin_l = np.tile(np.sin(ang), (lay["B"], 2))
    cos = np.concatenate([np.ones((lay["n_ctx"], 2 * MLA_ROPE)), cos_l], axis=0).astype(np.float32)
    sin = np.concatenate([np.zeros((lay["n_ctx"], 2 * MLA_ROPE)), sin_l], axis=0).astype(np.float32)
    return jnp.asarray(cos), jnp.asarray(sin)


def _rot_cols(w):
    q = MLA_ROPE // 4
    a, b, c, e = w[..., 0:q], w[..., q:2 * q], w[..., 2 * q:3 * q], w[..., 3 * q:4 * q]
    return jnp.concatenate([-b, a, -e, c], axis=-1)


def _split_w_in_kernel(x_ref, main_ref, mla_ref, small_ref, *, o_small, o_cq, o_kr, o_gate, n_gate):
    main_ref[:, 0:o_small] = x_ref[:, 0:o_small].astype(main_ref.dtype)
    main_ref[:, o_small:o_small + n_gate] = x_ref[:, o_gate:o_gate + n_gate].astype(main_ref.dtype)
    mla_ref[...] = x_ref[:, o_cq:o_kr].astype(mla_ref.dtype)
    n_ab = o_cq - o_small
    small_ref[...] = jnp.zeros(small_ref.shape, small_ref.dtype)
    small_ref[:, 0:n_ab] = x_ref[:, o_small:o_cq]
    small_ref[:, n_ab:n_ab + MLA_ROPE] = x_ref[:, o_kr:o_gate]


def split_w_in(w_in_all, layer, offs):
    o_small, o_cq, o_kr, o_gate = offs
    _, d, d_in = w_in_all.shape
    n_gate = d_in - o_gate
    tm = _pick(d, LANES, SUBLANES)
    row = lambda i: (i, 0)
    return pl.pallas_call(
        functools.partial(_split_w_in_kernel, o_small=o_small, o_cq=o_cq, o_kr=o_kr, o_gate=o_gate, n_gate=n_gate),
        grid=(d // tm,),
        in_specs=[pl.BlockSpec((None, tm, d_in), lambda i: (layer, i, 0))],
        out_specs=[pl.BlockSpec((tm, o_small + n_gate), row), pl.BlockSpec((tm, o_kr - o_cq), row),
                   pl.BlockSpec((tm, LANES), row)],
        out_shape=[jax.ShapeDtypeStruct((d, o_small + n_gate), BF16), jax.ShapeDtypeStruct((d, o_kr - o_cq), BF16),
                   jax.ShapeDtypeStruct((d, LANES), F32)],
        compiler_params=_params(("arbitrary",)),
        name="split_w_in",
    )(w_in_all)


def _prep_layer_weights(w_in_all, layer, w_uq, w_ukv, w_branch, w_out, w_router, sw1, sw3, sw2, d):
    w = BRANCH_W
    o_small = 6 * w
    o_cq = o_small + 4 * DN_HEADS
    o_kr = o_cq + Q_LORA + KV_LORA
    o_gate = o_kr + MLA_ROPE
    w_main, w_mla, w_small = split_w_in(w_in_all, layer, (o_small, o_cq, o_kr, o_gate))
    n_ab = o_cq - o_small
    w_kr = w_small[:, n_ab:n_ab + MLA_ROPE]
    zpad = jnp.zeros((d, LANES - MLA_ROPE), F32)
    w_misc = jnp.concatenate(
        [w_kr, zpad, _rot_cols(w_kr), zpad, w_small[:, 0:n_ab], jnp.zeros((d, LANES - n_ab), F32)],
        axis=1).astype(BF16)
    uq = w_uq.reshape(Q_LORA, MLA_HEADS, MLA_QK)
    uq_r = uq[..., MLA_NOPE:]
    wq_ext = jnp.concatenate(
        [uq[..., :MLA_NOPE].reshape(Q_LORA, -1), uq_r.reshape(Q_LORA, -1), _rot_cols(uq_r).reshape(Q_LORA, -1)],
        axis=1).astype(BF16)
    wr_pad = jnp.zeros((d, LANES), F32).at[:, :N_EXPERTS].set(w_router.astype(F32))
    return dict(w_main=w_main, w_mla=w_mla, w_misc=w_misc, wq_ext=wq_ext, w_ukv=w_ukv.astype(BF16),
                w_branch=w_branch.astype(BF16), w_out=w_out.astype(BF16), wr_pad=wr_pad,
                sw13=jnp.concatenate([sw1, sw3], axis=1).astype(BF16), sw2=sw2.astype(BF16))


def kernel(x, c, ctx, c_ctx, w_mod, b_mod, norm1, norm2, w_in, gm_norm, gm_ws, gm_bs, dn_conv, dn_a_log,
           dn_dt_bias, dn_norm, mla_q_norm, mla_kv_norm, mla_w_uq, mla_w_ukv, w_branch, w_out, moe_router,
           moe_bias, moe_w1, moe_w3, moe_w2, shared_w1, shared_w3, shared_w2, final_norm):
    nb, s, d = x.shape
    cl = ctx.shape[1]
    depth = w_mod.shape[0]
    n_ctx = nb * cl
    t = n_ctx + nb * s
    lay = dict(B=nb, S=s, C=cl, D=d, n_ctx=n_ctx, T=t)
    assert nb + 1 <= SUBLANES and s % GRID_W == 0

    xs = jnp.concatenate([ctx.reshape(n_ctx, d), x.reshape(nb * s, d)], axis=0).astype(F32)
    cvec = jnp.zeros((SUBLANES, d), F32).at[0].set(c_ctx.astype(F32)).at[1:1 + nb].set(c.astype(F32))
    mod = modulation(cvec, w_mod, b_mod)
    cos2, sin2 = _rope_tables(lay)
    scale = MLA_QK ** -0.5 * LOG2E

    h = prenorm(xs, norm1[0], mod, 0, lay)
    out = None
    xs_buf = None
    for i in range(depth):
        last = i == depth - 1
        wts = _prep_layer_weights(w_in, i, mla_w_uq[i], mla_w_ukv[i], w_branch[i], w_out[i], moe_router[i],
                                  shared_w1[i], shared_w3[i], shared_w2[i], d)
        z_main = matmul(h, wts["w_main"], BF16, "in_proj_main")
        z_mla = matmul(h, wts["w_mla"], BF16, "in_proj_mla")
        z_misc = matmul(h, wts["w_misc"], F32, "in_proj_misc")

        br_a = gmlp(z_main, gm_norm[i], gm_ws[i], gm_bs[i], lay)

        qkv_c = dn_short_conv(z_main, dn_conv[i], lay)
        o_f, o_b = dn_scan(qkv_c, z_misc, dn_a_log[i], dn_dt_bias[i], lay)

        kc, vc = mla_kv(z_mla, z_misc, mla_kv_norm[i], wts["w_ukv"], cos2, sin2, 0, n_ctx)
        kl, vl = mla_kv(z_mla, z_misc, mla_kv_norm[i], wts["w_ukv"], cos2, sin2, n_ctx, nb * s)
        ql = mla_q(z_mla, mla_q_norm[i], wts["wq_ext"], cos2, sin2, n_ctx, nb * s, scale)
        attn_l = attention(ql, kc, vc, kl, vl, s, lay)
        qc = mla_q(z_mla, mla_q_norm[i], wts["wq_ext"], cos2, sin2, 0, n_ctx, scale)
        attn_c = attention(qc, kc, vc, None, None, cl, lay)

        y = merge_branches(br_a, o_f, o_b, z_main, attn_c, attn_l, dn_norm[i], wts["w_branch"], lay)
        xs, h2, logits = out_proj_residual(y, wts["w_out"], xs, mod, i, norm2[i], wts["wr_pad"], lay)

        idx_t, wt_t, rank_t, counts = route_topk(logits, moe_bias[i])
        bm = MOE_ROW_TILE
        pos_t, blk_e, n_used, n_blocks = moe_layout(idx_t, rank_t, counts, bm)
        if xs_buf is None:
            xs_buf = jnp.zeros((n_blocks * bm, d // 2), jnp.uint32)
        xs_buf = moe_dispatch(h2, pos_t, xs_buf)
        y_sorted = moe_experts(xs_buf, blk_e, n_used, moe_w1[i], moe_w3[i], moe_w2[i], bm)

        next_g = final_norm if last else norm1[i + 1]
        xs, h = shared_residual(h2, wts["sw13"], wts["sw2"], y_sorted, pos_t, wt_t.T, xs, mod, i, next_g, last, lay)
        out = h
    return out[n_ctx:].reshape(nb, s, d).astype(x.dtype)
```

```python
import functools
import math

import numpy as np
import jax
import jax.numpy as jnp
from jax import lax
from jax.experimental import pallas as pl
from jax.experimental.pallas import tpu as pltpu

F32 = jnp.float32
BF16 = jnp.bfloat16

GRID_W = 64
EPS = 1e-6
GM_CHUNK = 128
GM_GROUPS = 8
DN_HEADS = 8
DN_HEAD_DIM = 128
DN_CHUNK = 64
DN_CONV = 5
MLA_HEADS = 8
MLA_NOPE = 128
MLA_ROPE = 64
MLA_V = 128
MLA_QK = MLA_NOPE + MLA_ROPE
MLA_VE = 2 * MLA_V
Q_LORA = 512
KV_LORA = 512
ROPE_BASE = 10000.0
BRANCH_W = 1024
N_EXPERTS = 64
TOP_K = 8
N_GROUPS = 8
TOPK_GROUPS = 4
ROUTED_SCALE = 2.5
LOG2E = 1.4426950408889634

LANES = 128
SUBLANES = 8
BF16_SUBLANES = 16
VMEM_LIMIT_MB = 56

ROW_TILE = 512
RES_ROW_TILE = 256
CONV_ROW_TILE = 256
COL_TILE = 1024
MOE_ROW_TILE = 256
ATTN_Q_TILE = 1024
ATTN_KV_TILE = 1024


def _pick(n, pref, mult=LANES):
    if n <= pref:
        return n
    for t in range(pref - pref % mult, 0, -mult):
        if n % t == 0:
            return t
    return n


def _params(sem, mb=VMEM_LIMIT_MB):
    return pltpu.CompilerParams(dimension_semantics=sem, vmem_limit_bytes=mb * 1024 * 1024)


def _silu(x):
    return x * jax.nn.sigmoid(x)


def _gelu(x):
    return 0.5 * x * (1.0 + jnp.tanh(0.7978845608028654 * (x + 0.044715 * x * x * x)))


def _dot(a, b):
    return jnp.dot(a, b, preferred_element_type=F32)


def _dot_nt(a, b):
    return lax.dot_general(a, b, (((1,), (1,)), ((), ())), preferred_element_type=F32)


def _dot_tn(a, b):
    return lax.dot_general(a, b, (((0,), (0,)), ((), ())), preferred_element_type=F32)


def _mod_kernel(c_ref, w_ref, b_ref, o_ref):
    s = _silu(c_ref[...])
    o_ref[...] = _dot(s.astype(BF16), w_ref[...].astype(BF16)) + b_ref[...]


def modulation(cvec, w_mod, b_mod):
    nl, d, n6 = w_mod.shape
    tn = _pick(n6, COL_TILE)
    out = pl.pallas_call(
        _mod_kernel,
        grid=(nl, n6 // tn),
        in_specs=[
            pl.BlockSpec((SUBLANES, d), lambda l, n: (0, 0)),
            pl.BlockSpec((None, d, tn), lambda l, n: (l, 0, n)),
            pl.BlockSpec((None, 1, tn), lambda l, n: (l, 0, n)),
        ],
        out_specs=pl.BlockSpec((None, SUBLANES, tn), lambda l, n: (l, 0, n)),
        out_shape=jax.ShapeDtypeStruct((nl, SUBLANES, n6), F32),
        compiler_params=_params(("arbitrary", "arbitrary")),
        name="modulation",
    )(cvec, w_mod, b_mod.reshape(nl, 1, n6))
    return out.reshape(nl, SUBLANES, 1, n6)


def _seg_of_block(i, tm, lay):
    nctx_blk = lay["n_ctx"] // tm
    lat_bps = lay["S"] // tm
    return jnp.where(i < nctx_blk, 0, 1 + (i - nctx_blk) // lat_bps)


def _mod_spec(layer, tm, lay, row0_blk=0):
    n6 = 6 * lay["D"]
    return pl.BlockSpec((None, None, 1, n6), lambda i: (layer, _seg_of_block(i + row0_blk, tm, lay), 0, 0))


def _mod_part(m, k, d):
    return m[:, k * d:(k + 1) * d]


def _rms(x, g):
    return x * lax.rsqrt(jnp.mean(x * x, axis=-1, keepdims=True) + EPS) * g


def _prenorm_kernel(x_ref, g_ref, mod_ref, o_ref, *, d):
    m = mod_ref[...]
    y = _rms(x_ref[...], g_ref[...])
    o_ref[...] = (y * (1.0 + _mod_part(m, 1, d)) + _mod_part(m, 0, d)).astype(o_ref.dtype)


def prenorm(x, g, mod, layer, lay):
    t, d = x.shape
    tm = _pick(lay["n_ctx"], ROW_TILE, SUBLANES)
    return pl.pallas_call(
        functools.partial(_prenorm_kernel, d=d),
        grid=(t // tm,),
        in_specs=[
            pl.BlockSpec((tm, d), lambda i: (i, 0)),
            pl.BlockSpec((1, d), lambda i: (0, 0)),
            _mod_spec(layer, tm, lay),
        ],
        out_specs=pl.BlockSpec((tm, d), lambda i: (i, 0)),
        out_shape=jax.ShapeDtypeStruct((t, d), BF16),
        compiler_params=_params(("arbitrary",)),
        name="prenorm",
    )(x, g.reshape(1, d), mod)


def _mm_kernel(x_ref, w_ref, o_ref):
    o_ref[...] = _dot(x_ref[...], w_ref[...]).astype(o_ref.dtype)


def matmul(x, w, out_dtype, name):
    t, k = x.shape
    n = w.shape[1]
    tm = _pick(t, ROW_TILE, BF16_SUBLANES)
    tn = _pick(n, COL_TILE)
    return pl.pallas_call(
        _mm_kernel,
        grid=(n // tn, t // tm),
        in_specs=[
            pl.BlockSpec((tm, k), lambda j, i: (i, 0)),
            pl.BlockSpec((k, tn), lambda j, i: (0, j)),
        ],
        out_specs=pl.BlockSpec((tm, tn), lambda j, i: (i, j)),
        out_shape=jax.ShapeDtypeStruct((t, n), out_dtype),
        compiler_params=_params(("arbitrary", "arbitrary")),
        name=name,
    )(x, w)


def _gmlp_kernel(u_ref, v_ref, gn_ref, ws_ref, bias_ref, o_ref, *, nchunk):
    for j in range(nchunk):
        rs = slice(j * GM_CHUNK, (j + 1) * GM_CHUNK)
        v = _gelu(v_ref[rs, :].astype(F32))
        vb = _rms(v, gn_ref[...]).astype(BF16)
        u = _gelu(u_ref[rs, :].astype(F32))
        for g in range(GM_GROUPS):
            cs = slice(g * LANES, (g + 1) * LANES)
            mixed = _dot(ws_ref[g], vb[:, cs]) + bias_ref[:, cs]
            o_ref[rs, cs] = (u[:, cs] * mixed).astype(o_ref.dtype)


def gmlp(z_main, gm_norm, gm_ws, gm_bs, lay):
    t = z_main.shape[0]
    w = BRANCH_W
    tm = _pick(lay["n_ctx"], ROW_TILE, GM_CHUNK)
    bias = jnp.repeat(gm_bs.T.astype(F32), w // GM_GROUPS, axis=1)
    return pl.pallas_call(
        functools.partial(_gmlp_kernel, nchunk=tm // GM_CHUNK),
        grid=(t // tm,),
        in_specs=[
            pl.BlockSpec((tm, w), lambda i: (i, 0)),
            pl.BlockSpec((tm, w), lambda i: (i, 1)),
            pl.BlockSpec((1, w), lambda i: (0, 0)),
            pl.BlockSpec((GM_GROUPS, GM_CHUNK, GM_CHUNK), lambda i: (0, 0, 0)),
            pl.BlockSpec((GM_CHUNK, w), lambda i: (0, 0)),
        ],
        out_specs=pl.BlockSpec((tm, w), lambda i: (i, 0)),
        out_shape=jax.ShapeDtypeStruct((t, w), BF16),
        compiler_params=_params(("arbitrary",)),
        name="gmlp",
    )(z_main, z_main, gm_norm.reshape(1, w), gm_ws.astype(BF16), bias)


def _conv_kernel(x_ref, hp_ref, hn_ref, w_ref, o_ref, xe_ref, *, tm, nctx_blk, ctx_bps, lat_bps):
    j = pl.program_id(0)
    i = pl.program_id(1)
    li = i - nctx_blk
    seg_start = jnp.where(i < nctx_blk, (i % ctx_bps) == 0, (li % lat_bps) == 0)
    seg_end = jnp.where(i < nctx_blk, ((i + 1) % ctx_bps) == 0, ((li + 1) % lat_bps) == 0)
    halo = BF16_SUBLANES
    xe_ref[0:halo, :] = jnp.where(seg_start, 0.0, hp_ref[...].astype(F32))
    xe_ref[halo:halo + tm, :] = x_ref[...].astype(F32)
    xe_ref[halo + tm:2 * halo + tm, :] = jnp.where(seg_end, 0.0, hn_ref[...].astype(F32))
    base = halo - DN_CONV // 2
    acc = w_ref[0:1, :] * xe_ref[base:base + tm, :]
    for tap in range(1, DN_CONV):
        acc = acc + w_ref[tap:tap + 1, :] * xe_ref[base + tap:base + tap + tm, :]
    y = _silu(acc)
    unit = j < 2
    for h in range(DN_HEADS):
        cs = slice(h * DN_HEAD_DIM, (h + 1) * DN_HEAD_DIM)
        yh = y[:, cs]
        nrm = yh * lax.rsqrt(jnp.sum(yh * yh, axis=-1, keepdims=True) + EPS)
        o_ref[:, cs] = jnp.where(unit, nrm, yh).astype(o_ref.dtype)


def dn_short_conv(z_main, conv_w, lay):
    t = z_main.shape[0]
    w = BRANCH_W
    tm = _pick(lay["C"], CONV_ROW_TILE, BF16_SUBLANES)
    halo = BF16_SUBLANES
    hb = tm // halo
    nhalo = t // halo
    wpad = jnp.zeros((SUBLANES, 3 * w), F32).at[:DN_CONV].set(conv_w.astype(F32))
    kern = functools.partial(_conv_kernel, tm=tm, nctx_blk=lay["n_ctx"] // tm, ctx_bps=lay["C"] // tm,
                             lat_bps=lay["S"] // tm)
    return pl.pallas_call(
        kern,
        grid=(3, t // tm),
        in_specs=[
            pl.BlockSpec((tm, w), lambda j, i: (i, 2 + j)),
            pl.BlockSpec((halo, w), lambda j, i: (jnp.maximum(i * hb - 1, 0), 2 + j)),
            pl.BlockSpec((halo, w), lambda j, i: (jnp.minimum((i + 1) * hb, nhalo - 1), 2 + j)),
            pl.BlockSpec((SUBLANES, w), lambda j, i: (0, j)),
        ],
        out_specs=pl.BlockSpec((tm, w), lambda j, i: (i, j)),
        out_shape=jax.ShapeDtypeStruct((t, 3 * w), BF16),
        scratch_shapes=[pltpu.VMEM((tm + 2 * halo, w), F32)],
        compiler_params=_params(("arbitrary", "arbitrary")),
        name="dn_conv",
    )(z_main, z_main, z_main, wpad)


def _dn_masks():
    c = DN_CHUNK
    r = np.arange(c)[:, None]
    s = np.arange(c)[None, :]
    tri = np.stack([r >= s, r <= s]).astype(np.float32)
    strict = np.stack([r > s, r < s]).astype(np.float32)
    lv = []
    b = 1
    while b < c:
        same = (r // (2 * b)) == (s // (2 * b))
        lo = same & ((r // b) % 2 == 1) & ((s // b) % 2 == 0)
        up = same & ((r // b) % 2 == 0) & ((s // b) % 2 == 1)
        lv.append(np.stack([lo, up]))
        b *= 2
    lvl = np.stack(lv, axis=1).astype(np.float32)
    return tri, strict, lvl, np.eye(c, dtype=np.float32)


def _dn_kernel(qf_ref, qb_ref, abf_ref, abb_ref, alog_ref, dt_ref, tri_ref, strict_ref, lvl_ref, eye_ref,
               of_ref, ob_ref, s_ref, *, nlevels):
    step = pl.program_id(1)

    @pl.when(step == 0)
    def _():
        s_ref[...] = jnp.zeros(s_ref.shape, F32)

    hd = DN_HEAD_DIM
    w = DN_HEADS * hd
    nh = DN_HEADS
    cc = DN_CHUNK
    eye = eye_ref[...]
    inst = []
    for d, (x_ref, ab_ref, o_ref) in enumerate(((qf_ref, abf_ref, of_ref), (qb_ref, abb_ref, ob_ref))):
        ab = ab_ref[...]
        g_all = -jnp.exp(alog_ref[...]) * jax.nn.softplus(ab + dt_ref[...])
        beta_all = jax.nn.sigmoid(ab)
        gam_c = jnp.dot(tri_ref[d], g_all, precision=lax.Precision.HIGHEST, preferred_element_type=F32)
        gam_r = gam_c.T
        for h in range(nh):
            lane = d * nh + h
            inst.append(dict(
                d=d, lane=lane, h=h, slot=lane, o_ref=o_ref,
                q=x_ref[:, h * hd:(h + 1) * hd].astype(F32),
                k=x_ref[:, w + h * hd:w + (h + 1) * hd].astype(F32),
                v=x_ref[:, 2 * w + h * hd:2 * w + (h + 1) * hd].astype(F32),
                gc=gam_c[:, lane:lane + 1],
                gr=gam_r[lane:lane + 1, :],
                bc=beta_all[:, 2 * nh + lane:2 * nh + lane + 1],
                st=s_ref[lane]))
    for it in inst:
        tri = tri_ref[it["d"]]
        diff = it["gc"] - it["gr"]
        it["dec"] = jnp.where(tri > 0, jnp.exp(jnp.where(tri > 0, diff, 0.0)), 0.0)
        it["kb"] = it["k"] * it["bc"]
        it["qs"] = it["q"] * (hd ** -0.5)
    for it in inst:
        kq = _dot_nt(jnp.concatenate([it["kb"], it["qs"]], axis=0).astype(BF16), it["k"].astype(BF16))
        it["a"] = kq[:cc] * it["dec"] * strict_ref[it["d"]]
        it["attn"] = (kq[cc:] * it["dec"]).astype(BF16)
    for it in inst:
        it["x"] = eye - it["a"] * lvl_ref[it["d"], 0]
    for lv in range(1, nlevels):
        for it in inst:
            it["xb"] = it["x"].astype(BF16)
            it["p"] = _dot(it["xb"], (it["a"] * lvl_ref[it["d"], lv]).astype(BF16)).astype(BF16)
        for it in inst:
            it["x"] = it["x"] - _dot(it["p"], it["xb"])
    for it in inst:
        eg = jnp.exp(it["gc"])
        rhs = jnp.concatenate([it["kb"] * eg, it["v"] * it["bc"]], axis=1).astype(BF16)
        it["sol"] = _dot(it["x"].astype(BF16), rhs)
        last = cc - 1 if it["d"] == 0 else 0
        g_last = it["gc"][last:last + 1, :]
        it["gtot"] = jnp.exp(g_last)
        it["kd"] = (it["k"] * jnp.exp(g_last - it["gc"])).astype(BF16)
        it["qd"] = it["qs"] * eg
    for it in inst:
        wq = jnp.concatenate([it["sol"][:, :hd], it["qd"]], axis=0).astype(BF16)
        it["r"] = _dot(wq, it["st"].astype(BF16))
    for it in inst:
        it["vn"] = (it["sol"][:, hd:] - it["r"][:cc]).astype(BF16)
    for it in inst:
        it["o"] = it["r"][cc:] + _dot(it["attn"], it["vn"])
        it["sn"] = it["st"] * it["gtot"] + _dot_tn(it["kd"], it["vn"])
    for it in inst:
        it["o_ref"][:, it["h"] * hd:(it["h"] + 1) * hd] = it["o"]
        s_ref[it["slot"]] = it["sn"]


def dn_scan(qkv_c, z_misc, a_log, dt_bias, lay):
    t = qkv_c.shape[0]
    w = BRANCH_W
    c = DN_CHUNK
    nb = lay["B"]
    ncc = lay["C"] // c
    ncl = lay["S"] // c
    nsteps = ncc + ncl

    def fidx(b, s):
        return jnp.where(s < ncc, b * ncc + s, nb * ncc + b * ncl + (s - ncc))

    def bidx(b, s):
        return jnp.where(s < ncc, b * ncc + (ncc - 1 - s), nb * ncc + b * ncl + (ncl - 1 - (s - ncc)))

    tri, strict, lvl, eye = _dn_masks()
    nlevels = lvl.shape[1]
    nh = DN_HEADS
    alog_row = jnp.zeros((1, LANES), F32).at[0, :2 * nh].set(a_log.reshape(-1).astype(F32))
    dt_row = jnp.zeros((1, LANES), F32).at[0, :2 * nh].set(dt_bias.reshape(-1).astype(F32))
    const2 = lambda b, s: (0, 0)
    const3 = lambda b, s: (0, 0, 0)
    const4 = lambda b, s: (0, 0, 0, 0)
    return pl.pallas_call(
        functools.partial(_dn_kernel, nlevels=nlevels),
        grid=(nb, nsteps),
        in_specs=[
            pl.BlockSpec((c, 3 * w), lambda b, s: (fidx(b, s), 0)),
            pl.BlockSpec((c, 3 * w), lambda b, s: (bidx(b, s), 0)),
            pl.BlockSpec((c, LANES), lambda b, s: (fidx(b, s), 2)),
            pl.BlockSpec((c, LANES), lambda b, s: (bidx(b, s), 2)),
            pl.BlockSpec((1, LANES), const2),
            pl.BlockSpec((1, LANES), const2),
            pl.BlockSpec((2, c, c), const3),
            pl.BlockSpec((2, c, c), const3),
            pl.BlockSpec((2, nlevels, c, c), const4),
            pl.BlockSpec((c, c), const2),
        ],
        out_specs=[
            pl.BlockSpec((c, w), lambda b, s: (fidx(b, s), 0)),
            pl.BlockSpec((c, w), lambda b, s: (bidx(b, s), 0)),
        ],
        out_shape=[jax.ShapeDtypeStruct((t, w), F32), jax.ShapeDtypeStruct((t, w), F32)],
        scratch_shapes=[pltpu.VMEM((2 * nh, DN_HEAD_DIM, DN_HEAD_DIM), F32)],
        compiler_params=_params(("arbitrary", "arbitrary")),
        name="dn_scan",
    )(qkv_c, qkv_c, z_misc, z_misc, alog_row, dt_row, jnp.asarray(tri), jnp.asarray(strict), jnp.asarray(lvl),
      jnp.asarray(eye))


def _mla_q_kernel(z_ref, qn_ref, w_ref, cos_ref, sin_ref, o_ref, *, scale):
    nh = MLA_HEADS
    xn = _rms(z_ref[...].astype(F32), qn_ref[...]).astype(BF16)
    na = nh * MLA_NOPE
    nr = nh * MLA_ROPE
    qa = _dot(xn, w_ref[:, 0:na])
    qr = _dot(xn, w_ref[:, na:na + nr])
    qt = _dot(xn, w_ref[:, na + nr:na + 2 * nr])
    reps = nr // LANES
    cos = jnp.concatenate([cos_ref[...]] * reps, axis=1)
    sin = jnp.concatenate([sin_ref[...]] * reps, axis=1)
    qrr = qr * cos + qt * sin
    for h in range(nh):
        o_ref[h, :, 0:MLA_NOPE] = (qa[:, h * MLA_NOPE:(h + 1) * MLA_NOPE] * scale).astype(o_ref.dtype)
        o_ref[h, :, MLA_NOPE:MLA_QK] = (qrr[:, h * MLA_ROPE:(h + 1) * MLA_ROPE] * scale).astype(o_ref.dtype)


def mla_q(z_mla, q_norm, wq_ext, cos2, sin2, row0, nrows, scale):
    tm = _pick(nrows, ROW_TILE, BF16_SUBLANES)
    assert row0 % tm == 0
    r0 = row0 // tm
    return pl.pallas_call(
        functools.partial(_mla_q_kernel, scale=scale),
        grid=(nrows // tm,),
        in_specs=[
            pl.BlockSpec((tm, Q_LORA), lambda i: (i + r0, 0)),
            pl.BlockSpec((1, Q_LORA), lambda i: (0, 0)),
            pl.BlockSpec(wq_ext.shape, lambda i: (0, 0)),
            pl.BlockSpec((tm, LANES), lambda i: (i + r0, 0)),
            pl.BlockSpec((tm, LANES), lambda i: (i + r0, 0)),
        ],
        out_specs=pl.BlockSpec((MLA_HEADS, tm, MLA_QK), lambda i: (0, i, 0)),
        out_shape=jax.ShapeDtypeStruct((MLA_HEADS, nrows, MLA_QK), BF16),
        compiler_params=_params(("arbitrary",)),
        name="mla_q",
    )(z_mla, q_norm.reshape(1, Q_LORA), wq_ext, cos2, sin2)


def _mla_kv_kernel(z_ref, kvn_ref, w_ref, kr_ref, kt_ref, cos_ref, sin_ref, k_ref, v_ref):
    nh = MLA_HEADS
    xn = _rms(z_ref[...].astype(F32), kvn_ref[...]).astype(BF16)
    kv = _dot(xn, w_ref[...])
    r = MLA_ROPE
    kr = (kr_ref[:, 0:r] * cos_ref[:, 0:r] + kt_ref[:, 0:r] * sin_ref[:, 0:r]).astype(k_ref.dtype)
    per = MLA_NOPE + MLA_V
    ones_col = (lax.broadcasted_iota(jnp.int32, (kv.shape[0], MLA_VE - MLA_V), 1) == 0).astype(v_ref.dtype)
    for h in range(nh):
        k_ref[h, :, 0:MLA_NOPE] = kv[:, h * per:h * per + MLA_NOPE].astype(k_ref.dtype)
        k_ref[h, :, MLA_NOPE:MLA_QK] = kr
        v_ref[h, :, 0:MLA_V] = kv[:, h * per + MLA_NOPE:(h + 1) * per].astype(v_ref.dtype)
        v_ref[h, :, MLA_V:MLA_VE] = ones_col


def mla_kv(z_mla, z_misc, kv_norm, w_ukv, cos2, sin2, row0, nrows):
    tm = _pick(nrows, ROW_TILE, BF16_SUBLANES)
    assert row0 % tm == 0
    r0 = row0 // tm
    return pl.pallas_call(
        _mla_kv_kernel,
        grid=(nrows // tm,),
        in_specs=[
            pl.BlockSpec((tm, KV_LORA), lambda i: (i + r0, 1)),
            pl.BlockSpec((1, KV_LORA), lambda i: (0, 0)),
            pl.BlockSpec(w_ukv.shape, lambda i: (0, 0)),
            pl.BlockSpec((tm, LANES), lambda i: (i + r0, 0)),
            pl.BlockSpec((tm, LANES), lambda i: (i + r0, 1)),
            pl.BlockSpec((tm, LANES), lambda i: (i + r0, 0)),
            pl.BlockSpec((tm, LANES), lambda i: (i + r0, 0)),
        ],
        out_specs=[
            pl.BlockSpec((MLA_HEADS, tm, MLA_QK), lambda i: (0, i, 0)),
            pl.BlockSpec((MLA_HEADS, tm, MLA_VE), lambda i: (0, i, 0)),
        ],
        out_shape=[jax.ShapeDtypeStruct((MLA_HEADS, nrows, MLA_QK), BF16),
                   jax.ShapeDtypeStruct((MLA_HEADS, nrows, MLA_VE), BF16)],
        compiler_params=_params(("arbitrary",)),
        name="mla_kv",
    )(z_mla, kv_norm.reshape(1, KV_LORA), w_ukv, z_misc, z_misc, cos2, sin2)


def _softmax_update(carry, q, k, v):
    m, acc = carry
    s = _dot_nt(q, k)
    m_new = jnp.maximum(m, jnp.max(s, axis=-1, keepdims=True))
    p = jnp.exp2(s - m_new)
    acc = jnp.exp2(m - m_new) * acc + _dot(p.astype(BF16), v)
    return m_new, acc


def _attn_kernel(q_ref, kc_ref, vc_ref, *rest, tk, nkl):
    if nkl:
        kl_ref, vl_ref, o_ref = rest
    else:
        (o_ref,) = rest
    q = q_ref[...]
    tq = q.shape[0]
    init = (jnp.full((tq, 1), -jnp.inf, F32), jnp.zeros((tq, MLA_VE), F32))
    carry = _softmax_update(init, q, kc_ref[...], vc_ref[...])
    if nkl:
        def body(j, c):
            off = pl.multiple_of(j * tk, tk)
            return _softmax_update(c, q, kl_ref[pl.ds(off, tk), :], vl_ref[pl.ds(off, tk), :])
        carry = lax.fori_loop(0, nkl, body, carry, unroll=True)
    _, acc = carry
    o_ref[...] = (acc[:, 0:MLA_V] / acc[:, MLA_V:MLA_V + 1]).astype(o_ref.dtype)


def attention(q, kc, vc, kl, vl, nq_per_batch, lay):
    nb, c, s = lay["B"], lay["C"], lay["S"]
    tq = _pick(nq_per_batch, ATTN_Q_TILE, BF16_SUBLANES)
    nqb = nq_per_batch // tq
    in_specs = [
        pl.BlockSpec((None, tq, MLA_QK), lambda b, h, i: (h, b * nqb + i, 0)),
        pl.BlockSpec((None, c, MLA_QK), lambda b, h, i: (h, b, 0)),
        pl.BlockSpec((None, c, MLA_VE), lambda b, h, i: (h, b, 0)),
    ]
    args = [q, kc, vc]
    tk = 0
    nkl = 0
    if kl is not None:
        tk = _pick(s, ATTN_KV_TILE, BF16_SUBLANES)
        nkl = s // tk
        in_specs += [
            pl.BlockSpec((None, s, MLA_QK), lambda b, h, i: (h, b, 0)),
            pl.BlockSpec((None, s, MLA_VE), lambda b, h, i: (h, b, 0)),
        ]
        args += [kl, vl]
    return pl.pallas_call(
        functools.partial(_attn_kernel, tk=tk, nkl=nkl),
        grid=(nb, MLA_HEADS, nqb),
        in_specs=in_specs,
        out_specs=pl.BlockSpec((tq, MLA_V), lambda b, h, i: (b * nqb + i, h)),
        out_shape=jax.ShapeDtypeStruct((nb * nq_per_batch, MLA_HEADS * MLA_V), BF16),
        compiler_params=_params(("arbitrary", "arbitrary", "arbitrary")),
        name="attention_lat" if kl is not None else "attention_ctx",
    )(*args)


def _merge1_kernel(bra_ref, of_ref, ob_ref, gate_ref, attc_ref, attl_ref, g0_ref, g1_ref, g2_ref, dnn_ref, wb_ref,
                   y_ref, brb_ref, *, nctx_blk):
    hd = DN_HEAD_DIM
    att = jnp.where(pl.program_id(1) < nctx_blk, attc_ref[...], attl_ref[...])
    o = of_ref[...] + ob_ref[...]
    for h in range(DN_HEADS):
        cs = slice(h * hd, (h + 1) * hd)
        oh = o[:, cs]
        yn = oh * lax.rsqrt(jnp.mean(oh * oh, axis=-1, keepdims=True) + EPS) * dnn_ref[:, cs]
        brb_ref[:, cs] = (yn * _silu(gate_ref[:, cs].astype(F32))).astype(brb_ref.dtype)
    y = jax.nn.sigmoid(g0_ref[...].astype(F32)) * _dot(bra_ref[...], wb_ref[0])
    y = y + jax.nn.sigmoid(g1_ref[...].astype(F32)) * _dot(brb_ref[...], wb_ref[1])
    y = y + jax.nn.sigmoid(g2_ref[...].astype(F32)) * _dot(att, wb_ref[2])
    y_ref[...] = y.astype(y_ref.dtype)


def merge_branches(br_a, o_f, o_b, z_main, attn_c, attn_l, dn_norm, w_branch, lay):
    t = br_a.shape[0]
    d = lay["D"]
    w = BRANCH_W
    tm = _pick(lay["n_ctx"], ROW_TILE, BF16_SUBLANES)
    tn = _pick(d, COL_TILE)
    nn = d // tn
    gate0 = 6 * w // tn
    nctx_blk = lay["n_ctx"] // tm

    def gspec(j):
        return pl.BlockSpec((tm, tn), lambda n, i: (i, gate0 + j * nn + n))

    row = lambda n, i: (i, 0)
    return pl.pallas_call(
        functools.partial(_merge1_kernel, nctx_blk=nctx_blk),
        grid=(nn, t // tm),
        in_specs=[
            pl.BlockSpec((tm, w), row),
            pl.BlockSpec((tm, w), row),
            pl.BlockSpec((tm, w), row),
            pl.BlockSpec((tm, w), lambda n, i: (i, 5)),
            pl.BlockSpec((tm, w), lambda n, i: (jnp.minimum(i, nctx_blk - 1), 0)),
            pl.BlockSpec((tm, w), lambda n, i: (jnp.maximum(i - nctx_blk, 0), 0)),
            gspec(0), gspec(1), gspec(2),
            pl.BlockSpec((1, w), lambda n, i: (0, 0)),
            pl.BlockSpec((3, w, tn), lambda n, i: (0, 0, n)),
        ],
        out_specs=pl.BlockSpec((tm, tn), lambda n, i: (i, n)),
        out_shape=jax.ShapeDtypeStruct((t, d), BF16),
        scratch_shapes=[pltpu.VMEM((tm, w), BF16)],
        compiler_params=_params(("arbitrary", "arbitrary")),
        name="merge_branches",
    )(br_a, o_f, o_b, z_main, attn_c, attn_l, z_main, z_main, z_main,
      jnp.tile(dn_norm.astype(F32), DN_HEADS).reshape(1, w), w_branch)


def _merge2_kernel(y_ref, wo_ref, x_ref, mod_ref, n2_ref, wrh_ref, wrl_ref, xo_ref, h2_ref, lg_ref, *, d):
    m = mod_ref[...]
    mix = _dot(y_ref[...], wo_ref[...])
    xn = x_ref[...] + _mod_part(m, 2, d) * mix
    xo_ref[...] = xn
    h2 = _rms(xn, n2_ref[...]) * (1.0 + _mod_part(m, 4, d)) + _mod_part(m, 3, d)
    h2_ref[...] = h2
    hh = h2.astype(BF16)
    hl = (h2 - hh.astype(F32)).astype(BF16)
    lg_ref[...] = _dot(hh, wrh_ref[...]) + (_dot(hl, wrh_ref[...]) + _dot(hh, wrl_ref[...]))


def out_proj_residual(y, w_out, x, mod, layer, norm2, w_router_pad, lay):
    t, d = x.shape
    tm = _pick(lay["n_ctx"], RES_ROW_TILE, BF16_SUBLANES)
    row = lambda i: (i, 0)
    const = lambda i: (0, 0)
    wr_hi = w_router_pad.astype(BF16)
    wr_lo = (w_router_pad - wr_hi.astype(F32)).astype(BF16)
    return pl.pallas_call(
        functools.partial(_merge2_kernel, d=d),
        grid=(t // tm,),
        in_specs=[
            pl.BlockSpec((tm, d), row),
            pl.BlockSpec((d, d), const),
            pl.BlockSpec((tm, d), row),
            _mod_spec(layer, tm, lay),
            pl.BlockSpec((1, d), const),
            pl.BlockSpec((d, LANES), const),
            pl.BlockSpec((d, LANES), const),
        ],
        out_specs=[pl.BlockSpec((tm, d), row), pl.BlockSpec((tm, d), row), pl.BlockSpec((tm, LANES), row)],
        out_shape=[jax.ShapeDtypeStruct((t, d), F32), jax.ShapeDtypeStruct((t, d), F32),
                   jax.ShapeDtypeStruct((t, LANES), F32)],
        compiler_params=_params(("arbitrary",)),
        name="out_proj_residual",
    )(y, w_out, x, mod, norm2.reshape(1, d), wr_hi, wr_lo)


def _topk_kernel(lg_ref, bias_ref, tri_ref, idx_ref, wt_ref, rank_ref, cnt_ref, run_ref):
    ne, ng = N_EXPERTS, N_GROUPS
    per = ne // ng
    lt = lg_ref[...].T
    tm = lt.shape[1]
    sc = jax.nn.sigmoid(lt[0:ne])
    ch = sc + bias_ref[...]
    ch3 = ch.reshape(ng, per, tm)
    neg = -jnp.inf
    sub = lax.broadcasted_iota(jnp.int32, (ng, per, tm), 1)
    m1 = jnp.max(ch3, axis=1, keepdims=True)
    i1 = jnp.min(jnp.where(ch3 == m1, sub, per), axis=1, keepdims=True)
    m2 = jnp.max(jnp.where(sub == i1, neg, ch3), axis=1, keepdims=True)
    gs = (m1 + m2).reshape(ng, tm)
    giota = lax.broadcasted_iota(jnp.int32, (ng, tm), 0)
    sel = jnp.zeros((ng, tm), F32)
    cur = gs
    for _ in range(TOPK_GROUPS):
        m = jnp.max(cur, axis=0, keepdims=True)
        ix = jnp.min(jnp.where(cur == m, giota, ng), axis=0, keepdims=True)
        hit = giota == ix
        sel = jnp.where(hit, 1.0, sel)
        cur = jnp.where(hit, neg, cur)
    masked = jnp.where(sel.reshape(ng, 1, tm) > 0, ch3, neg).reshape(ne, tm)
    eiota = lax.broadcasted_iota(jnp.int32, (ne, tm), 0)
    idxs, ws, hits = [], [], []
    for _ in range(TOP_K):
        m = jnp.max(masked, axis=0, keepdims=True)
        ix = jnp.min(jnp.where(masked == m, eiota, ne), axis=0, keepdims=True)
        hit = eiota == ix
        ws.append(jnp.sum(jnp.where(hit, sc, 0.0), axis=0, keepdims=True))
        idxs.append(ix)
        hits.append(hit)
        masked = jnp.where(hit, neg, masked)
    wall = jnp.concatenate(ws, axis=0)
    idx_ref[...] = jnp.concatenate(idxs, axis=0)
    wt_ref[...] = wall / jnp.sum(wall, axis=0, keepdims=True) * ROUTED_SCALE

    @pl.when(pl.program_id(0) == 0)
    def _():
        run_ref[...] = jnp.zeros(run_ref.shape, F32)

    chosen = hits[0].astype(F32)
    for hit in hits[1:]:
        chosen = chosen + hit.astype(F32)
    before = run_ref[:, 0:1] + _dot(chosen.astype(BF16), tri_ref[...])
    ranks = [jnp.sum(jnp.where(hit, before, 0.0), axis=0, keepdims=True) for hit in hits]
    rank_ref[...] = jnp.concatenate(ranks, axis=0).astype(jnp.int32)
    run_ref[...] = run_ref[...] + jnp.sum(chosen, axis=1, keepdims=True)
    cnt_ref[...] = run_ref[...]


def route_topk(logits, bias):
    t = logits.shape[0]
    tm = _pick(t, ROW_TILE)
    strict_upper = jnp.asarray(np.triu(np.ones((tm, tm), np.float32), 1), BF16)
    col = lambda i: (0, i)
    idx_t, wt_t, rank_t, cnt = pl.pallas_call(
        _topk_kernel,
        grid=(t // tm,),
        in_specs=[pl.BlockSpec((tm, LANES), lambda i: (i, 0)), pl.BlockSpec((N_EXPERTS, 1), lambda i: (0, 0)),
                  pl.BlockSpec((tm, tm), lambda i: (0, 0))],
        out_specs=[pl.BlockSpec((TOP_K, tm), col), pl.BlockSpec((TOP_K, tm), col), pl.BlockSpec((TOP_K, tm), col),
                   pl.BlockSpec((N_EXPERTS, LANES), lambda i: (0, 0))],
        out_shape=[jax.ShapeDtypeStruct((TOP_K, t), jnp.int32), jax.ShapeDtypeStruct((TOP_K, t), F32),
                   jax.ShapeDtypeStruct((TOP_K, t), jnp.int32), jax.ShapeDtypeStruct((N_EXPERTS, LANES), F32)],
        scratch_shapes=[pltpu.VMEM((N_EXPERTS, LANES), F32)],
        compiler_params=_params(("arbitrary",)),
        name="route_topk",
    )(logits, bias.astype(F32).reshape(N_EXPERTS, 1), strict_upper)
    return idx_t, wt_t, rank_t, cnt[:, 0].astype(jnp.int32)


def moe_layout(idx_t, rank_t, counts, bm):
    k, t = idx_t.shape
    n_blocks = -(-(t * k) // bm) + N_EXPERTS
    padded = (counts + bm - 1) // bm * bm
    pend = jnp.cumsum(padded)
    pstart = pend - padded
    experts = jnp.arange(N_EXPERTS, dtype=jnp.int32)[:, None, None]
    pos_t = rank_t + jnp.sum(jnp.where(idx_t[None] == experts, pstart[:, None, None], 0), axis=0)
    n_used = (pend[-1] // bm).astype(jnp.int32)
    blk = jnp.minimum(jnp.arange(n_blocks, dtype=jnp.int32), n_used - 1) * bm
    blk_e = jnp.minimum(jnp.sum((pend[None, :] <= blk[:, None]).astype(jnp.int32), axis=1), N_EXPERTS - 1)
    bidx = jnp.arange(n_blocks, dtype=jnp.int32)
    last_of_expert = jnp.any((pend[None, :] == (bidx[:, None] + 1) * bm) & (counts[None, :] > 0), axis=1)
    zero_flag = (last_of_expert | (bidx >= n_used)).astype(jnp.int32)
    blk_e = blk_e.astype(jnp.int32)
    fresh = jnp.concatenate([jnp.ones((1,), jnp.int32), (blk_e[1:] != blk_e[:-1]).astype(jnp.int32)])
    run_par = (jnp.cumsum(fresh) - 1) % 2
    active = jnp.where(counts > 0, jnp.arange(N_EXPERTS, dtype=jnp.int32), N_EXPERTS)
    later = lax.cummin(jnp.concatenate([active[1:], jnp.full((1,), N_EXPERTS, jnp.int32)]), reverse=True)
    next_e = jnp.where(later < N_EXPERTS, later, -1)[blk_e]
    return (pos_t.astype(jnp.int32), blk_e, n_used.reshape(1), run_par.astype(jnp.int32),
            next_e.astype(jnp.int32), zero_flag)


def _pack_pairs(x):
    n = x.shape[1] // 2
    lo = lax.bitcast_convert_type(x[:, :n].astype(BF16).astype(F32), jnp.uint32)
    hi = lax.bitcast_convert_type(x[:, n:].astype(BF16).astype(F32), jnp.uint32)
    return (lo >> 16) | (hi & jnp.uint32(0xFFFF0000))


def _unpack_pairs(w):
    lo = lax.bitcast_convert_type(w << 16, F32)
    hi = lax.bitcast_convert_type(w & jnp.uint32(0xFFFF0000), F32)
    return lo, hi


def _row_copy(src_ref, src_row, dst_ref, dst_row, sem):
    return pltpu.make_async_copy(src_ref.at[pl.ds(src_row, 1)], dst_ref.at[pl.ds(dst_row, 1)], sem)


def _dispatch_kernel(pos_ref, zf_ref, h_ref, xs_ref, buf_ref, zero_ref, sem_ref, zsem_ref, *, tm, bm, n_blocks,
                     nsteps):
    i = pl.program_id(0)
    slot = i % 2

    def wait_slot(s):
        for _ in range(TOP_K):
            pltpu.make_async_copy(buf_ref.at[s], xs_ref.at[pl.ds(0, tm)], sem_ref.at[s]).wait()

    @pl.when(i == 0)
    def _():
        zero_ref[...] = jnp.zeros(zero_ref.shape, zero_ref.dtype)

        def zero_copy(b):
            return pltpu.make_async_copy(zero_ref, xs_ref.at[pl.ds(pl.multiple_of(b * bm, bm), bm)], zsem_ref.at[0])

        def start(b, carry):
            @pl.when(zf_ref[b] != 0)
            def _():
                zero_copy(b).start()
            return carry

        def wait(b, carry):
            @pl.when(zf_ref[b] != 0)
            def _():
                zero_copy(b).wait()
            return carry

        lax.fori_loop(0, n_blocks, start, 0)
        lax.fori_loop(0, n_blocks, wait, 0)

    @pl.when(i >= 2)
    def _():
        wait_slot(slot)

    buf_ref[slot] = _pack_pairs(h_ref[...])

    def body(tok, carry):
        for k in range(TOP_K):
            _row_copy(buf_ref.at[slot], tok, xs_ref, pos_ref[k, tok], sem_ref.at[slot]).start()
        return carry

    lax.fori_loop(0, tm, body, 0)

    @pl.when(i == nsteps - 1)
    def _():
        if nsteps > 1:
            wait_slot(1 - slot)
        wait_slot(slot)


def moe_dispatch(h2, pos_t, zero_flag, bm):
    t, d = h2.shape
    tm = _pick(t, MOE_ROW_TILE)
    nsteps = t // tm
    n_blocks = zero_flag.shape[0]
    return pl.pallas_call(
        functools.partial(_dispatch_kernel, tm=tm, bm=bm, n_blocks=n_blocks, nsteps=nsteps),
        grid=(nsteps,),
        in_specs=[
            pl.BlockSpec((TOP_K, tm), lambda i: (0, i), memory_space=pltpu.SMEM),
            pl.BlockSpec(memory_space=pltpu.SMEM),
            pl.BlockSpec((tm, d), lambda i: (i, 0)),
        ],
        out_specs=pl.BlockSpec(memory_space=pl.ANY),
        out_shape=jax.ShapeDtypeStruct((n_blocks * bm, d // 2), jnp.uint32),
        scratch_shapes=[pltpu.VMEM((2, tm, d // 2), jnp.uint32), pltpu.VMEM((bm, d // 2), jnp.uint32),
                        pltpu.SemaphoreType.DMA((2,)), pltpu.SemaphoreType.DMA((1,))],
        compiler_params=_params(("arbitrary",)),
        name="moe_dispatch",
    )(pos_t, zero_flag, h2)


def _moe_kernel(be_ref, nu_ref, par_ref, nx_ref, x_ref, w1_hbm, w3_hbm, w2_hbm, y_ref, f1_s, f3_s, f2_s, w13_s, w2_s,
                sem_ref, *, ed, half, layer):
    i = pl.program_id(0)
    prev = be_ref[jnp.maximum(i - 1, 0)]
    fresh = jnp.logical_or(i == 0, be_ref[i] != prev)
    slot = par_ref[i]

    def weight_copies(e, s):
        return (pltpu.make_async_copy(w1_hbm.at[layer, e], f1_s.at[s], sem_ref.at[s, 0]),
                pltpu.make_async_copy(w3_hbm.at[layer, e], f3_s.at[s], sem_ref.at[s, 1]),
                pltpu.make_async_copy(w2_hbm.at[layer, e], f2_s.at[s], sem_ref.at[s, 2]))

    @pl.when(i == 0)
    def _():
        for cp in weight_copies(be_ref[0], slot):
            cp.start()

    @pl.when(fresh)
    def _():
        for cp in weight_copies(be_ref[i], slot):
            cp.wait()

        @pl.when(nx_ref[i] >= 0)
        def _():
            for cp in weight_copies(nx_ref[i], 1 - slot):
                cp.start()

        w13_s[:, 0:ed] = f1_s[slot].astype(BF16)
        w13_s[:, ed:2 * ed] = f3_s[slot].astype(BF16)
        w2_s[...] = f2_s[slot].astype(BF16)

    @pl.when(i < nu_ref[0])
    def _():
        lo, hi = _unpack_pairs(x_ref[...])
        a = _dot(lo.astype(BF16), w13_s[0:half, :]) + _dot(hi.astype(BF16), w13_s[half:2 * half, :])
        act = (_silu(a[:, 0:ed]) * a[:, ed:2 * ed]).astype(BF16)
        y_ref[...] = _pack_pairs(_dot(act, w2_s[...]))

    @pl.when(i >= nu_ref[0])
    def _():
        y_ref[...] = jnp.zeros(y_ref.shape, y_ref.dtype)


def moe_experts(xs, blk_e, n_used, run_par, next_e, w1, w3, w2, layer, bm):
    n_rows, half = xs.shape
    d = 2 * half
    ed = w1.shape[-1]
    n_blocks = n_rows // bm
    any_spec = pl.BlockSpec(memory_space=pl.ANY)
    grid_spec = pltpu.PrefetchScalarGridSpec(
        num_scalar_prefetch=4,
        grid=(n_blocks,),
        in_specs=[pl.BlockSpec((bm, half), lambda i, *_: (i, 0)), any_spec, any_spec, any_spec],
        out_specs=pl.BlockSpec((bm, half), lambda i, *_: (i, 0)),
        scratch_shapes=[pltpu.VMEM((2, d, ed), F32), pltpu.VMEM((2, d, ed), F32), pltpu.VMEM((2, ed, d), F32),
                        pltpu.VMEM((d, 2 * ed), BF16), pltpu.VMEM((ed, d), BF16),
                        pltpu.SemaphoreType.DMA((2, 3))],
    )
    return pl.pallas_call(
        functools.partial(_moe_kernel, ed=ed, half=half, layer=layer),
        grid_spec=grid_spec,
        out_shape=jax.ShapeDtypeStruct(xs.shape, jnp.uint32),
        compiler_params=_params(("arbitrary",)),
        name="moe_experts",
    )(blk_e, n_used, run_par, next_e, xs, w1, w3, w2)


def _shared_kernel(pos_ref, posn_ref, wt_ref, h2_ref, w13_ref, w2_ref, ys_ref, x_ref, mod_ref, ng_ref, nmod_ref,
                   xo_ref, ho_ref, ybuf_ref, sem_ref, *, d, ed, tm, nsteps, final):
    i = pl.program_id(0)
    slot = i % 2

    def issue(p_ref, s):
        def body(tok, carry):
            for k in range(TOP_K):
                _row_copy(ys_ref, p_ref[k, tok], ybuf_ref.at[s, k], tok, sem_ref.at[s]).start()
            return carry
        lax.fori_loop(0, tm, body, 0)

    @pl.when(i == 0)
    def _():
        issue(pos_ref, 0)

    @pl.when(i + 1 < nsteps)
    def _():
        issue(posn_ref, 1 - slot)

    m = mod_ref[...]
    a = _dot(h2_ref[...].astype(BF16), w13_ref[...])
    act = (_silu(a[:, 0:ed]) * a[:, ed:2 * ed]).astype(BF16)
    f = _dot(act, w2_ref[...])

    for k in range(TOP_K):
        pltpu.make_async_copy(ys_ref.at[pl.ds(0, tm)], ybuf_ref.at[slot, k], sem_ref.at[slot]).wait()
    half = d // 2
    wks = [wt_ref[:, k:k + 1] for k in range(TOP_K)]
    r_lo, r_hi = [], []
    for c in range(half // LANES):
        cs = slice(c * LANES, (c + 1) * LANES)
        acc_lo = acc_hi = None
        for k in range(TOP_K):
            lo, hi = _unpack_pairs(ybuf_ref[slot, k, :, cs])
            acc_lo = wks[k] * lo if acc_lo is None else acc_lo + wks[k] * lo
            acc_hi = wks[k] * hi if acc_hi is None else acc_hi + wks[k] * hi
        r_lo.append(acc_lo)
        r_hi.append(acc_hi)
    f = f + jnp.concatenate(r_lo + r_hi, axis=1)
    xn = x_ref[...] + _mod_part(m, 5, d) * f
    xo_ref[...] = xn
    y = _rms(xn, ng_ref[...])
    if not final:
        nm = nmod_ref[...]
        y = y * (1.0 + _mod_part(nm, 1, d)) + _mod_part(nm, 0, d)
    ho_ref[...] = y.astype(ho_ref.dtype)


def shared_residual(h2, sw13, sw2, ys, pos_t, wts, x, mod, layer, next_g, final, lay):
    t, d = x.shape
    ed = sw2.shape[0]
    tm = _pick(lay["n_ctx"], RES_ROW_TILE, BF16_SUBLANES)
    nsteps = t // tm
    row = lambda i: (i, 0)
    const = lambda i: (0, 0)
    next_layer = layer if final else layer + 1
    return pl.pallas_call(
        functools.partial(_shared_kernel, d=d, ed=ed, tm=tm, nsteps=nsteps, final=final),
        grid=(nsteps,),
        in_specs=[
            pl.BlockSpec((TOP_K, tm), lambda i: (0, i), memory_space=pltpu.SMEM),
            pl.BlockSpec((TOP_K, tm), lambda i: (0, jnp.minimum(i + 1, nsteps - 1)), memory_space=pltpu.SMEM),
            pl.BlockSpec((tm, TOP_K), row),
            pl.BlockSpec((tm, d), row),
            pl.BlockSpec((d, 2 * ed), const),
            pl.BlockSpec((ed, d), const),
            pl.BlockSpec(memory_space=pl.ANY),
            pl.BlockSpec((tm, d), row),
            _mod_spec(layer, tm, lay),
            pl.BlockSpec((1, d), const),
            _mod_spec(next_layer, tm, lay),
        ],
        out_specs=[pl.BlockSpec((tm, d), row), pl.BlockSpec((tm, d), row)],
        out_shape=[jax.ShapeDtypeStruct((t, d), F32), jax.ShapeDtypeStruct((t, d), F32 if final else BF16)],
        scratch_shapes=[pltpu.VMEM((2, TOP_K, tm, d // 2), jnp.uint32), pltpu.SemaphoreType.DMA((2,))],
        compiler_params=_params(("arbitrary",)),
        name="shared_residual",
    )(pos_t, pos_t, wts, h2, sw13, sw2, ys, x, mod, next_g.reshape(1, d), mod)


def _rope_tables(lay):
    half = MLA_ROPE // 4
    s = lay["S"]
    freq = ROPE_BASE ** (-np.arange(half, dtype=np.float64) / half)
    tpos = np.arange(s)
    ang_r = (tpos // GRID_W)[:, None] * freq
    ang_c = (tpos % GRID_W)[:, None] * freq
    ang = np.concatenate([ang_r, ang_r, ang_c, ang_c], axis=1)
    cos_l = np.tile(np.cos(ang), (lay["B"], 2))
    sin_l = np.tile(np.sin(ang), (lay["B"], 2))
    cos = np.concatenate([np.ones((lay["n_ctx"], 2 * MLA_ROPE)), cos_l], axis=0).astype(np.float32)
    sin = np.concatenate([np.zeros((lay["n_ctx"], 2 * MLA_ROPE)), sin_l], axis=0).astype(np.float32)
    return jnp.asarray(cos), jnp.asarray(sin)


def _rot_cols(w):
    q = MLA_ROPE // 4
    a, b, c, e = w[..., 0:q], w[..., q:2 * q], w[..., 2 * q:3 * q], w[..., 3 * q:4 * q]
    return jnp.concatenate([-b, a, -e, c], axis=-1)


def _split_w_in_kernel(x_ref, main_ref, mla_ref, small_ref, *, o_small, o_cq, o_kr, o_gate, n_gate):
    main_ref[:, 0:o_small] = x_ref[:, 0:o_small].astype(main_ref.dtype)
    main_ref[:, o_small:o_small + n_gate] = x_ref[:, o_gate:o_gate + n_gate].astype(main_ref.dtype)
    mla_ref[...] = x_ref[:, o_cq:o_kr].astype(mla_ref.dtype)
    n_ab = o_cq - o_small
    small_ref[...] = jnp.zeros(small_ref.shape, small_ref.dtype)
    small_ref[:, 0:n_ab] = x_ref[:, o_small:o_cq]
    small_ref[:, n_ab:n_ab + MLA_ROPE] = x_ref[:, o_kr:o_gate]


def split_w_in(w_in_all, layer, offs):
    o_small, o_cq, o_kr, o_gate = offs
    _, d, d_in = w_in_all.shape
    n_gate = d_in - o_gate
    tm = _pick(d, LANES, SUBLANES)
    row = lambda i: (i, 0)
    return pl.pallas_call(
        functools.partial(_split_w_in_kernel, o_small=o_small, o_cq=o_cq, o_kr=o_kr, o_gate=o_gate, n_gate=n_gate),
        grid=(d // tm,),
        in_specs=[pl.BlockSpec((None, tm, d_in), lambda i: (layer, i, 0))],
        out_specs=[pl.BlockSpec((tm, o_small + n_gate), row), pl.BlockSpec((tm, o_kr - o_cq), row),
                   pl.BlockSpec((tm, LANES), row)],
        out_shape=[jax.ShapeDtypeStruct((d, o_small + n_gate), BF16), jax.ShapeDtypeStruct((d, o_kr - o_cq), BF16),
                   jax.ShapeDtypeStruct((d, LANES), F32)],
        compiler_params=_params(("arbitrary",)),
        name="split_w_in",
    )(w_in_all)


def _prep_layer_weights(w_in_all, layer, w_uq, w_ukv, w_branch, w_out, w_router, sw1, sw3, sw2, d):
    w = BRANCH_W
    o_small = 6 * w
    o_cq = o_small + 4 * DN_HEADS
    o_kr = o_cq + Q_LORA + KV_LORA
    o_gate = o_kr + MLA_ROPE
    w_main, w_mla, w_small = split_w_in(w_in_all, layer, (o_small, o_cq, o_kr, o_gate))
    n_ab = o_cq - o_small
    w_kr = w_small[:, n_ab:n_ab + MLA_ROPE]
    zpad = jnp.zeros((d, LANES - MLA_ROPE), F32)
    w_misc = jnp.concatenate(
        [w_kr, zpad, _rot_cols(w_kr), zpad, w_small[:, 0:n_ab], jnp.zeros((d, LANES - n_ab), F32)],
        axis=1).astype(BF16)
    uq = w_uq.reshape(Q_LORA, MLA_HEADS, MLA_QK)
    uq_r = uq[..., MLA_NOPE:]
    wq_ext = jnp.concatenate(
        [uq[..., :MLA_NOPE].reshape(Q_LORA, -1), uq_r.reshape(Q_LORA, -1), _rot_cols(uq_r).reshape(Q_LORA, -1)],
        axis=1).astype(BF16)
    wr_pad = jnp.zeros((d, LANES), F32).at[:, :N_EXPERTS].set(w_router.astype(F32))
    return dict(w_main=w_main, w_mla=w_mla, w_misc=w_misc, wq_ext=wq_ext, w_ukv=w_ukv.astype(BF16),
                w_branch=w_branch.astype(BF16), w_out=w_out.astype(BF16), wr_pad=wr_pad,
                sw13=jnp.concatenate([sw1, sw3], axis=1).astype(BF16), sw2=sw2.astype(BF16))


def kernel(x, c, ctx, c_ctx, w_mod, b_mod, norm1, norm2, w_in, gm_norm, gm_ws, gm_bs, dn_conv, dn_a_log,
           dn_dt_bias, dn_norm, mla_q_norm, mla_kv_norm, mla_w_uq, mla_w_ukv, w_branch, w_out, moe_router,
           moe_bias, moe_w1, moe_w3, moe_w2, shared_w1, shared_w3, shared_w2, final_norm):
    nb, s, d = x.shape
    cl = ctx.shape[1]
    depth = w_mod.shape[0]
    n_ctx = nb * cl
    t = n_ctx + nb * s
    lay = dict(B=nb, S=s, C=cl, D=d, n_ctx=n_ctx, T=t)
    assert nb + 1 <= SUBLANES and s % GRID_W == 0

    xs = jnp.concatenate([ctx.reshape(n_ctx, d), x.reshape(nb * s, d)], axis=0).astype(F32)
    cvec = jnp.zeros((SUBLANES, d), F32).at[0].set(c_ctx.astype(F32)).at[1:1 + nb].set(c.astype(F32))
    mod = modulation(cvec, w_mod, b_mod)
    cos2, sin2 = _rope_tables(lay)
    scale = MLA_QK ** -0.5 * LOG2E

    h = prenorm(xs, norm1[0], mod, 0, lay)
    out = None
    for i in range(depth):
        last = i == depth - 1
        wts = _prep_layer_weights(w_in, i, mla_w_uq[i], mla_w_ukv[i], w_branch[i], w_out[i], moe_router[i],
                                  shared_w1[i], shared_w3[i], shared_w2[i], d)
        z_main = matmul(h, wts["w_main"], BF16, "in_proj_main")
        z_mla = matmul(h, wts["w_mla"], BF16, "in_proj_mla")
        z_misc = matmul(h, wts["w_misc"], F32, "in_proj_misc")

        br_a = gmlp(z_main, gm_norm[i], gm_ws[i], gm_bs[i], lay)

        qkv_c = dn_short_conv(z_main, dn_conv[i], lay)
        o_f, o_b = dn_scan(qkv_c, z_misc, dn_a_log[i], dn_dt_bias[i], lay)

        kc, vc = mla_kv(z_mla, z_misc, mla_kv_norm[i], wts["w_ukv"], cos2, sin2, 0, n_ctx)
        kl, vl = mla_kv(z_mla, z_misc, mla_kv_norm[i], wts["w_ukv"], cos2, sin2, n_ctx, nb * s)
        ql = mla_q(z_mla, mla_q_norm[i], wts["wq_ext"], cos2, sin2, n_ctx, nb * s, scale)
        attn_l = attention(ql, kc, vc, kl, vl, s, lay)
        qc = mla_q(z_mla, mla_q_norm[i], wts["wq_ext"], cos2, sin2, 0, n_ctx, scale)
        attn_c = attention(qc, kc, vc, None, None, cl, lay)

        y = merge_branches(br_a, o_f, o_b, z_main, attn_c, attn_l, dn_norm[i], wts["w_branch"], lay)
        xs, h2, logits = out_proj_residual(y, wts["w_out"], xs, mod, i, norm2[i], wts["wr_pad"], lay)

        idx_t, wt_t, rank_t, counts = route_topk(logits, moe_bias[i])
        bm = MOE_ROW_TILE
        pos_t, blk_e, n_used, run_par, next_e, zero_flag = moe_layout(idx_t, rank_t, counts, bm)
        x_sorted = moe_dispatch(h2, pos_t, zero_flag, bm)
        y_sorted = moe_experts(x_sorted, blk_e, n_used, run_par, next_e, moe_w1, moe_w3, moe_w2, i, bm)

        next_g = final_norm if last else norm1[i + 1]
        xs, h = shared_residual(h2, wts["sw13"], wts["sw2"], y_sorted, pos_t, wt_t.T, xs, mod, i, next_g, last, lay)
        out = h
    return out[n_ctx:].reshape(nb, s, d).astype(x.dtype)
```

```python
import functools
import math

import numpy as np
import jax
import jax.numpy as jnp
from jax import lax
from jax.experimental import pallas as pl
from jax.experimental.pallas import tpu as pltpu

F32 = jnp.float32
BF16 = jnp.bfloat16

GRID_W = 64
EPS = 1e-6
GM_CHUNK = 128
GM_GROUPS = 8
DN_HEADS = 8
DN_HEAD_DIM = 128
DN_CHUNK = 64
DN_CONV = 5
MLA_HEADS = 8
MLA_NOPE = 128
MLA_ROPE = 64
MLA_V = 128
MLA_QK = MLA_NOPE + MLA_ROPE
MLA_VE = 2 * MLA_V
Q_LORA = 512
KV_LORA = 512
ROPE_BASE = 10000.0
BRANCH_W = 1024
N_EXPERTS = 64
TOP_K = 8
N_GROUPS = 8
TOPK_GROUPS = 4
ROUTED_SCALE = 2.5
LOG2E = 1.4426950408889634

LANES = 128
SUBLANES = 8
BF16_SUBLANES = 16
VMEM_LIMIT_MB = 56

ROW_TILE = 512
RES_ROW_TILE = 256
CONV_ROW_TILE = 256
COL_TILE = 1024
MOE_ROW_TILE = 256
ATTN_Q_TILE = 1024
ATTN_KV_TILE = 1024


def _pick(n, pref, mult=LANES):
    if n <= pref:
        return n
    for t in range(pref - pref % mult, 0, -mult):
        if n % t == 0:
            return t
    return n


def _params(sem, mb=VMEM_LIMIT_MB):
    return pltpu.CompilerParams(dimension_semantics=sem, vmem_limit_bytes=mb * 1024 * 1024)


def _silu(x):
    return x * jax.nn.sigmoid(x)


def _gelu(x):
    return 0.5 * x * (1.0 + jnp.tanh(0.7978845608028654 * (x + 0.044715 * x * x * x)))


def _dot(a, b):
    return jnp.dot(a, b, preferred_element_type=F32)


def _dot_nt(a, b):
    return lax.dot_general(a, b, (((1,), (1,)), ((), ())), preferred_element_type=F32)


def _dot_tn(a, b):
    return lax.dot_general(a, b, (((0,), (0,)), ((), ())), preferred_element_type=F32)


def _mod_kernel(c_ref, w_ref, b_ref, o_ref):
    s = _silu(c_ref[...])
    o_ref[...] = _dot(s.astype(BF16), w_ref[...].astype(BF16)) + b_ref[...]


def modulation(cvec, w_mod, b_mod):
    nl, d, n6 = w_mod.shape
    tn = _pick(n6, COL_TILE)
    out = pl.pallas_call(
        _mod_kernel,
        grid=(nl, n6 // tn),
        in_specs=[
            pl.BlockSpec((SUBLANES, d), lambda l, n: (0, 0)),
            pl.BlockSpec((None, d, tn), lambda l, n: (l, 0, n)),
            pl.BlockSpec((None, 1, tn), lambda l, n: (l, 0, n)),
        ],
        out_specs=pl.BlockSpec((None, SUBLANES, tn), lambda l, n: (l, 0, n)),
        out_shape=jax.ShapeDtypeStruct((nl, SUBLANES, n6), F32),
        compiler_params=_params(("arbitrary", "arbitrary")),
        name="modulation",
    )(cvec, w_mod, b_mod.reshape(nl, 1, n6))
    return out.reshape(nl, SUBLANES, 1, n6)


def _seg_of_block(i, tm, lay):
    nctx_blk = lay["n_ctx"] // tm
    lat_bps = lay["S"] // tm
    return jnp.where(i < nctx_blk, 0, 1 + (i - nctx_blk) // lat_bps)


def _mod_spec(layer, tm, lay, row0_blk=0):
    n6 = 6 * lay["D"]
    return pl.BlockSpec((None, None, 1, n6), lambda i: (layer, _seg_of_block(i + row0_blk, tm, lay), 0, 0))


def _mod_part(m, k, d):
    return m[:, k * d:(k + 1) * d]


def _rms(x, g):
    return x * lax.rsqrt(jnp.mean(x * x, axis=-1, keepdims=True) + EPS) * g


def _prenorm_kernel(x_ref, g_ref, mod_ref, o_ref, *, d):
    m = mod_ref[...]
    y = _rms(x_ref[...], g_ref[...])
    o_ref[...] = (y * (1.0 + _mod_part(m, 1, d)) + _mod_part(m, 0, d)).astype(o_ref.dtype)


def prenorm(x, g, mod, layer, lay):
    t, d = x.shape
    tm = _pick(lay["n_ctx"], ROW_TILE, SUBLANES)
    return pl.pallas_call(
        functools.partial(_prenorm_kernel, d=d),
        grid=(t // tm,),
        in_specs=[
            pl.BlockSpec((tm, d), lambda i: (i, 0)),
            pl.BlockSpec((1, d), lambda i: (0, 0)),
            _mod_spec(layer, tm, lay),
        ],
        out_specs=pl.BlockSpec((tm, d), lambda i: (i, 0)),
        out_shape=jax.ShapeDtypeStruct((t, d), BF16),
        compiler_params=_params(("arbitrary",)),
        name="prenorm",
    )(x, g.reshape(1, d), mod)


def _mm_kernel(x_ref, w_ref, o_ref):
    o_ref[...] = _dot(x_ref[...], w_ref[...]).astype(o_ref.dtype)


def matmul(x, w, out_dtype, name):
    t, k = x.shape
    n = w.shape[1]
    tm = _pick(t, ROW_TILE, BF16_SUBLANES)
    tn = _pick(n, COL_TILE)
    return pl.pallas_call(
        _mm_kernel,
        grid=(n // tn, t // tm),
        in_specs=[
            pl.BlockSpec((tm, k), lambda j, i: (i, 0)),
            pl.BlockSpec((k, tn), lambda j, i: (0, j)),
        ],
        out_specs=pl.BlockSpec((tm, tn), lambda j, i: (i, j)),
        out_shape=jax.ShapeDtypeStruct((t, n), out_dtype),
        compiler_params=_params(("arbitrary", "arbitrary")),
        name=name,
    )(x, w)


def _gmlp_kernel(u_ref, v_ref, gn_ref, ws_ref, bias_ref, o_ref, *, nchunk):
    for j in range(nchunk):
        rs = slice(j * GM_CHUNK, (j + 1) * GM_CHUNK)
        v = _gelu(v_ref[rs, :].astype(F32))
        vb = _rms(v, gn_ref[...]).astype(BF16)
        u = _gelu(u_ref[rs, :].astype(F32))
        for g in range(GM_GROUPS):
            cs = slice(g * LANES, (g + 1) * LANES)
            mixed = _dot(ws_ref[g], vb[:, cs]) + bias_ref[:, cs]
            o_ref[rs, cs] = (u[:, cs] * mixed).astype(o_ref.dtype)


def gmlp(z_main, gm_norm, gm_ws, gm_bs, lay):
    t = z_main.shape[0]
    w = BRANCH_W
    tm = _pick(lay["n_ctx"], ROW_TILE, GM_CHUNK)
    bias = jnp.repeat(gm_bs.T.astype(F32), w // GM_GROUPS, axis=1)
    return pl.pallas_call(
        functools.partial(_gmlp_kernel, nchunk=tm // GM_CHUNK),
        grid=(t // tm,),
        in_specs=[
            pl.BlockSpec((tm, w), lambda i: (i, 0)),
            pl.BlockSpec((tm, w), lambda i: (i, 1)),
            pl.BlockSpec((1, w), lambda i: (0, 0)),
            pl.BlockSpec((GM_GROUPS, GM_CHUNK, GM_CHUNK), lambda i: (0, 0, 0)),
            pl.BlockSpec((GM_CHUNK, w), lambda i: (0, 0)),
        ],
        out_specs=pl.BlockSpec((tm, w), lambda i: (i, 0)),
        out_shape=jax.ShapeDtypeStruct((t, w), BF16),
        compiler_params=_params(("arbitrary",)),
        name="gmlp",
    )(z_main, z_main, gm_norm.reshape(1, w), gm_ws.astype(BF16), bias)


def _conv_kernel(x_ref, hp_ref, hn_ref, w_ref, o_ref, xe_ref, *, tm, nctx_blk, ctx_bps, lat_bps):
    j = pl.program_id(0)
    i = pl.program_id(1)
    li = i - nctx_blk
    seg_start = jnp.where(i < nctx_blk, (i % ctx_bps) == 0, (li % lat_bps) == 0)
    seg_end = jnp.where(i < nctx_blk, ((i + 1) % ctx_bps) == 0, ((li + 1) % lat_bps) == 0)
    halo = BF16_SUBLANES
    xe_ref[0:halo, :] = jnp.where(seg_start, 0.0, hp_ref[...].astype(F32))
    xe_ref[halo:halo + tm, :] = x_ref[...].astype(F32)
    xe_ref[halo + tm:2 * halo + tm, :] = jnp.where(seg_end, 0.0, hn_ref[...].astype(F32))
    base = halo - DN_CONV // 2
    acc = w_ref[0:1, :] * xe_ref[base:base + tm, :]
    for tap in range(1, DN_CONV):
        acc = acc + w_ref[tap:tap + 1, :] * xe_ref[base + tap:base + tap + tm, :]
    y = _silu(acc)
    unit = j < 2
    for h in range(DN_HEADS):
        cs = slice(h * DN_HEAD_DIM, (h + 1) * DN_HEAD_DIM)
        yh = y[:, cs]
        nrm = yh * lax.rsqrt(jnp.sum(yh * yh, axis=-1, keepdims=True) + EPS)
        o_ref[:, cs] = jnp.where(unit, nrm, yh).astype(o_ref.dtype)


def dn_short_conv(z_main, conv_w, lay):
    t = z_main.shape[0]
    w = BRANCH_W
    tm = _pick(lay["C"], CONV_ROW_TILE, BF16_SUBLANES)
    halo = BF16_SUBLANES
    hb = tm // halo
    nhalo = t // halo
    wpad = jnp.zeros((SUBLANES, 3 * w), F32).at[:DN_CONV].set(conv_w.astype(F32))
    kern = functools.partial(_conv_kernel, tm=tm, nctx_blk=lay["n_ctx"] // tm, ctx_bps=lay["C"] // tm,
                             lat_bps=lay["S"] // tm)
    return pl.pallas_call(
        kern,
        grid=(3, t // tm),
        in_specs=[
            pl.BlockSpec((tm, w), lambda j, i: (i, 2 + j)),
            pl.BlockSpec((halo, w), lambda j, i: (jnp.maximum(i * hb - 1, 0), 2 + j)),
            pl.BlockSpec((halo, w), lambda j, i: (jnp.minimum((i + 1) * hb, nhalo - 1), 2 + j)),
            pl.BlockSpec((SUBLANES, w), lambda j, i: (0, j)),
        ],
        out_specs=pl.BlockSpec((tm, w), lambda j, i: (i, j)),
        out_shape=jax.ShapeDtypeStruct((t, 3 * w), BF16),
        scratch_shapes=[pltpu.VMEM((tm + 2 * halo, w), F32)],
        compiler_params=_params(("arbitrary", "arbitrary")),
        name="dn_conv",
    )(z_main, z_main, z_main, wpad)


def _dn_masks():
    c = DN_CHUNK
    r = np.arange(c)[:, None]
    s = np.arange(c)[None, :]
    tri = np.stack([r >= s, r <= s]).astype(np.float32)
    strict = np.stack([r > s, r < s]).astype(np.float32)
    lv = []
    b = 1
    while b < c:
        same = (r // (2 * b)) == (s // (2 * b))
        lo = same & ((r // b) % 2 == 1) & ((s // b) % 2 == 0)
        up = same & ((r // b) % 2 == 0) & ((s // b) % 2 == 1)
        lv.append(np.stack([lo, up]))
        b *= 2
    lvl = np.stack(lv, axis=1).astype(np.float32)
    return tri, strict, lvl, np.eye(c, dtype=np.float32)


def _dn_kernel(qf_ref, qb_ref, abf_ref, abb_ref, alog_ref, dt_ref, tri_ref, strict_ref, lvl_ref, eye_ref,
               of_ref, ob_ref, s_ref, *, nlevels):
    step = pl.program_id(1)

    @pl.when(step == 0)
    def _():
        s_ref[...] = jnp.zeros(s_ref.shape, F32)

    hd = DN_HEAD_DIM
    w = DN_HEADS * hd
    nh = DN_HEADS
    cc = DN_CHUNK
    eye = eye_ref[...]
    inst = []
    for d, (x_ref, ab_ref, o_ref) in enumerate(((qf_ref, abf_ref, of_ref), (qb_ref, abb_ref, ob_ref))):
        ab = ab_ref[...]
        g_all = -jnp.exp(alog_ref[...]) * jax.nn.softplus(ab + dt_ref[...])
        beta_all = jax.nn.sigmoid(ab)
        gam_c = jnp.dot(tri_ref[d], g_all, precision=lax.Precision.HIGHEST, preferred_element_type=F32)
        gam_r = gam_c.T
        for h in range(nh):
            lane = d * nh + h
            inst.append(dict(
                d=d, lane=lane, h=h, slot=lane, o_ref=o_ref,
                q=x_ref[:, h * hd:(h + 1) * hd].astype(F32),
                k=x_ref[:, w + h * hd:w + (h + 1) * hd].astype(F32),
                v=x_ref[:, 2 * w + h * hd:2 * w + (h + 1) * hd].astype(F32),
                gc=gam_c[:, lane:lane + 1],
                gr=gam_r[lane:lane + 1, :],
                bc=beta_all[:, 2 * nh + lane:2 * nh + lane + 1],
                st=s_ref[lane]))
    for it in inst:
        tri = tri_ref[it["d"]]
        diff = it["gc"] - it["gr"]
        it["dec"] = jnp.where(tri > 0, jnp.exp(jnp.where(tri > 0, diff, 0.0)), 0.0)
        it["kb"] = it["k"] * it["bc"]
        it["qs"] = it["q"] * (hd ** -0.5)
    for it in inst:
        kq = _dot_nt(jnp.concatenate([it["kb"], it["qs"]], axis=0).astype(BF16), it["k"].astype(BF16))
        it["a"] = kq[:cc] * it["dec"] * strict_ref[it["d"]]
        it["attn"] = (kq[cc:] * it["dec"]).astype(BF16)
    for it in inst:
        it["x"] = eye - it["a"] * lvl_ref[it["d"], 0]
    for lv in range(1, nlevels):
        for it in inst:
            it["xb"] = it["x"].astype(BF16)
            it["p"] = _dot(it["xb"], (it["a"] * lvl_ref[it["d"], lv]).astype(BF16)).astype(BF16)
        for it in inst:
            it["x"] = it["x"] - _dot(it["p"], it["xb"])
    for it in inst:
        eg = jnp.exp(it["gc"])
        rhs = jnp.concatenate([it["kb"] * eg, it["v"] * it["bc"]], axis=1).astype(BF16)
        it["sol"] = _dot(it["x"].astype(BF16), rhs)
        last = cc - 1 if it["d"] == 0 else 0
        g_last = it["gc"][last:last + 1, :]
        it["gtot"] = jnp.exp(g_last)
        it["kd"] = (it["k"] * jnp.exp(g_last - it["gc"])).astype(BF16)
        it["qd"] = it["qs"] * eg
    for it in inst:
        wq = jnp.concatenate([it["sol"][:, :hd], it["qd"]], axis=0).astype(BF16)
        it["r"] = _dot(wq, it["st"].astype(BF16))
    for it in inst:
        it["vn"] = (it["sol"][:, hd:] - it["r"][:cc]).astype(BF16)
    for it in inst:
        it["o"] = it["r"][cc:] + _dot(it["attn"], it["vn"])
        it["sn"] = it["st"] * it["gtot"] + _dot_tn(it["kd"], it["vn"])
    for it in inst:
        it["o_ref"][:, it["h"] * hd:(it["h"] + 1) * hd] = it["o"]
        s_ref[it["slot"]] = it["sn"]


def dn_scan(qkv_c, z_misc, a_log, dt_bias, lay):
    t = qkv_c.shape[0]
    w = BRANCH_W
    c = DN_CHUNK
    nb = lay["B"]
    ncc = lay["C"] // c
    ncl = lay["S"] // c
    nsteps = ncc + ncl

    def fidx(b, s):
        return jnp.where(s < ncc, b * ncc + s, nb * ncc + b * ncl + (s - ncc))

    def bidx(b, s):
        return jnp.where(s < ncc, b * ncc + (ncc - 1 - s), nb * ncc + b * ncl + (ncl - 1 - (s - ncc)))

    tri, strict, lvl, eye = _dn_masks()
    nlevels = lvl.shape[1]
    nh = DN_HEADS
    alog_row = jnp.zeros((1, LANES), F32).at[0, :2 * nh].set(a_log.reshape(-1).astype(F32))
    dt_row = jnp.zeros((1, LANES), F32).at[0, :2 * nh].set(dt_bias.reshape(-1).astype(F32))
    const2 = lambda b, s: (0, 0)
    const3 = lambda b, s: (0, 0, 0)
    const4 = lambda b, s: (0, 0, 0, 0)
    return pl.pallas_call(
        functools.partial(_dn_kernel, nlevels=nlevels),
        grid=(nb, nsteps),
        in_specs=[
            pl.BlockSpec((c, 3 * w), lambda b, s: (fidx(b, s), 0)),
            pl.BlockSpec((c, 3 * w), lambda b, s: (bidx(b, s), 0)),
            pl.BlockSpec((c, LANES), lambda b, s: (fidx(b, s), 2)),
            pl.BlockSpec((c, LANES), lambda b, s: (bidx(b, s), 2)),
            pl.BlockSpec((1, LANES), const2),
            pl.BlockSpec((1, LANES), const2),
            pl.BlockSpec((2, c, c), const3),
            pl.BlockSpec((2, c, c), const3),
            pl.BlockSpec((2, nlevels, c, c), const4),
            pl.BlockSpec((c, c), const2),
        ],
        out_specs=[
            pl.BlockSpec((c, w), lambda b, s: (fidx(b, s), 0)),
            pl.BlockSpec((c, w), lambda b, s: (bidx(b, s), 0)),
        ],
        out_shape=[jax.ShapeDtypeStruct((t, w), F32), jax.ShapeDtypeStruct((t, w), F32)],
        scratch_shapes=[pltpu.VMEM((2 * nh, DN_HEAD_DIM, DN_HEAD_DIM), F32)],
        compiler_params=_params(("arbitrary", "arbitrary")),
        name="dn_scan",
    )(qkv_c, qkv_c, z_misc, z_misc, alog_row, dt_row, jnp.asarray(tri), jnp.asarray(strict), jnp.asarray(lvl),
      jnp.asarray(eye))


def _mla_q_kernel(z_ref, qn_ref, w_ref, cos_ref, sin_ref, o_ref, *, scale):
    nh = MLA_HEADS
    xn = _rms(z_ref[...].astype(F32), qn_ref[...]).astype(BF16)
    na = nh * MLA_NOPE
    nr = nh * MLA_ROPE
    qa = _dot(xn, w_ref[:, 0:na])
    qr = _dot(xn, w_ref[:, na:na + nr])
    qt = _dot(xn, w_ref[:, na + nr:na + 2 * nr])
    reps = nr // LANES
    cos = jnp.concatenate([cos_ref[...]] * reps, axis=1)
    sin = jnp.concatenate([sin_ref[...]] * reps, axis=1)
    qrr = qr * cos + qt * sin
    for h in range(nh):
        o_ref[h, :, 0:MLA_NOPE] = (qa[:, h * MLA_NOPE:(h + 1) * MLA_NOPE] * scale).astype(o_ref.dtype)
        o_ref[h, :, MLA_NOPE:MLA_QK] = (qrr[:, h * MLA_ROPE:(h + 1) * MLA_ROPE] * scale).astype(o_ref.dtype)


def mla_q(z_mla, q_norm, wq_ext, cos2, sin2, row0, nrows, scale):
    tm = _pick(nrows, ROW_TILE, BF16_SUBLANES)
    assert row0 % tm == 0
    r0 = row0 // tm
    return pl.pallas_call(
        functools.partial(_mla_q_kernel, scale=scale),
        grid=(nrows // tm,),
        in_specs=[
            pl.BlockSpec((tm, Q_LORA), lambda i: (i + r0, 0)),
            pl.BlockSpec((1, Q_LORA), lambda i: (0, 0)),
            pl.BlockSpec(wq_ext.shape, lambda i: (0, 0)),
            pl.BlockSpec((tm, LANES), lambda i: (i + r0, 0)),
            pl.BlockSpec((tm, LANES), lambda i: (i + r0, 0)),
        ],
        out_specs=pl.BlockSpec((MLA_HEADS, tm, MLA_QK), lambda i: (0, i, 0)),
        out_shape=jax.ShapeDtypeStruct((MLA_HEADS, nrows, MLA_QK), BF16),
        compiler_params=_params(("arbitrary",)),
        name="mla_q",
    )(z_mla, q_norm.reshape(1, Q_LORA), wq_ext, cos2, sin2)


def _mla_kv_kernel(z_ref, kvn_ref, w_ref, kr_ref, kt_ref, cos_ref, sin_ref, k_ref, v_ref):
    nh = MLA_HEADS
    xn = _rms(z_ref[...].astype(F32), kvn_ref[...]).astype(BF16)
    kv = _dot(xn, w_ref[...])
    r = MLA_ROPE
    kr = (kr_ref[:, 0:r] * cos_ref[:, 0:r] + kt_ref[:, 0:r] * sin_ref[:, 0:r]).astype(k_ref.dtype)
    per = MLA_NOPE + MLA_V
    ones_col = (lax.broadcasted_iota(jnp.int32, (kv.shape[0], MLA_VE - MLA_V), 1) == 0).astype(v_ref.dtype)
    for h in range(nh):
        k_ref[h, :, 0:MLA_NOPE] = kv[:, h * per:h * per + MLA_NOPE].astype(k_ref.dtype)
        k_ref[h, :, MLA_NOPE:MLA_QK] = kr
        v_ref[h, :, 0:MLA_V] = kv[:, h * per + MLA_NOPE:(h + 1) * per].astype(v_ref.dtype)
        v_ref[h, :, MLA_V:MLA_VE] = ones_col


def mla_kv(z_mla, z_misc, kv_norm, w_ukv, cos2, sin2, row0, nrows):
    tm = _pick(nrows, ROW_TILE, BF16_SUBLANES)
    assert row0 % tm == 0
    r0 = row0 // tm
    return pl.pallas_call(
        _mla_kv_kernel,
        grid=(nrows // tm,),
        in_specs=[
            pl.BlockSpec((tm, KV_LORA), lambda i: (i + r0, 1)),
            pl.BlockSpec((1, KV_LORA), lambda i: (0, 0)),
            pl.BlockSpec(w_ukv.shape, lambda i: (0, 0)),
            pl.BlockSpec((tm, LANES), lambda i: (i + r0, 0)),
            pl.BlockSpec((tm, LANES), lambda i: (i + r0, 1)),
            pl.BlockSpec((tm, LANES), lambda i: (i + r0, 0)),
            pl.BlockSpec((tm, LANES), lambda i: (i + r0, 0)),
        ],
        out_specs=[
            pl.BlockSpec((MLA_HEADS, tm, MLA_QK), lambda i: (0, i, 0)),
            pl.BlockSpec((MLA_HEADS, tm, MLA_VE), lambda i: (0, i, 0)),
        ],
        out_shape=[jax.ShapeDtypeStruct((MLA_HEADS, nrows, MLA_QK), BF16),
                   jax.ShapeDtypeStruct((MLA_HEADS, nrows, MLA_VE), BF16)],
        compiler_params=_params(("arbitrary",)),
        name="mla_kv",
    )(z_mla, kv_norm.reshape(1, KV_LORA), w_ukv, z_misc, z_misc, cos2, sin2)


def _softmax_update(carry, q, k, v):
    m, acc = carry
    s = _dot_nt(q, k)
    m_new = jnp.maximum(m, jnp.max(s, axis=-1, keepdims=True))
    p = jnp.exp2(s - m_new)
    acc = jnp.exp2(m - m_new) * acc + _dot(p.astype(BF16), v)
    return m_new, acc


def _attn_kernel(q_ref, kc_ref, vc_ref, *rest, tk, nkl):
    if nkl:
        kl_ref, vl_ref, o_ref = rest
    else:
        (o_ref,) = rest
    q = q_ref[...]
    tq = q.shape[0]
    init = (jnp.full((tq, 1), -jnp.inf, F32), jnp.zeros((tq, MLA_VE), F32))
    carry = _softmax_update(init, q, kc_ref[...], vc_ref[...])
    if nkl:
        def body(j, c):
            off = pl.multiple_of(j * tk, tk)
            return _softmax_update(c, q, kl_ref[pl.ds(off, tk), :], vl_ref[pl.ds(off, tk), :])
        carry = lax.fori_loop(0, nkl, body, carry, unroll=True)
    _, acc = carry
    o_ref[...] = (acc[:, 0:MLA_V] / acc[:, MLA_V:MLA_V + 1]).astype(o_ref.dtype)


def attention(q, kc, vc, kl, vl, nq_per_batch, lay):
    nb, c, s = lay["B"], lay["C"], lay["S"]
    tq = _pick(nq_per_batch, ATTN_Q_TILE, BF16_SUBLANES)
    nqb = nq_per_batch // tq
    in_specs = [
        pl.BlockSpec((None, tq, MLA_QK), lambda b, h, i: (h, b * nqb + i, 0)),
        pl.BlockSpec((None, c, MLA_QK), lambda b, h, i: (h, b, 0)),
        pl.BlockSpec((None, c, MLA_VE), lambda b, h, i: (h, b, 0)),
    ]
    args = [q, kc, vc]
    tk = 0
    nkl = 0
    if kl is not None:
        tk = _pick(s, ATTN_KV_TILE, BF16_SUBLANES)
        nkl = s // tk
        in_specs += [
            pl.BlockSpec((None, s, MLA_QK), lambda b, h, i: (h, b, 0)),
            pl.BlockSpec((None, s, MLA_VE), lambda b, h, i: (h, b, 0)),
        ]
        args += [kl, vl]
    return pl.pallas_call(
        functools.partial(_attn_kernel, tk=tk, nkl=nkl),
        grid=(nb, MLA_HEADS, nqb),
        in_specs=in_specs,
        out_specs=pl.BlockSpec((tq, MLA_V), lambda b, h, i: (b * nqb + i, h)),
        out_shape=jax.ShapeDtypeStruct((nb * nq_per_batch, MLA_HEADS * MLA_V), BF16),
        compiler_params=_params(("arbitrary", "arbitrary", "arbitrary")),
        name="attention_lat" if kl is not None else "attention_ctx",
    )(*args)


def _merge1_kernel(bra_ref, of_ref, ob_ref, gate_ref, attc_ref, attl_ref, g0_ref, g1_ref, g2_ref, dnn_ref, wb_ref,
                   y_ref, brb_ref, *, nctx_blk):
    hd = DN_HEAD_DIM
    att = jnp.where(pl.program_id(1) < nctx_blk, attc_ref[...], attl_ref[...])
    o = of_ref[...] + ob_ref[...]
    for h in range(DN_HEADS):
        cs = slice(h * hd, (h + 1) * hd)
        oh = o[:, cs]
        yn = oh * lax.rsqrt(jnp.mean(oh * oh, axis=-1, keepdims=True) + EPS) * dnn_ref[:, cs]
        brb_ref[:, cs] = (yn * _silu(gate_ref[:, cs].astype(F32))).astype(brb_ref.dtype)
    y = jax.nn.sigmoid(g0_ref[...].astype(F32)) * _dot(bra_ref[...], wb_ref[0])
    y = y + jax.nn.sigmoid(g1_ref[...].astype(F32)) * _dot(brb_ref[...], wb_ref[1])
    y = y + jax.nn.sigmoid(g2_ref[...].astype(F32)) * _dot(att, wb_ref[2])
    y_ref[...] = y.astype(y_ref.dtype)


def merge_branches(br_a, o_f, o_b, z_main, attn_c, attn_l, dn_norm, w_branch, lay):
    t = br_a.shape[0]
    d = lay["D"]
    w = BRANCH_W
    tm = _pick(lay["n_ctx"], ROW_TILE, BF16_SUBLANES)
    tn = _pick(d, COL_TILE)
    nn = d // tn
    gate0 = 6 * w // tn
    nctx_blk = lay["n_ctx"] // tm

    def gspec(j):
        return pl.BlockSpec((tm, tn), lambda n, i: (i, gate0 + j * nn + n))

    row = lambda n, i: (i, 0)
    return pl.pallas_call(
        functools.partial(_merge1_kernel, nctx_blk=nctx_blk),
        grid=(nn, t // tm),
        in_specs=[
            pl.BlockSpec((tm, w), row),
            pl.BlockSpec((tm, w), row),
            pl.BlockSpec((tm, w), row),
            pl.BlockSpec((tm, w), lambda n, i: (i, 5)),
            pl.BlockSpec((tm, w), lambda n, i: (jnp.minimum(i, nctx_blk - 1), 0)),
            pl.BlockSpec((tm, w), lambda n, i: (jnp.maximum(i - nctx_blk, 0), 0)),
            gspec(0), gspec(1), gspec(2),
            pl.BlockSpec((1, w), lambda n, i: (0, 0)),
            pl.BlockSpec((3, w, tn), lambda n, i: (0, 0, n)),
        ],
        out_specs=pl.BlockSpec((tm, tn), lambda n, i: (i, n)),
        out_shape=jax.ShapeDtypeStruct((t, d), BF16),
        scratch_shapes=[pltpu.VMEM((tm, w), BF16)],
        compiler_params=_params(("arbitrary", "arbitrary")),
        name="merge_branches",
    )(br_a, o_f, o_b, z_main, attn_c, attn_l, z_main, z_main, z_main,
      jnp.tile(dn_norm.astype(F32), DN_HEADS).reshape(1, w), w_branch)


def _merge2_kernel(y_ref, wo_ref, x_ref, mod_ref, n2_ref, wrh_ref, wrl_ref, xo_ref, h2_ref, lg_ref, *, d):
    m = mod_ref[...]
    mix = _dot(y_ref[...], wo_ref[...])
    xn = x_ref[...] + _mod_part(m, 2, d) * mix
    xo_ref[...] = xn
    h2 = _rms(xn, n2_ref[...]) * (1.0 + _mod_part(m, 4, d)) + _mod_part(m, 3, d)
    h2_ref[...] = h2
    hh = h2.astype(BF16)
    hl = (h2 - hh.astype(F32)).astype(BF16)
    lg_ref[...] = _dot(hh, wrh_ref[...]) + (_dot(hl, wrh_ref[...]) + _dot(hh, wrl_ref[...]))


def out_proj_residual(y, w_out, x, mod, layer, norm2, w_router_pad, lay):
    t, d = x.shape
    tm = _pick(lay["n_ctx"], RES_ROW_TILE, BF16_SUBLANES)
    row = lambda i: (i, 0)
    const = lambda i: (0, 0)
    wr_hi = w_router_pad.astype(BF16)
    wr_lo = (w_router_pad - wr_hi.astype(F32)).astype(BF16)
    return pl.pallas_call(
        functools.partial(_merge2_kernel, d=d),
        grid=(t // tm,),
        in_specs=[
            pl.BlockSpec((tm, d), row),
            pl.BlockSpec((d, d), const),
            pl.BlockSpec((tm, d), row),
            _mod_spec(layer, tm, lay),
            pl.BlockSpec((1, d), const),
            pl.BlockSpec((d, LANES), const),
            pl.BlockSpec((d, LANES), const),
        ],
        out_specs=[pl.BlockSpec((tm, d), row), pl.BlockSpec((tm, d), row), pl.BlockSpec((tm, LANES), row)],
        out_shape=[jax.ShapeDtypeStruct((t, d), F32), jax.ShapeDtypeStruct((t, d), F32),
                   jax.ShapeDtypeStruct((t, LANES), F32)],
        compiler_params=_params(("arbitrary",)),
        name="out_proj_residual",
    )(y, w_out, x, mod, norm2.reshape(1, d), wr_hi, wr_lo)


def _topk_kernel(lg_ref, bias_ref, tri_ref, idx_ref, wt_ref, rank_ref, cnt_ref, run_ref):
    ne, ng = N_EXPERTS, N_GROUPS
    per = ne // ng
    lt = lg_ref[...].T
    tm = lt.shape[1]
    sc = jax.nn.sigmoid(lt[0:ne])
    ch = sc + bias_ref[...]
    ch3 = ch.reshape(ng, per, tm)
    neg = -jnp.inf
    sub = lax.broadcasted_iota(jnp.int32, (ng, per, tm), 1)
    m1 = jnp.max(ch3, axis=1, keepdims=True)
    i1 = jnp.min(jnp.where(ch3 == m1, sub, per), axis=1, keepdims=True)
    m2 = jnp.max(jnp.where(sub == i1, neg, ch3), axis=1, keepdims=True)
    gs = (m1 + m2).reshape(ng, tm)
    giota = lax.broadcasted_iota(jnp.int32, (ng, tm), 0)
    sel = jnp.zeros((ng, tm), F32)
    cur = gs
    for _ in range(TOPK_GROUPS):
        m = jnp.max(cur, axis=0, keepdims=True)
        ix = jnp.min(jnp.where(cur == m, giota, ng), axis=0, keepdims=True)
        hit = giota == ix
        sel = jnp.where(hit, 1.0, sel)
        cur = jnp.where(hit, neg, cur)
    masked = jnp.where(sel.reshape(ng, 1, tm) > 0, ch3, neg).reshape(ne, tm)
    eiota = lax.broadcasted_iota(jnp.int32, (ne, tm), 0)
    idxs, ws, hits = [], [], []
    for _ in range(TOP_K):
        m = jnp.max(masked, axis=0, keepdims=True)
        ix = jnp.min(jnp.where(masked == m, eiota, ne), axis=0, keepdims=True)
        hit = eiota == ix
        ws.append(jnp.sum(jnp.where(hit, sc, 0.0), axis=0, keepdims=True))
        idxs.append(ix)
        hits.append(hit)
        masked = jnp.where(hit, neg, masked)
    wall = jnp.concatenate(ws, axis=0)
    idx_ref[...] = jnp.concatenate(idxs, axis=0)
    wt_ref[...] = wall / jnp.sum(wall, axis=0, keepdims=True) * ROUTED_SCALE

    @pl.when(pl.program_id(0) == 0)
    def _():
        run_ref[...] = jnp.zeros(run_ref.shape, F32)

    chosen = hits[0].astype(F32)
    for hit in hits[1:]:
        chosen = chosen + hit.astype(F32)
    before = run_ref[:, 0:1] + _dot(chosen.astype(BF16), tri_ref[...])
    ranks = [jnp.sum(jnp.where(hit, before, 0.0), axis=0, keepdims=True) for hit in hits]
    rank_ref[...] = jnp.concatenate(ranks, axis=0).astype(jnp.int32)
    run_ref[...] = run_ref[...] + jnp.sum(chosen, axis=1, keepdims=True)
    cnt_ref[...] = run_ref[...]


def route_topk(logits, bias):
    t = logits.shape[0]
    tm = _pick(t, ROW_TILE)
    strict_upper = jnp.asarray(np.triu(np.ones((tm, tm), np.float32), 1), BF16)
    col = lambda i: (0, i)
    idx_t, wt_t, rank_t, cnt = pl.pallas_call(
        _topk_kernel,
        grid=(t // tm,),
        in_specs=[pl.BlockSpec((tm, LANES), lambda i: (i, 0)), pl.BlockSpec((N_EXPERTS, 1), lambda i: (0, 0)),
                  pl.BlockSpec((tm, tm), lambda i: (0, 0))],
        out_specs=[pl.BlockSpec((TOP_K, tm), col), pl.BlockSpec((TOP_K, tm), col), pl.BlockSpec((TOP_K, tm), col),
                   pl.BlockSpec((N_EXPERTS, LANES), lambda i: (0, 0))],
        out_shape=[jax.ShapeDtypeStruct((TOP_K, t), jnp.int32), jax.ShapeDtypeStruct((TOP_K, t), F32),
                   jax.ShapeDtypeStruct((TOP_K, t), jnp.int32), jax.ShapeDtypeStruct((N_EXPERTS, LANES), F32)],
        scratch_shapes=[pltpu.VMEM((N_EXPERTS, LANES), F32)],
        compiler_params=_params(("arbitrary",)),
        name="route_topk",
    )(logits, bias.astype(F32).reshape(N_EXPERTS, 1), strict_upper)
    return idx_t, wt_t, rank_t, cnt[:, 0].astype(jnp.int32)


def moe_layout(idx_t, rank_t, counts, bm):
    k, t = idx_t.shape
    n_blocks = -(-(t * k) // bm) + N_EXPERTS
    padded = (counts + bm - 1) // bm * bm
    pend = jnp.cumsum(padded)
    pstart = pend - padded
    experts = jnp.arange(N_EXPERTS, dtype=jnp.int32)[:, None, None]
    pos_t = rank_t + jnp.sum(jnp.where(idx_t[None] == experts, pstart[:, None, None], 0), axis=0)
    n_used = (pend[-1] // bm).astype(jnp.int32)
    blk = jnp.minimum(jnp.arange(n_blocks, dtype=jnp.int32), n_used - 1) * bm
    blk_e = jnp.minimum(jnp.sum((pend[None, :] <= blk[:, None]).astype(jnp.int32), axis=1), N_EXPERTS - 1)
    bidx = jnp.arange(n_blocks, dtype=jnp.int32)
    last_of_expert = jnp.any((pend[None, :] == (bidx[:, None] + 1) * bm) & (counts[None, :] > 0), axis=1)
    zero_flag = (last_of_expert | (bidx >= n_used)).astype(jnp.int32)
    blk_e = blk_e.astype(jnp.int32)
    fresh = jnp.concatenate([jnp.ones((1,), jnp.int32), (blk_e[1:] != blk_e[:-1]).astype(jnp.int32)])
    run_par = (jnp.cumsum(fresh) - 1) % 2
    active = jnp.where(counts > 0, jnp.arange(N_EXPERTS, dtype=jnp.int32), N_EXPERTS)
    later = lax.cummin(jnp.concatenate([active[1:], jnp.full((1,), N_EXPERTS, jnp.int32)]), reverse=True)
    next_e = jnp.where(later < N_EXPERTS, later, -1)[blk_e]
    pos_flat = pos_t.astype(jnp.int32).T.reshape(-1)
    return pos_flat, blk_e, n_used.reshape(1), run_par.astype(jnp.int32), next_e.astype(jnp.int32), zero_flag


def _pack_pairs(x):
    n = x.shape[1] // 2
    lo = lax.bitcast_convert_type(x[:, :n].astype(BF16).astype(F32), jnp.uint32)
    hi = lax.bitcast_convert_type(x[:, n:].astype(BF16).astype(F32), jnp.uint32)
    return (lo >> 16) | (hi & jnp.uint32(0xFFFF0000))


def _unpack_pairs(w):
    lo = lax.bitcast_convert_type(w << 16, F32)
    hi = lax.bitcast_convert_type(w & jnp.uint32(0xFFFF0000), F32)
    return lo, hi


def _row_copy(src_ref, src_row, dst_ref, dst_row, sem):
    return pltpu.make_async_copy(src_ref.at[pl.ds(src_row, 1)], dst_ref.at[pl.ds(dst_row, 1)], sem)


def _dispatch_kernel(pos_ref, zf_ref, h_ref, xs_ref, buf_ref, zero_ref, sem_ref, zsem_ref, *, tm, bm, n_blocks,
                     nsteps):
    i = pl.program_id(0)
    slot = i % 2

    def wait_slot(s):
        for _ in range(TOP_K):
            pltpu.make_async_copy(buf_ref.at[s], xs_ref.at[pl.ds(0, tm)], sem_ref.at[s]).wait()

    @pl.when(i == 0)
    def _():
        zero_ref[...] = jnp.zeros(zero_ref.shape, zero_ref.dtype)

        def zero_copy(b):
            return pltpu.make_async_copy(zero_ref, xs_ref.at[pl.ds(pl.multiple_of(b * bm, bm), bm)], zsem_ref.at[0])

        def start(b, carry):
            @pl.when(zf_ref[b] != 0)
            def _():
                zero_copy(b).start()
            return carry

        def wait(b, carry):
            @pl.when(zf_ref[b] != 0)
            def _():
                zero_copy(b).wait()
            return carry

        lax.fori_loop(0, n_blocks, start, 0)
        lax.fori_loop(0, n_blocks, wait, 0)

    @pl.when(i >= 2)
    def _():
        wait_slot(slot)

    buf_ref[slot] = _pack_pairs(h_ref[...])

    def body(tok, carry):
        for k in range(TOP_K):
            _row_copy(buf_ref.at[slot], tok, xs_ref, pos_ref[tok * TOP_K + k], sem_ref.at[slot]).start()
        return carry

    lax.fori_loop(0, tm, body, 0)

    @pl.when(i == nsteps - 1)
    def _():
        if nsteps > 1:
            wait_slot(1 - slot)
        wait_slot(slot)


def moe_dispatch(h2, pos_t, zero_flag, bm):
    t, d = h2.shape
    tm = _pick(t, MOE_ROW_TILE)
    nsteps = t // tm
    n_blocks = zero_flag.shape[0]
    return pl.pallas_call(
        functools.partial(_dispatch_kernel, tm=tm, bm=bm, n_blocks=n_blocks, nsteps=nsteps),
        grid=(nsteps,),
        in_specs=[
            pl.BlockSpec((tm * TOP_K,), lambda i: (i,), memory_space=pltpu.SMEM),
            pl.BlockSpec(memory_space=pltpu.SMEM),
            pl.BlockSpec((tm, d), lambda i: (i, 0)),
        ],
        out_specs=pl.BlockSpec(memory_space=pl.ANY),
        out_shape=jax.ShapeDtypeStruct((n_blocks * bm, d // 2), jnp.uint32),
        scratch_shapes=[pltpu.VMEM((2, tm, d // 2), jnp.uint32), pltpu.VMEM((bm, d // 2), jnp.uint32),
                        pltpu.SemaphoreType.DMA((2,)), pltpu.SemaphoreType.DMA((1,))],
        compiler_params=_params(("arbitrary",)),
        name="moe_dispatch",
    )(pos_t, zero_flag, h2)


def _moe_kernel(be_ref, nu_ref, par_ref, nx_ref, x_ref, w1_hbm, w3_hbm, w2_hbm, y_ref, f1_s, f3_s, f2_s, w13_s, w2_s,
                sem_ref, *, ed, half, layer):
    i = pl.program_id(0)
    prev = be_ref[jnp.maximum(i - 1, 0)]
    fresh = jnp.logical_or(i == 0, be_ref[i] != prev)
    slot = par_ref[i]

    def weight_copies(e, s):
        return (pltpu.make_async_copy(w1_hbm.at[layer, e], f1_s.at[s], sem_ref.at[s, 0]),
                pltpu.make_async_copy(w3_hbm.at[layer, e], f3_s.at[s], sem_ref.at[s, 1]),
                pltpu.make_async_copy(w2_hbm.at[layer, e], f2_s.at[s], sem_ref.at[s, 2]))

    @pl.when(i == 0)
    def _():
        for cp in weight_copies(be_ref[0], slot):
            cp.start()

    @pl.when(fresh)
    def _():
        for cp in weight_copies(be_ref[i], slot):
            cp.wait()

        @pl.when(nx_ref[i] >= 0)
        def _():
            for cp in weight_copies(nx_ref[i], 1 - slot):
                cp.start()

        w13_s[:, 0:ed] = f1_s[slot].astype(BF16)
        w13_s[:, ed:2 * ed] = f3_s[slot].astype(BF16)
        w2_s[...] = f2_s[slot].astype(BF16)

    @pl.when(i < nu_ref[0])
    def _():
        lo, hi = _unpack_pairs(x_ref[...])
        a = _dot(lo.astype(BF16), w13_s[0:half, :]) + _dot(hi.astype(BF16), w13_s[half:2 * half, :])
        act = (_silu(a[:, 0:ed]) * a[:, ed:2 * ed]).astype(BF16)
        y_ref[...] = _pack_pairs(_dot(act, w2_s[...]))

    @pl.when(i >= nu_ref[0])
    def _():
        y_ref[...] = jnp.zeros(y_ref.shape, y_ref.dtype)


def moe_experts(xs, blk_e, n_used, run_par, next_e, w1, w3, w2, layer, bm):
    n_rows, half = xs.shape
    d = 2 * half
    ed = w1.shape[-1]
    n_blocks = n_rows // bm
    any_spec = pl.BlockSpec(memory_space=pl.ANY)
    grid_spec = pltpu.PrefetchScalarGridSpec(
        num_scalar_prefetch=4,
        grid=(n_blocks,),
        in_specs=[pl.BlockSpec((bm, half), lambda i, *_: (i, 0)), any_spec, any_spec, any_spec],
        out_specs=pl.BlockSpec((bm, half), lambda i, *_: (i, 0)),
        scratch_shapes=[pltpu.VMEM((2, d, ed), F32), pltpu.VMEM((2, d, ed), F32), pltpu.VMEM((2, ed, d), F32),
                        pltpu.VMEM((d, 2 * ed), BF16), pltpu.VMEM((ed, d), BF16),
                        pltpu.SemaphoreType.DMA((2, 3))],
    )
    return pl.pallas_call(
        functools.partial(_moe_kernel, ed=ed, half=half, layer=layer),
        grid_spec=grid_spec,
        out_shape=jax.ShapeDtypeStruct(xs.shape, jnp.uint32),
        compiler_params=_params(("arbitrary",)),
        name="moe_experts",
    )(blk_e, n_used, run_par, next_e, xs, w1, w3, w2)


def _shared_kernel(pos_ref, posn_ref, wt_ref, h2_ref, w13_ref, w2_ref, ys_ref, x_ref, mod_ref, ng_ref, nmod_ref,
                   xo_ref, ho_ref, ybuf_ref, sem_ref, *, d, ed, tm, nsteps, final):
    i = pl.program_id(0)
    slot = i % 2

    def issue(p_ref, s):
        def body(tok, carry):
            for k in range(TOP_K):
                _row_copy(ys_ref, p_ref[tok * TOP_K + k], ybuf_ref.at[s, k], tok, sem_ref.at[s]).start()
            return carry
        lax.fori_loop(0, tm, body, 0)

    @pl.when(i == 0)
    def _():
        issue(pos_ref, 0)

    @pl.when(i + 1 < nsteps)
    def _():
        issue(posn_ref, 1 - slot)

    m = mod_ref[...]
    a = _dot(h2_ref[...].astype(BF16), w13_ref[...])
    act = (_silu(a[:, 0:ed]) * a[:, ed:2 * ed]).astype(BF16)
    f = _dot(act, w2_ref[...])

    for k in range(TOP_K):
        pltpu.make_async_copy(ys_ref.at[pl.ds(0, tm)], ybuf_ref.at[slot, k], sem_ref.at[slot]).wait()
    half = d // 2
    wks = [wt_ref[:, k:k + 1] for k in range(TOP_K)]
    r_lo, r_hi = [], []
    for c in range(half // LANES):
        cs = slice(c * LANES, (c + 1) * LANES)
        acc_lo = acc_hi = None
        for k in range(TOP_K):
            lo, hi = _unpack_pairs(ybuf_ref[slot, k, :, cs])
            acc_lo = wks[k] * lo if acc_lo is None else acc_lo + wks[k] * lo
            acc_hi = wks[k] * hi if acc_hi is None else acc_hi + wks[k] * hi
        r_lo.append(acc_lo)
        r_hi.append(acc_hi)
    f = f + jnp.concatenate(r_lo + r_hi, axis=1)
    xn = x_ref[...] + _mod_part(m, 5, d) * f
    xo_ref[...] = xn
    y = _rms(xn, ng_ref[...])
    if not final:
        nm = nmod_ref[...]
        y = y * (1.0 + _mod_part(nm, 1, d)) + _mod_part(nm, 0, d)
    ho_ref[...] = y.astype(ho_ref.dtype)


def shared_residual(h2, sw13, sw2, ys, pos_t, wts, x, mod, layer, next_g, final, lay):
    t, d = x.shape
    ed = sw2.shape[0]
    tm = _pick(lay["n_ctx"], RES_ROW_TILE, BF16_SUBLANES)
    r0 = lay["n_ctx"] // tm if final else 0
    nsteps = t // tm - r0
    row = lambda i: (i + r0, 0)
    out_row = lambda i: (i, 0)
    const = lambda i: (0, 0)
    next_layer = layer if final else layer + 1
    n_out = nsteps * tm
    return pl.pallas_call(
        functools.partial(_shared_kernel, d=d, ed=ed, tm=tm, nsteps=nsteps, final=final),
        grid=(nsteps,),
        in_specs=[
            pl.BlockSpec((tm * TOP_K,), lambda i: (i + r0,), memory_space=pltpu.SMEM),
            pl.BlockSpec((tm * TOP_K,), lambda i: (jnp.minimum(i + 1, nsteps - 1) + r0,), memory_space=pltpu.SMEM),
            pl.BlockSpec((tm, TOP_K), row),
            pl.BlockSpec((tm, d), row),
            pl.BlockSpec((d, 2 * ed), const),
            pl.BlockSpec((ed, d), const),
            pl.BlockSpec(memory_space=pl.ANY),
            pl.BlockSpec((tm, d), row),
            _mod_spec(layer, tm, lay, r0),
            pl.BlockSpec((1, d), const),
            _mod_spec(next_layer, tm, lay, r0),
        ],
        out_specs=[pl.BlockSpec((tm, d), out_row), pl.BlockSpec((tm, d), out_row)],
        out_shape=[jax.ShapeDtypeStruct((n_out, d), F32), jax.ShapeDtypeStruct((n_out, d), F32 if final else BF16)],
        scratch_shapes=[pltpu.VMEM((2, TOP_K, tm, d // 2), jnp.uint32), pltpu.SemaphoreType.DMA((2,))],
        compiler_params=_params(("arbitrary",)),
        name="shared_residual",
    )(pos_t, pos_t, wts, h2, sw13, sw2, ys, x, mod, next_g.reshape(1, d), mod)


def _rope_tables(lay):
    half = MLA_ROPE // 4
    s = lay["S"]
    freq = ROPE_BASE ** (-np.arange(half, dtype=np.float64) / half)
    tpos = np.arange(s)
    ang_r = (tpos // GRID_W)[:, None] * freq
    ang_c = (tpos % GRID_W)[:, None] * freq
    ang = np.concatenate([ang_r, ang_r, ang_c, ang_c], axis=1)
    cos_l = np.tile(np.cos(ang), (lay["B"], 2))
    sin_l = np.tile(np.sin(ang), (lay["B"], 2))
    cos = np.concatenate([np.ones((lay["n_ctx"], 2 * MLA_ROPE)), cos_l], axis=0).astype(np.float32)
    sin = np.concatenate([np.zeros((lay["n_ctx"], 2 * MLA_ROPE)), sin_l], axis=0).astype(np.float32)
    return jnp.asarray(cos), jnp.asarray(sin)


def _rot_cols(w):
    q = MLA_ROPE // 4
    a, b, c, e = w[..., 0:q], w[..., q:2 * q], w[..., 2 * q:3 * q], w[..., 3 * q:4 * q]
    return jnp.concatenate([-b, a, -e, c], axis=-1)


def _split_w_in_t_kernel(x_ref, main_ref, mla_ref, small_ref, *, o_small, o_cq, o_kr, o_gate, n_gate):
    ch = 2 * LANES
    for a in range(0, o_small, ch):
        main_ref[:, a:a + ch] = x_ref[a:a + ch, :].T.astype(main_ref.dtype)
    for a in range(0, n_gate, ch):
        main_ref[:, o_small + a:o_small + a + ch] = x_ref[o_gate + a:o_gate + a + ch, :].T.astype(main_ref.dtype)
    for a in range(0, o_kr - o_cq, ch):
        mla_ref[:, a:a + ch] = x_ref[o_cq + a:o_cq + a + ch, :].T.astype(mla_ref.dtype)
    n_ab = o_cq - o_small
    w0 = o_kr - n_ab
    t_ab = x_ref[o_small:o_small + LANES, :].T
    t_kr = x_ref[w0:w0 + LANES, :].T
    lane = lax.broadcasted_iota(jnp.int32, t_ab.shape, 1)
    small_ref[...] = jnp.where(lane < n_ab, t_ab, jnp.where(lane < n_ab + MLA_ROPE, t_kr, 0.0))


def split_w_in(w_in_all, layer, offs):
    o_small, o_cq, o_kr, o_gate = offs
    _, d, d_in = w_in_all.shape
    n_gate = d_in - o_gate
    tk = _pick(d, 2 * LANES)
    row = lambda i: (i, 0)
    return pl.pallas_call(
        functools.partial(_split_w_in_t_kernel, o_small=o_small, o_cq=o_cq, o_kr=o_kr, o_gate=o_gate,
                          n_gate=n_gate),
        grid=(d // tk,),
        in_specs=[pl.BlockSpec((None, d_in, tk), lambda i: (layer, 0, i))],
        out_specs=[pl.BlockSpec((tk, o_small + n_gate), row), pl.BlockSpec((tk, o_kr - o_cq), row),
                   pl.BlockSpec((tk, LANES), row)],
        out_shape=[jax.ShapeDtypeStruct((d, o_small + n_gate), BF16), jax.ShapeDtypeStruct((d, o_kr - o_cq), BF16),
                   jax.ShapeDtypeStruct((d, LANES), F32)],
        compiler_params=_params(("arbitrary",)),
        name="split_w_in",
    )(jnp.swapaxes(w_in_all, 1, 2))


def _prep_layer_weights(w_in_all, layer, w_uq, w_ukv, w_branch, w_out, w_router, sw1, sw3, sw2, d):
    w = BRANCH_W
    o_small = 6 * w
    o_cq = o_small + 4 * DN_HEADS
    o_kr = o_cq + Q_LORA + KV_LORA
    o_gate = o_kr + MLA_ROPE
    w_main, w_mla, w_small = split_w_in(w_in_all, layer, (o_small, o_cq, o_kr, o_gate))
    n_ab = o_cq - o_small
    w_kr = w_small[:, n_ab:n_ab + MLA_ROPE]
    zpad = jnp.zeros((d, LANES - MLA_ROPE), F32)
    w_misc = jnp.concatenate(
        [w_kr, zpad, _rot_cols(w_kr), zpad, w_small[:, 0:n_ab], jnp.zeros((d, LANES - n_ab), F32)],
        axis=1).astype(BF16)
    uq = w_uq.reshape(Q_LORA, MLA_HEADS, MLA_QK)
    uq_r = uq[..., MLA_NOPE:]
    wq_ext = jnp.concatenate(
        [uq[..., :MLA_NOPE].reshape(Q_LORA, -1), uq_r.reshape(Q_LORA, -1), _rot_cols(uq_r).reshape(Q_LORA, -1)],
        axis=1).astype(BF16)
    wr_pad = jnp.zeros((d, LANES), F32).at[:, :N_EXPERTS].set(w_router.astype(F32))
    return dict(w_main=w_main, w_mla=w_mla, w_misc=w_misc, wq_ext=wq_ext, w_ukv=w_ukv.astype(BF16),
                w_branch=w_branch.astype(BF16), w_out=w_out.astype(BF16), wr_pad=wr_pad,
                sw13=jnp.concatenate([sw1, sw3], axis=1).astype(BF16), sw2=sw2.astype(BF16))


def kernel(x, c, ctx, c_ctx, w_mod, b_mod, norm1, norm2, w_in, gm_norm, gm_ws, gm_bs, dn_conv, dn_a_log,
           dn_dt_bias, dn_norm, mla_q_norm, mla_kv_norm, mla_w_uq, mla_w_ukv, w_branch, w_out, moe_router,
           moe_bias, moe_w1, moe_w3, moe_w2, shared_w1, shared_w3, shared_w2, final_norm):
    nb, s, d = x.shape
    cl = ctx.shape[1]
    depth = w_mod.shape[0]
    n_ctx = nb * cl
    t = n_ctx + nb * s
    lay = dict(B=nb, S=s, C=cl, D=d, n_ctx=n_ctx, T=t)
    assert nb + 1 <= SUBLANES and s % GRID_W == 0

    xs = jnp.concatenate([ctx.reshape(n_ctx, d), x.reshape(nb * s, d)], axis=0).astype(F32)
    cvec = jnp.zeros((SUBLANES, d), F32).at[0].set(c_ctx.astype(F32)).at[1:1 + nb].set(c.astype(F32))
    mod = modulation(cvec, w_mod, b_mod)
    cos2, sin2 = _rope_tables(lay)
    scale = MLA_QK ** -0.5 * LOG2E

    h = prenorm(xs, norm1[0], mod, 0, lay)
    out = None
    for i in range(depth):
        last = i == depth - 1
        wts = _prep_layer_weights(w_in, i, mla_w_uq[i], mla_w_ukv[i], w_branch[i], w_out[i], moe_router[i],
                                  shared_w1[i], shared_w3[i], shared_w2[i], d)
        z_main = matmul(h, wts["w_main"], BF16, "in_proj_main")
        z_mla = matmul(h, wts["w_mla"], BF16, "in_proj_mla")
        z_misc = matmul(h, wts["w_misc"], F32, "in_proj_misc")

        br_a = gmlp(z_main, gm_norm[i], gm_ws[i], gm_bs[i], lay)

        qkv_c = dn_short_conv(z_main, dn_conv[i], lay)
        o_f, o_b = dn_scan(qkv_c, z_misc, dn_a_log[i], dn_dt_bias[i], lay)

        kc, vc = mla_kv(z_mla, z_misc, mla_kv_norm[i], wts["w_ukv"], cos2, sin2, 0, n_ctx)
        kl, vl = mla_kv(z_mla, z_misc, mla_kv_norm[i], wts["w_ukv"], cos2, sin2, n_ctx, nb * s)
        ql = mla_q(z_mla, mla_q_norm[i], wts["wq_ext"], cos2, sin2, n_ctx, nb * s, scale)
        attn_l = attention(ql, kc, vc, kl, vl, s, lay)
        qc = mla_q(z_mla, mla_q_norm[i], wts["wq_ext"], cos2, sin2, 0, n_ctx, scale)
        attn_c = attention(qc, kc, vc, None, None, cl, lay)

        y = merge_branches(br_a, o_f, o_b, z_main, attn_c, attn_l, dn_norm[i], wts["w_branch"], lay)
        xs, h2, logits = out_proj_residual(y, wts["w_out"], xs, mod, i, norm2[i], wts["wr_pad"], lay)

        idx_t, wt_t, rank_t, counts = route_topk(logits, moe_bias[i])
        bm = MOE_ROW_TILE
        pos_t, blk_e, n_used, run_par, next_e, zero_flag = moe_layout(idx_t, rank_t, counts, bm)
        x_sorted = moe_dispatch(h2, pos_t, zero_flag, bm)
        y_sorted = moe_experts(x_sorted, blk_e, n_used, run_par, next_e, moe_w1, moe_w3, moe_w2, i, bm)

        next_g = final_norm if last else norm1[i + 1]
        xs, h = shared_residual(h2, wts["sw13"], wts["sw2"], y_sorted, pos_t, wt_t.T, xs, mod, i, next_g, last, lay)
        out = h
    return out.reshape(nb, s, d).astype(x.dtype)
```

```python
import functools
import math

import numpy as np
import jax
import jax.numpy as jnp
from jax import lax
from jax.experimental import pallas as pl
from jax.experimental.pallas import tpu as pltpu

F32 = jnp.float32
BF16 = jnp.bfloat16

GRID_W = 64
EPS = 1e-6
GM_CHUNK = 128
GM_GROUPS = 8
DN_HEADS = 8
DN_HEAD_DIM = 128
DN_CHUNK = 64
DN_CONV = 5
MLA_HEADS = 8
MLA_NOPE = 128
MLA_ROPE = 64
MLA_V = 128
MLA_QK = MLA_NOPE + MLA_ROPE
MLA_VE = 2 * MLA_V
Q_LORA = 512
KV_LORA = 512
ROPE_BASE = 10000.0
BRANCH_W = 1024
N_EXPERTS = 64
TOP_K = 8
N_GROUPS = 8
TOPK_GROUPS = 4
ROUTED_SCALE = 2.5
LOG2E = 1.4426950408889634

LANES = 128
SUBLANES = 8
BF16_SUBLANES = 16
VMEM_LIMIT_MB = 56

ROW_TILE = 512
RES_ROW_TILE = 256
CONV_ROW_TILE = 256
COL_TILE = 1024
MOE_ROW_TILE = 512
MOE_TOKEN_TILE = 256
ATTN_Q_TILE = 1024
ATTN_KV_TILE = 1024


def _pick(n, pref, mult=LANES):
    if n <= pref:
        return n
    for t in range(pref - pref % mult, 0, -mult):
        if n % t == 0:
            return t
    return n


def _params(sem, mb=VMEM_LIMIT_MB):
    return pltpu.CompilerParams(dimension_semantics=sem, vmem_limit_bytes=mb * 1024 * 1024)


def _silu(x):
    return x * jax.nn.sigmoid(x)


def _gelu(x):
    return 0.5 * x * (1.0 + jnp.tanh(0.7978845608028654 * (x + 0.044715 * x * x * x)))


def _dot(a, b):
    return jnp.dot(a, b, preferred_element_type=F32)


def _dot_nt(a, b):
    return lax.dot_general(a, b, (((1,), (1,)), ((), ())), preferred_element_type=F32)


def _dot_tn(a, b):
    return lax.dot_general(a, b, (((0,), (0,)), ((), ())), preferred_element_type=F32)


def _mod_kernel(c_ref, w_ref, b_ref, o_ref):
    s = _silu(c_ref[...])
    o_ref[...] = _dot(s.astype(BF16), w_ref[...].astype(BF16)) + b_ref[...]


def modulation(cvec, w_mod, b_mod):
    nl, d, n6 = w_mod.shape
    tn = _pick(n6, COL_TILE)
    out = pl.pallas_call(
        _mod_kernel,
        grid=(nl, n6 // tn),
        in_specs=[
            pl.BlockSpec((SUBLANES, d), lambda l, n: (0, 0)),
            pl.BlockSpec((None, d, tn), lambda l, n: (l, 0, n)),
            pl.BlockSpec((None, 1, tn), lambda l, n: (l, 0, n)),
        ],
        out_specs=pl.BlockSpec((None, SUBLANES, tn), lambda l, n: (l, 0, n)),
        out_shape=jax.ShapeDtypeStruct((nl, SUBLANES, n6), F32),
        compiler_params=_params(("arbitrary", "arbitrary")),
        name="modulation",
    )(cvec, w_mod, b_mod.reshape(nl, 1, n6))
    return out.reshape(nl, SUBLANES, 1, n6)


def _seg_of_block(i, tm, lay):
    nctx_blk = lay["n_ctx"] // tm
    lat_bps = lay["S"] // tm
    return jnp.where(i < nctx_blk, 0, 1 + (i - nctx_blk) // lat_bps)


def _mod_spec(layer, tm, lay, row0_blk=0):
    n6 = 6 * lay["D"]
    return pl.BlockSpec((None, None, 1, n6), lambda i: (layer, _seg_of_block(i + row0_blk, tm, lay), 0, 0))


def _mod_part(m, k, d):
    return m[:, k * d:(k + 1) * d]


def _rms(x, g):
    return x * lax.rsqrt(jnp.mean(x * x, axis=-1, keepdims=True) + EPS) * g


def _prenorm_kernel(x_ref, g_ref, mod_ref, o_ref, *, d):
    m = mod_ref[...]
    y = _rms(x_ref[...], g_ref[...])
    o_ref[...] = (y * (1.0 + _mod_part(m, 1, d)) + _mod_part(m, 0, d)).astype(o_ref.dtype)


def prenorm(x, g, mod, layer, lay):
    t, d = x.shape
    tm = _pick(lay["n_ctx"], ROW_TILE, SUBLANES)
    return pl.pallas_call(
        functools.partial(_prenorm_kernel, d=d),
        grid=(t // tm,),
        in_specs=[
            pl.BlockSpec((tm, d), lambda i: (i, 0)),
            pl.BlockSpec((1, d), lambda i: (0, 0)),
            _mod_spec(layer, tm, lay),
        ],
        out_specs=pl.BlockSpec((tm, d), lambda i: (i, 0)),
        out_shape=jax.ShapeDtypeStruct((t, d), BF16),
        compiler_params=_params(("arbitrary",)),
        name="prenorm",
    )(x, g.reshape(1, d), mod)


def _mm_kernel(x_ref, w_ref, o_ref):
    o_ref[...] = _dot(x_ref[...], w_ref[...]).astype(o_ref.dtype)


def matmul(x, w, out_dtype, name):
    t, k = x.shape
    n = w.shape[1]
    tm = _pick(t, ROW_TILE, BF16_SUBLANES)
    tn = _pick(n, COL_TILE)
    return pl.pallas_call(
        _mm_kernel,
        grid=(n // tn, t // tm),
        in_specs=[
            pl.BlockSpec((tm, k), lambda j, i: (i, 0)),
            pl.BlockSpec((k, tn), lambda j, i: (0, j)),
        ],
        out_specs=pl.BlockSpec((tm, tn), lambda j, i: (i, j)),
        out_shape=jax.ShapeDtypeStruct((t, n), out_dtype),
        compiler_params=_params(("arbitrary", "arbitrary")),
        name=name,
    )(x, w)


def _gmlp_kernel(u_ref, v_ref, gn_ref, ws_ref, bias_ref, o_ref, *, nchunk):
    for j in range(nchunk):
        rs = slice(j * GM_CHUNK, (j + 1) * GM_CHUNK)
        v = _gelu(v_ref[rs, :].astype(F32))
        vb = _rms(v, gn_ref[...]).astype(BF16)
        u = _gelu(u_ref[rs, :].astype(F32))
        for g in range(GM_GROUPS):
            cs = slice(g * LANES, (g + 1) * LANES)
            mixed = _dot(ws_ref[g], vb[:, cs]) + bias_ref[:, cs]
            o_ref[rs, cs] = (u[:, cs] * mixed).astype(o_ref.dtype)


def gmlp(z_main, gm_norm, gm_ws, gm_bs, lay):
    t = z_main.shape[0]
    w = BRANCH_W
    tm = _pick(lay["n_ctx"], ROW_TILE, GM_CHUNK)
    bias = jnp.repeat(gm_bs.T.astype(F32), w // GM_GROUPS, axis=1)
    return pl.pallas_call(
        functools.partial(_gmlp_kernel, nchunk=tm // GM_CHUNK),
        grid=(t // tm,),
        in_specs=[
            pl.BlockSpec((tm, w), lambda i: (i, 0)),
            pl.BlockSpec((tm, w), lambda i: (i, 1)),
            pl.BlockSpec((1, w), lambda i: (0, 0)),
            pl.BlockSpec((GM_GROUPS, GM_CHUNK, GM_CHUNK), lambda i: (0, 0, 0)),
            pl.BlockSpec((GM_CHUNK, w), lambda i: (0, 0)),
        ],
        out_specs=pl.BlockSpec((tm, w), lambda i: (i, 0)),
        out_shape=jax.ShapeDtypeStruct((t, w), BF16),
        compiler_params=_params(("arbitrary",)),
        name="gmlp",
    )(z_main, z_main, gm_norm.reshape(1, w), gm_ws.astype(BF16), bias)


def _conv_kernel(x_ref, hp_ref, hn_ref, w_ref, o_ref, xe_ref, *, tm, nctx_blk, ctx_bps, lat_bps):
    j = pl.program_id(0)
    i = pl.program_id(1)
    li = i - nctx_blk
    seg_start = jnp.where(i < nctx_blk, (i % ctx_bps) == 0, (li % lat_bps) == 0)
    seg_end = jnp.where(i < nctx_blk, ((i + 1) % ctx_bps) == 0, ((li + 1) % lat_bps) == 0)
    halo = BF16_SUBLANES
    xe_ref[0:halo, :] = jnp.where(seg_start, 0.0, hp_ref[...].astype(F32))
    xe_ref[halo:halo + tm, :] = x_ref[...].astype(F32)
    xe_ref[halo + tm:2 * halo + tm, :] = jnp.where(seg_end, 0.0, hn_ref[...].astype(F32))
    base = halo - DN_CONV // 2
    acc = w_ref[0:1, :] * xe_ref[base:base + tm, :]
    for tap in range(1, DN_CONV):
        acc = acc + w_ref[tap:tap + 1, :] * xe_ref[base + tap:base + tap + tm, :]
    y = _silu(acc)
    unit = j < 2
    for h in range(DN_HEADS):
        cs = slice(h * DN_HEAD_DIM, (h + 1) * DN_HEAD_DIM)
        yh = y[:, cs]
        nrm = yh * lax.rsqrt(jnp.sum(yh * yh, axis=-1, keepdims=True) + EPS)
        o_ref[:, cs] = jnp.where(unit, nrm, yh).astype(o_ref.dtype)


def dn_short_conv(z_main, conv_w, lay):
    t = z_main.shape[0]
    w = BRANCH_W
    tm = _pick(lay["C"], CONV_ROW_TILE, BF16_SUBLANES)
    halo = BF16_SUBLANES
    hb = tm // halo
    nhalo = t // halo
    wpad = jnp.zeros((SUBLANES, 3 * w), F32).at[:DN_CONV].set(conv_w.astype(F32))
    kern = functools.partial(_conv_kernel, tm=tm, nctx_blk=lay["n_ctx"] // tm, ctx_bps=lay["C"] // tm,
                             lat_bps=lay["S"] // tm)
    return pl.pallas_call(
        kern,
        grid=(3, t // tm),
        in_specs=[
            pl.BlockSpec((tm, w), lambda j, i: (i, 2 + j)),
            pl.BlockSpec((halo, w), lambda j, i: (jnp.maximum(i * hb - 1, 0), 2 + j)),
            pl.BlockSpec((halo, w), lambda j, i: (jnp.minimum((i + 1) * hb, nhalo - 1), 2 + j)),
            pl.BlockSpec((SUBLANES, w), lambda j, i: (0, j)),
        ],
        out_specs=pl.BlockSpec((tm, w), lambda j, i: (i, j)),
        out_shape=jax.ShapeDtypeStruct((t, 3 * w), BF16),
        scratch_shapes=[pltpu.VMEM((tm + 2 * halo, w), F32)],
        compiler_params=_params(("arbitrary", "arbitrary")),
        name="dn_conv",
    )(z_main, z_main, z_main, wpad)


def _dn_masks():
    c = DN_CHUNK
    r = np.arange(c)[:, None]
    s = np.arange(c)[None, :]
    tri = np.stack([r >= s, r <= s]).astype(np.float32)
    strict = np.stack([r > s, r < s]).astype(np.float32)
    lv = []
    b = 1
    while b < c:
        same = (r // (2 * b)) == (s // (2 * b))
        lo = same & ((r // b) % 2 == 1) & ((s // b) % 2 == 0)
        up = same & ((r // b) % 2 == 0) & ((s // b) % 2 == 1)
        lv.append(np.stack([lo, up]))
        b *= 2
    lvl = np.stack(lv, axis=1).astype(np.float32)
    return tri, strict, lvl, np.eye(c, dtype=np.float32)


def _dn_kernel(qf_ref, qb_ref, abf_ref, abb_ref, alog_ref, dt_ref, tri_ref, strict_ref, lvl_ref, eye_ref,
               of_ref, ob_ref, s_ref, *, nlevels):
    step = pl.program_id(1)

    @pl.when(step == 0)
    def _():
        s_ref[...] = jnp.zeros(s_ref.shape, F32)

    hd = DN_HEAD_DIM
    w = DN_HEADS * hd
    nh = DN_HEADS
    cc = DN_CHUNK
    eye = eye_ref[...]
    inst = []
    for d, (x_ref, ab_ref, o_ref) in enumerate(((qf_ref, abf_ref, of_ref), (qb_ref, abb_ref, ob_ref))):
        ab = ab_ref[...]
        g_all = -jnp.exp(alog_ref[...]) * jax.nn.softplus(ab + dt_ref[...])
        beta_all = jax.nn.sigmoid(ab)
        gam_c = jnp.dot(tri_ref[d], g_all, precision=lax.Precision.HIGHEST, preferred_element_type=F32)
        gam_r = gam_c.T
        for h in range(nh):
            lane = d * nh + h
            inst.append(dict(
                d=d, lane=lane, h=h, slot=lane, o_ref=o_ref,
                q=x_ref[:, h * hd:(h + 1) * hd].astype(F32),
                k=x_ref[:, w + h * hd:w + (h + 1) * hd].astype(F32),
                v=x_ref[:, 2 * w + h * hd:2 * w + (h + 1) * hd].astype(F32),
                gc=gam_c[:, lane:lane + 1],
                gr=gam_r[lane:lane + 1, :],
                bc=beta_all[:, 2 * nh + lane:2 * nh + lane + 1],
                st=s_ref[lane]))
    for it in inst:
        tri = tri_ref[it["d"]]
        diff = it["gc"] - it["gr"]
        it["dec"] = jnp.where(tri > 0, jnp.exp(jnp.where(tri > 0, diff, 0.0)), 0.0)
        it["kb"] = it["k"] * it["bc"]
        it["qs"] = it["q"] * (hd ** -0.5)
    for it in inst:
        kq = _dot_nt(jnp.concatenate([it["kb"], it["qs"]], axis=0).astype(BF16), it["k"].astype(BF16))
        it["a"] = kq[:cc] * it["dec"] * strict_ref[it["d"]]
        it["attn"] = (kq[cc:] * it["dec"]).astype(BF16)
    for it in inst:
        it["x"] = eye - it["a"] * lvl_ref[it["d"], 0]
    for lv in range(1, nlevels):
        for it in inst:
            it["xb"] = it["x"].astype(BF16)
            it["p"] = _dot(it["xb"], (it["a"] * lvl_ref[it["d"], lv]).astype(BF16)).astype(BF16)
        for it in inst:
            it["x"] = it["x"] - _dot(it["p"], it["xb"])
    for it in inst:
        eg = jnp.exp(it["gc"])
        rhs = jnp.concatenate([it["kb"] * eg, it["v"] * it["bc"]], axis=1).astype(BF16)
        it["sol"] = _dot(it["x"].astype(BF16), rhs)
        last = cc - 1 if it["d"] == 0 else 0
        g_last = it["gc"][last:last + 1, :]
        it["gtot"] = jnp.exp(g_last)
        it["kd"] = (it["k"] * jnp.exp(g_last - it["gc"])).astype(BF16)
        it["qd"] = it["qs"] * eg
    for it in inst:
        wq = jnp.concatenate([it["sol"][:, :hd], it["qd"]], axis=0).astype(BF16)
        it["r"] = _dot(wq, it["st"].astype(BF16))
    for it in inst:
        it["vn"] = (it["sol"][:, hd:] - it["r"][:cc]).astype(BF16)
    for it in inst:
        it["o"] = it["r"][cc:] + _dot(it["attn"], it["vn"])
        it["sn"] = it["st"] * it["gtot"] + _dot_tn(it["kd"], it["vn"])
    for it in inst:
        it["o_ref"][:, it["h"] * hd:(it["h"] + 1) * hd] = it["o"]
        s_ref[it["slot"]] = it["sn"]


def dn_scan(qkv_c, z_misc, a_log, dt_bias, lay):
    t = qkv_c.shape[0]
    w = BRANCH_W
    c = DN_CHUNK
    nb = lay["B"]
    ncc = lay["C"] // c
    ncl = lay["S"] // c
    nsteps = ncc + ncl

    def fidx(b, s):
        return jnp.where(s < ncc, b * ncc + s, nb * ncc + b * ncl + (s - ncc))

    def bidx(b, s):
        return jnp.where(s < ncc, b * ncc + (ncc - 1 - s), nb * ncc + b * ncl + (ncl - 1 - (s - ncc)))

    tri, strict, lvl, eye = _dn_masks()
    nlevels = lvl.shape[1]
    nh = DN_HEADS
    alog_row = jnp.zeros((1, LANES), F32).at[0, :2 * nh].set(a_log.reshape(-1).astype(F32))
    dt_row = jnp.zeros((1, LANES), F32).at[0, :2 * nh].set(dt_bias.reshape(-1).astype(F32))
    const2 = lambda b, s: (0, 0)
    const3 = lambda b, s: (0, 0, 0)
    const4 = lambda b, s: (0, 0, 0, 0)
    return pl.pallas_call(
        functools.partial(_dn_kernel, nlevels=nlevels),
        grid=(nb, nsteps),
        in_specs=[
            pl.BlockSpec((c, 3 * w), lambda b, s: (fidx(b, s), 0)),
            pl.BlockSpec((c, 3 * w), lambda b, s: (bidx(b, s), 0)),
            pl.BlockSpec((c, LANES), lambda b, s: (fidx(b, s), 2)),
            pl.BlockSpec((c, LANES), lambda b, s: (bidx(b, s), 2)),
            pl.BlockSpec((1, LANES), const2),
            pl.BlockSpec((1, LANES), const2),
            pl.BlockSpec((2, c, c), const3),
            pl.BlockSpec((2, c, c), const3),
            pl.BlockSpec((2, nlevels, c, c), const4),
            pl.BlockSpec((c, c), const2),
        ],
        out_specs=[
            pl.BlockSpec((c, w), lambda b, s: (fidx(b, s), 0)),
            pl.BlockSpec((c, w), lambda b, s: (bidx(b, s), 0)),
        ],
        out_shape=[jax.ShapeDtypeStruct((t, w), F32), jax.ShapeDtypeStruct((t, w), F32)],
        scratch_shapes=[pltpu.VMEM((2 * nh, DN_HEAD_DIM, DN_HEAD_DIM), F32)],
        compiler_params=_params(("arbitrary", "arbitrary")),
        name="dn_scan",
    )(qkv_c, qkv_c, z_misc, z_misc, alog_row, dt_row, jnp.asarray(tri), jnp.asarray(strict), jnp.asarray(lvl),
      jnp.asarray(eye))


def _mla_q_kernel(z_ref, qn_ref, w_ref, cos_ref, sin_ref, o_ref, *, scale):
    nh = MLA_HEADS
    xn = _rms(z_ref[...].astype(F32), qn_ref[...]).astype(BF16)
    na = nh * MLA_NOPE
    nr = nh * MLA_ROPE
    qa = _dot(xn, w_ref[:, 0:na])
    qr = _dot(xn, w_ref[:, na:na + nr])
    qt = _dot(xn, w_ref[:, na + nr:na + 2 * nr])
    reps = nr // LANES
    cos = jnp.concatenate([cos_ref[...]] * reps, axis=1)
    sin = jnp.concatenate([sin_ref[...]] * reps, axis=1)
    qrr = qr * cos + qt * sin
    for h in range(nh):
        o_ref[h, :, 0:MLA_NOPE] = (qa[:, h * MLA_NOPE:(h + 1) * MLA_NOPE] * scale).astype(o_ref.dtype)
        o_ref[h, :, MLA_NOPE:MLA_QK] = (qrr[:, h * MLA_ROPE:(h + 1) * MLA_ROPE] * scale).astype(o_ref.dtype)


def mla_q(z_mla, q_norm, wq_ext, cos2, sin2, row0, nrows, scale):
    tm = _pick(nrows, ROW_TILE, BF16_SUBLANES)
    assert row0 % tm == 0
    r0 = row0 // tm
    return pl.pallas_call(
        functools.partial(_mla_q_kernel, scale=scale),
        grid=(nrows // tm,),
        in_specs=[
            pl.BlockSpec((tm, Q_LORA), lambda i: (i + r0, 0)),
            pl.BlockSpec((1, Q_LORA), lambda i: (0, 0)),
            pl.BlockSpec(wq_ext.shape, lambda i: (0, 0)),
            pl.BlockSpec((tm, LANES), lambda i: (i + r0, 0)),
            pl.BlockSpec((tm, LANES), lambda i: (i + r0, 0)),
        ],
        out_specs=pl.BlockSpec((MLA_HEADS, tm, MLA_QK), lambda i: (0, i, 0)),
        out_shape=jax.ShapeDtypeStruct((MLA_HEADS, nrows, MLA_QK), BF16),
        compiler_params=_params(("arbitrary",)),
        name="mla_q",
    )(z_mla, q_norm.reshape(1, Q_LORA), wq_ext, cos2, sin2)


def _mla_kv_kernel(z_ref, kvn_ref, w_ref, kr_ref, kt_ref, cos_ref, sin_ref, k_ref, v_ref):
    nh = MLA_HEADS
    xn = _rms(z_ref[...].astype(F32), kvn_ref[...]).astype(BF16)
    kv = _dot(xn, w_ref[...])
    r = MLA_ROPE
    kr = (kr_ref[:, 0:r] * cos_ref[:, 0:r] + kt_ref[:, 0:r] * sin_ref[:, 0:r]).astype(k_ref.dtype)
    per = MLA_NOPE + MLA_V
    ones_col = (lax.broadcasted_iota(jnp.int32, (kv.shape[0], MLA_VE - MLA_V), 1) == 0).astype(v_ref.dtype)
    for h in range(nh):
        k_ref[h, :, 0:MLA_NOPE] = kv[:, h * per:h * per + MLA_NOPE].astype(k_ref.dtype)
        k_ref[h, :, MLA_NOPE:MLA_QK] = kr
        v_ref[h, :, 0:MLA_V] = kv[:, h * per + MLA_NOPE:(h + 1) * per].astype(v_ref.dtype)
        v_ref[h, :, MLA_V:MLA_VE] = ones_col


def mla_kv(z_mla, z_misc, kv_norm, w_ukv, cos2, sin2, row0, nrows):
    tm = _pick(nrows, ROW_TILE, BF16_SUBLANES)
    assert row0 % tm == 0
    r0 = row0 // tm
    return pl.pallas_call(
        _mla_kv_kernel,
        grid=(nrows // tm,),
        in_specs=[
            pl.BlockSpec((tm, KV_LORA), lambda i: (i + r0, 1)),
            pl.BlockSpec((1, KV_LORA), lambda i: (0, 0)),
            pl.BlockSpec(w_ukv.shape, lambda i: (0, 0)),
            pl.BlockSpec((tm, LANES), lambda i: (i + r0, 0)),
            pl.BlockSpec((tm, LANES), lambda i: (i + r0, 1)),
            pl.BlockSpec((tm, LANES), lambda i: (i + r0, 0)),
            pl.BlockSpec((tm, LANES), lambda i: (i + r0, 0)),
        ],
        out_specs=[
            pl.BlockSpec((MLA_HEADS, tm, MLA_QK), lambda i: (0, i, 0)),
            pl.BlockSpec((MLA_HEADS, tm, MLA_VE), lambda i: (0, i, 0)),
        ],
        out_shape=[jax.ShapeDtypeStruct((MLA_HEADS, nrows, MLA_QK), BF16),
                   jax.ShapeDtypeStruct((MLA_HEADS, nrows, MLA_VE), BF16)],
        compiler_params=_params(("arbitrary",)),
        name="mla_kv",
    )(z_mla, kv_norm.reshape(1, KV_LORA), w_ukv, z_misc, z_misc, cos2, sin2)


def _softmax_update(carry, q, k, v):
    m, acc = carry
    s = _dot_nt(q, k)
    m_new = jnp.maximum(m, jnp.max(s, axis=-1, keepdims=True))
    p = jnp.exp2(s - m_new)
    acc = jnp.exp2(m - m_new) * acc + _dot(p.astype(BF16), v)
    return m_new, acc


def _attn_kernel(q_ref, kc_ref, vc_ref, *rest, tk, nkl):
    if nkl:
        kl_ref, vl_ref, o_ref = rest
    else:
        (o_ref,) = rest
    q = q_ref[...]
    tq = q.shape[0]
    init = (jnp.full((tq, 1), -jnp.inf, F32), jnp.zeros((tq, MLA_VE), F32))
    carry = _softmax_update(init, q, kc_ref[...], vc_ref[...])
    if nkl:
        def body(j, c):
            off = pl.multiple_of(j * tk, tk)
            return _softmax_update(c, q, kl_ref[pl.ds(off, tk), :], vl_ref[pl.ds(off, tk), :])
        carry = lax.fori_loop(0, nkl, body, carry, unroll=True)
    _, acc = carry
    o_ref[...] = (acc[:, 0:MLA_V] / acc[:, MLA_V:MLA_V + 1]).astype(o_ref.dtype)


def attention(q, kc, vc, kl, vl, nq_per_batch, lay):
    nb, c, s = lay["B"], lay["C"], lay["S"]
    tq = _pick(nq_per_batch, ATTN_Q_TILE, BF16_SUBLANES)
    nqb = nq_per_batch // tq
    in_specs = [
        pl.BlockSpec((None, tq, MLA_QK), lambda b, h, i: (h, b * nqb + i, 0)),
        pl.BlockSpec((None, c, MLA_QK), lambda b, h, i: (h, b, 0)),
        pl.BlockSpec((None, c, MLA_VE), lambda b, h, i: (h, b, 0)),
    ]
    args = [q, kc, vc]
    tk = 0
    nkl = 0
    if kl is not None:
        tk = _pick(s, ATTN_KV_TILE, BF16_SUBLANES)
        nkl = s // tk
        in_specs += [
            pl.BlockSpec((None, s, MLA_QK), lambda b, h, i: (h, b, 0)),
            pl.BlockSpec((None, s, MLA_VE), lambda b, h, i: (h, b, 0)),
        ]
        args += [kl, vl]
    return pl.pallas_call(
        functools.partial(_attn_kernel, tk=tk, nkl=nkl),
        grid=(nb, MLA_HEADS, nqb),
        in_specs=in_specs,
        out_specs=pl.BlockSpec((tq, MLA_V), lambda b, h, i: (b * nqb + i, h)),
        out_shape=jax.ShapeDtypeStruct((nb * nq_per_batch, MLA_HEADS * MLA_V), BF16),
        compiler_params=_params(("arbitrary", "arbitrary", "arbitrary")),
        name="attention_lat" if kl is not None else "attention_ctx",
    )(*args)


def _merge1_kernel(bra_ref, of_ref, ob_ref, gate_ref, attc_ref, attl_ref, g0_ref, g1_ref, g2_ref, dnn_ref, wb_ref,
                   y_ref, brb_ref, *, nctx_blk):
    hd = DN_HEAD_DIM
    att = jnp.where(pl.program_id(1) < nctx_blk, attc_ref[...], attl_ref[...])
    o = of_ref[...] + ob_ref[...]
    for h in range(DN_HEADS):
        cs = slice(h * hd, (h + 1) * hd)
        oh = o[:, cs]
        yn = oh * lax.rsqrt(jnp.mean(oh * oh, axis=-1, keepdims=True) + EPS) * dnn_ref[:, cs]
        brb_ref[:, cs] = (yn * _silu(gate_ref[:, cs].astype(F32))).astype(brb_ref.dtype)
    y = jax.nn.sigmoid(g0_ref[...].astype(F32)) * _dot(bra_ref[...], wb_ref[0])
    y = y + jax.nn.sigmoid(g1_ref[...].astype(F32)) * _dot(brb_ref[...], wb_ref[1])
    y = y + jax.nn.sigmoid(g2_ref[...].astype(F32)) * _dot(att, wb_ref[2])
    y_ref[...] = y.astype(y_ref.dtype)


def merge_branches(br_a, o_f, o_b, z_main, attn_c, attn_l, dn_norm, w_branch, lay):
    t = br_a.shape[0]
    d = lay["D"]
    w = BRANCH_W
    tm = _pick(lay["n_ctx"], ROW_TILE, BF16_SUBLANES)
    tn = _pick(d, COL_TILE)
    nn = d // tn
    gate0 = 6 * w // tn
    nctx_blk = lay["n_ctx"] // tm

    def gspec(j):
        return pl.BlockSpec((tm, tn), lambda n, i: (i, gate0 + j * nn + n))

    row = lambda n, i: (i, 0)
    return pl.pallas_call(
        functools.partial(_merge1_kernel, nctx_blk=nctx_blk),
        grid=(nn, t // tm),
        in_specs=[
            pl.BlockSpec((tm, w), row),
            pl.BlockSpec((tm, w), row),
            pl.BlockSpec((tm, w), row),
            pl.BlockSpec((tm, w), lambda n, i: (i, 5)),
            pl.BlockSpec((tm, w), lambda n, i: (jnp.minimum(i, nctx_blk - 1), 0)),
            pl.BlockSpec((tm, w), lambda n, i: (jnp.maximum(i - nctx_blk, 0), 0)),
            gspec(0), gspec(1), gspec(2),
            pl.BlockSpec((1, w), lambda n, i: (0, 0)),
            pl.BlockSpec((3, w, tn), lambda n, i: (0, 0, n)),
        ],
        out_specs=pl.BlockSpec((tm, tn), lambda n, i: (i, n)),
        out_shape=jax.ShapeDtypeStruct((t, d), BF16),
        scratch_shapes=[pltpu.VMEM((tm, w), BF16)],
        compiler_params=_params(("arbitrary", "arbitrary")),
        name="merge_branches",
    )(br_a, o_f, o_b, z_main, attn_c, attn_l, z_main, z_main, z_main,
      jnp.tile(dn_norm.astype(F32), DN_HEADS).reshape(1, w), w_branch)


def _merge2_kernel(y_ref, wo_ref, x_ref, mod_ref, n2_ref, wrh_ref, wrl_ref, xo_ref, h2_ref, lg_ref, *, d):
    m = mod_ref[...]
    mix = _dot(y_ref[...], wo_ref[...])
    xn = x_ref[...] + _mod_part(m, 2, d) * mix
    xo_ref[...] = xn
    h2 = _rms(xn, n2_ref[...]) * (1.0 + _mod_part(m, 4, d)) + _mod_part(m, 3, d)
    h2_ref[...] = h2
    hh = h2.astype(BF16)
    hl = (h2 - hh.astype(F32)).astype(BF16)
    lg_ref[...] = _dot(hh, wrh_ref[...]) + (_dot(hl, wrh_ref[...]) + _dot(hh, wrl_ref[...]))


def out_proj_residual(y, w_out, x, mod, layer, norm2, w_router_pad, lay):
    t, d = x.shape
    tm = _pick(lay["n_ctx"], RES_ROW_TILE, BF16_SUBLANES)
    row = lambda i: (i, 0)
    const = lambda i: (0, 0)
    wr_hi = w_router_pad.astype(BF16)
    wr_lo = (w_router_pad - wr_hi.astype(F32)).astype(BF16)
    return pl.pallas_call(
        functools.partial(_merge2_kernel, d=d),
        grid=(t // tm,),
        in_specs=[
            pl.BlockSpec((tm, d), row),
            pl.BlockSpec((d, d), const),
            pl.BlockSpec((tm, d), row),
            _mod_spec(layer, tm, lay),
            pl.BlockSpec((1, d), const),
            pl.BlockSpec((d, LANES), const),
            pl.BlockSpec((d, LANES), const),
        ],
        out_specs=[pl.BlockSpec((tm, d), row), pl.BlockSpec((tm, d), row), pl.BlockSpec((tm, LANES), row)],
        out_shape=[jax.ShapeDtypeStruct((t, d), F32), jax.ShapeDtypeStruct((t, d), F32),
                   jax.ShapeDtypeStruct((t, LANES), F32)],
        compiler_params=_params(("arbitrary",)),
        name="out_proj_residual",
    )(y, w_out, x, mod, norm2.reshape(1, d), wr_hi, wr_lo)


def _topk_kernel(lg_ref, bias_ref, tri_ref, idx_ref, wt_ref, rank_ref, cnt_ref, run_ref):
    ne, ng = N_EXPERTS, N_GROUPS
    per = ne // ng
    lt = lg_ref[...].T
    tm = lt.shape[1]
    sc = jax.nn.sigmoid(lt[0:ne])
    ch = sc + bias_ref[...]
    ch3 = ch.reshape(ng, per, tm)
    neg = -jnp.inf
    sub = lax.broadcasted_iota(jnp.int32, (ng, per, tm), 1)
    m1 = jnp.max(ch3, axis=1, keepdims=True)
    i1 = jnp.min(jnp.where(ch3 == m1, sub, per), axis=1, keepdims=True)
    m2 = jnp.max(jnp.where(sub == i1, neg, ch3), axis=1, keepdims=True)
    gs = (m1 + m2).reshape(ng, tm)
    giota = lax.broadcasted_iota(jnp.int32, (ng, tm), 0)
    sel = jnp.zeros((ng, tm), F32)
    cur = gs
    for _ in range(TOPK_GROUPS):
        m = jnp.max(cur, axis=0, keepdims=True)
        ix = jnp.min(jnp.where(cur == m, giota, ng), axis=0, keepdims=True)
        hit = giota == ix
        sel = jnp.where(hit, 1.0, sel)
        cur = jnp.where(hit, neg, cur)
    masked = jnp.where(sel.reshape(ng, 1, tm) > 0, ch3, neg).reshape(ne, tm)
    eiota = lax.broadcasted_iota(jnp.int32, (ne, tm), 0)
    idxs, ws, hits = [], [], []
    for _ in range(TOP_K):
        m = jnp.max(masked, axis=0, keepdims=True)
        ix = jnp.min(jnp.where(masked == m, eiota, ne), axis=0, keepdims=True)
        hit = eiota == ix
        ws.append(jnp.sum(jnp.where(hit, sc, 0.0), axis=0, keepdims=True))
        idxs.append(ix)
        hits.append(hit)
        masked = jnp.where(hit, neg, masked)
    wall = jnp.concatenate(ws, axis=0)
    idx_ref[...] = jnp.concatenate(idxs, axis=0)
    wt_ref[...] = wall / jnp.sum(wall, axis=0, keepdims=True) * ROUTED_SCALE

    @pl.when(pl.program_id(0) == 0)
    def _():
        run_ref[...] = jnp.zeros(run_ref.shape, F32)

    chosen = hits[0].astype(F32)
    for hit in hits[1:]:
        chosen = chosen + hit.astype(F32)
    before = run_ref[:, 0:1] + _dot(chosen.astype(BF16), tri_ref[...])
    ranks = [jnp.sum(jnp.where(hit, before, 0.0), axis=0, keepdims=True) for hit in hits]
    rank_ref[...] = jnp.concatenate(ranks, axis=0).astype(jnp.int32)
    run_ref[...] = run_ref[...] + jnp.sum(chosen, axis=1, keepdims=True)
    cnt_ref[...] = run_ref[...]


def route_topk(logits, bias):
    t = logits.shape[0]
    tm = _pick(t, ROW_TILE)
    strict_upper = jnp.asarray(np.triu(np.ones((tm, tm), np.float32), 1), BF16)
    col = lambda i: (0, i)
    idx_t, wt_t, rank_t, cnt = pl.pallas_call(
        _topk_kernel,
        grid=(t // tm,),
        in_specs=[pl.BlockSpec((tm, LANES), lambda i: (i, 0)), pl.BlockSpec((N_EXPERTS, 1), lambda i: (0, 0)),
                  pl.BlockSpec((tm, tm), lambda i: (0, 0))],
        out_specs=[pl.BlockSpec((TOP_K, tm), col), pl.BlockSpec((TOP_K, tm), col), pl.BlockSpec((TOP_K, tm), col),
                   pl.BlockSpec((N_EXPERTS, LANES), lambda i: (0, 0))],
        out_shape=[jax.ShapeDtypeStruct((TOP_K, t), jnp.int32), jax.ShapeDtypeStruct((TOP_K, t), F32),
                   jax.ShapeDtypeStruct((TOP_K, t), jnp.int32), jax.ShapeDtypeStruct((N_EXPERTS, LANES), F32)],
        scratch_shapes=[pltpu.VMEM((N_EXPERTS, LANES), F32)],
        compiler_params=_params(("arbitrary",)),
        name="route_topk",
    )(logits, bias.astype(F32).reshape(N_EXPERTS, 1), strict_upper)
    return idx_t, wt_t, rank_t, cnt[:, 0].astype(jnp.int32)


def moe_layout(idx_t, rank_t, counts, bm):
    k, t = idx_t.shape
    n_blocks = -(-(t * k) // bm) + N_EXPERTS
    padded = (counts + bm - 1) // bm * bm
    pend = jnp.cumsum(padded)
    pstart = pend - padded
    experts = jnp.arange(N_EXPERTS, dtype=jnp.int32)[:, None, None]
    pos_t = rank_t + jnp.sum(jnp.where(idx_t[None] == experts, pstart[:, None, None], 0), axis=0)
    n_used = (pend[-1] // bm).astype(jnp.int32)
    blk = jnp.minimum(jnp.arange(n_blocks, dtype=jnp.int32), n_used - 1) * bm
    blk_e = jnp.minimum(jnp.sum((pend[None, :] <= blk[:, None]).astype(jnp.int32), axis=1), N_EXPERTS - 1)
    bidx = jnp.arange(n_blocks, dtype=jnp.int32)
    last_of_expert = jnp.any((pend[None, :] == (bidx[:, None] + 1) * bm) & (counts[None, :] > 0), axis=1)
    zero_flag = (last_of_expert | (bidx >= n_used)).astype(jnp.int32)
    blk_e = blk_e.astype(jnp.int32)
    fresh = jnp.concatenate([jnp.ones((1,), jnp.int32), (blk_e[1:] != blk_e[:-1]).astype(jnp.int32)])
    run_par = (jnp.cumsum(fresh) - 1) % 2
    active = jnp.where(counts > 0, jnp.arange(N_EXPERTS, dtype=jnp.int32), N_EXPERTS)
    later = lax.cummin(jnp.concatenate([active[1:], jnp.full((1,), N_EXPERTS, jnp.int32)]), reverse=True)
    next_e = jnp.where(later < N_EXPERTS, later, -1)[blk_e]
    pos_flat = pos_t.astype(jnp.int32).T.reshape(-1)
    return pos_flat, blk_e, n_used.reshape(1), run_par.astype(jnp.int32), next_e.astype(jnp.int32), zero_flag


def _pack_pairs(x):
    n = x.shape[1] // 2
    lo = lax.bitcast_convert_type(x[:, :n].astype(BF16).astype(F32), jnp.uint32)
    hi = lax.bitcast_convert_type(x[:, n:].astype(BF16).astype(F32), jnp.uint32)
    return (lo >> 16) | (hi & jnp.uint32(0xFFFF0000))


def _unpack_pairs(w):
    lo = lax.bitcast_convert_type(w << 16, F32)
    hi = lax.bitcast_convert_type(w & jnp.uint32(0xFFFF0000), F32)
    return lo, hi


def _row_copy(src_ref, src_row, dst_ref, dst_row, sem):
    return pltpu.make_async_copy(src_ref.at[pl.ds(src_row, 1)], dst_ref.at[pl.ds(dst_row, 1)], sem)


def _dispatch_kernel(pos_ref, zf_ref, h_ref, xs_ref, buf_ref, zero_ref, sem_ref, zsem_ref, *, tm, bm, n_blocks,
                     nsteps):
    i = pl.program_id(0)
    slot = i % 2

    def wait_slot(s):
        for _ in range(TOP_K):
            pltpu.make_async_copy(buf_ref.at[s], xs_ref.at[pl.ds(0, tm)], sem_ref.at[s]).wait()

    @pl.when(i == 0)
    def _():
        zero_ref[...] = jnp.zeros(zero_ref.shape, zero_ref.dtype)

        def zero_copy(b):
            return pltpu.make_async_copy(zero_ref, xs_ref.at[pl.ds(pl.multiple_of(b * bm, bm), bm)], zsem_ref.at[0])

        def start(b, carry):
            @pl.when(zf_ref[b] != 0)
            def _():
                zero_copy(b).start()
            return carry

        def wait(b, carry):
            @pl.when(zf_ref[b] != 0)
            def _():
                zero_copy(b).wait()
            return carry

        lax.fori_loop(0, n_blocks, start, 0)
        lax.fori_loop(0, n_blocks, wait, 0)

    @pl.when(i >= 2)
    def _():
        wait_slot(slot)

    buf_ref[slot] = _pack_pairs(h_ref[...])

    def body(tok, carry):
        for k in range(TOP_K):
            _row_copy(buf_ref.at[slot], tok, xs_ref, pos_ref[tok * TOP_K + k], sem_ref.at[slot]).start(
                priority=k % 2)
        return carry

    lax.fori_loop(0, tm, body, 0)

    @pl.when(i == nsteps - 1)
    def _():
        if nsteps > 1:
            wait_slot(1 - slot)
        wait_slot(slot)


def moe_dispatch(h2, pos_t, zero_flag, bm):
    t, d = h2.shape
    tm = _pick(t, MOE_TOKEN_TILE)
    nsteps = t // tm
    n_blocks = zero_flag.shape[0]
    return pl.pallas_call(
        functools.partial(_dispatch_kernel, tm=tm, bm=bm, n_blocks=n_blocks, nsteps=nsteps),
        grid=(nsteps,),
        in_specs=[
            pl.BlockSpec((tm * TOP_K,), lambda i: (i,), memory_space=pltpu.SMEM),
            pl.BlockSpec(memory_space=pltpu.SMEM),
            pl.BlockSpec((tm, d), lambda i: (i, 0)),
        ],
        out_specs=pl.BlockSpec(memory_space=pl.ANY),
        out_shape=jax.ShapeDtypeStruct((n_blocks * bm, d // 2), jnp.uint32),
        scratch_shapes=[pltpu.VMEM((2, tm, d // 2), jnp.uint32), pltpu.VMEM((bm, d // 2), jnp.uint32),
                        pltpu.SemaphoreType.DMA((2,)), pltpu.SemaphoreType.DMA((1,))],
        compiler_params=_params(("arbitrary",)),
        name="moe_dispatch",
    )(pos_t, zero_flag, h2)


def _moe_kernel(be_ref, nu_ref, par_ref, nx_ref, x_ref, w1_hbm, w3_hbm, w2_hbm, y_ref, f1_s, f3_s, f2_s, w13_s, w2_s,
                sem_ref, *, ed, half, layer):
    i = pl.program_id(0)
    prev = be_ref[jnp.maximum(i - 1, 0)]
    fresh = jnp.logical_or(i == 0, be_ref[i] != prev)
    slot = par_ref[i]

    def weight_copies(e, s):
        return (pltpu.make_async_copy(w1_hbm.at[layer, e], f1_s.at[s], sem_ref.at[s, 0]),
                pltpu.make_async_copy(w3_hbm.at[layer, e], f3_s.at[s], sem_ref.at[s, 1]),
                pltpu.make_async_copy(w2_hbm.at[layer, e], f2_s.at[s], sem_ref.at[s, 2]))

    @pl.when(i == 0)
    def _():
        for cp in weight_copies(be_ref[0], slot):
            cp.start()

    @pl.when(fresh)
    def _():
        for cp in weight_copies(be_ref[i], slot):
            cp.wait()

        @pl.when(nx_ref[i] >= 0)
        def _():
            for cp in weight_copies(nx_ref[i], 1 - slot):
                cp.start()

        w13_s[:, 0:ed] = f1_s[slot].astype(BF16)
        w13_s[:, ed:2 * ed] = f3_s[slot].astype(BF16)
        w2_s[...] = f2_s[slot].astype(BF16)

    @pl.when(i < nu_ref[0])
    def _():
        lo, hi = _unpack_pairs(x_ref[...])
        a = _dot(lo.astype(BF16), w13_s[0:half, :]) + _dot(hi.astype(BF16), w13_s[half:2 * half, :])
        act = (_silu(a[:, 0:ed]) * a[:, ed:2 * ed]).astype(BF16)
        y_ref[...] = _pack_pairs(_dot(act, w2_s[...]))

    @pl.when(i >= nu_ref[0])
    def _():
        y_ref[...] = jnp.zeros(y_ref.shape, y_ref.dtype)


def moe_experts(xs, blk_e, n_used, run_par, next_e, w1, w3, w2, layer, bm):
    n_rows, half = xs.shape
    d = 2 * half
    ed = w1.shape[-1]
    n_blocks = n_rows // bm
    any_spec = pl.BlockSpec(memory_space=pl.ANY)
    grid_spec = pltpu.PrefetchScalarGridSpec(
        num_scalar_prefetch=4,
        grid=(n_blocks,),
        in_specs=[pl.BlockSpec((bm, half), lambda i, *_: (i, 0)), any_spec, any_spec, any_spec],
        out_specs=pl.BlockSpec((bm, half), lambda i, *_: (i, 0)),
        scratch_shapes=[pltpu.VMEM((2, d, ed), F32), pltpu.VMEM((2, d, ed), F32), pltpu.VMEM((2, ed, d), F32),
                        pltpu.VMEM((d, 2 * ed), BF16), pltpu.VMEM((ed, d), BF16),
                        pltpu.SemaphoreType.DMA((2, 3))],
    )
    return pl.pallas_call(
        functools.partial(_moe_kernel, ed=ed, half=half, layer=layer),
        grid_spec=grid_spec,
        out_shape=jax.ShapeDtypeStruct(xs.shape, jnp.uint32),
        compiler_params=_params(("arbitrary",)),
        name="moe_experts",
    )(blk_e, n_used, run_par, next_e, xs, w1, w3, w2)


def _shared_kernel(pos_ref, posn_ref, wt_ref, h2_ref, w13_ref, w2_ref, ys_ref, x_ref, mod_ref, ng_ref, nmod_ref,
                   xo_ref, ho_ref, ybuf_ref, sem_ref, *, d, ed, tm, nsteps, final):
    i = pl.program_id(0)
    slot = i % 2

    def issue(p_ref, s):
        def body(tok, carry):
            for k in range(TOP_K):
                _row_copy(ys_ref, p_ref[tok * TOP_K + k], ybuf_ref.at[s, k], tok, sem_ref.at[s]).start(
                    priority=k % 2)
            return carry
        lax.fori_loop(0, tm, body, 0)

    @pl.when(i == 0)
    def _():
        issue(pos_ref, 0)

    @pl.when(i + 1 < nsteps)
    def _():
        issue(posn_ref, 1 - slot)

    m = mod_ref[...]
    a = _dot(h2_ref[...].astype(BF16), w13_ref[...])
    act = (_silu(a[:, 0:ed]) * a[:, ed:2 * ed]).astype(BF16)
    f = _dot(act, w2_ref[...])

    for k in range(TOP_K):
        pltpu.make_async_copy(ys_ref.at[pl.ds(0, tm)], ybuf_ref.at[slot, k], sem_ref.at[slot]).wait()
    half = d // 2
    wks = [wt_ref[:, k:k + 1] for k in range(TOP_K)]
    r_lo, r_hi = [], []
    for c in range(half // LANES):
        cs = slice(c * LANES, (c + 1) * LANES)
        acc_lo = acc_hi = None
        for k in range(TOP_K):
            lo, hi = _unpack_pairs(ybuf_ref[slot, k, :, cs])
            acc_lo = wks[k] * lo if acc_lo is None else acc_lo + wks[k] * lo
            acc_hi = wks[k] * hi if acc_hi is None else acc_hi + wks[k] * hi
        r_lo.append(acc_lo)
        r_hi.append(acc_hi)
    f = f + jnp.concatenate(r_lo + r_hi, axis=1)
    xn = x_ref[...] + _mod_part(m, 5, d) * f
    xo_ref[...] = xn
    y = _rms(xn, ng_ref[...])
    if not final:
        nm = nmod_ref[...]
        y = y * (1.0 + _mod_part(nm, 1, d)) + _mod_part(nm, 0, d)
    ho_ref[...] = y.astype(ho_ref.dtype)


def shared_residual(h2, sw13, sw2, ys, pos_t, wts, x, mod, layer, next_g, final, lay):
    t, d = x.shape
    ed = sw2.shape[0]
    tm = _pick(lay["n_ctx"], RES_ROW_TILE, BF16_SUBLANES)
    r0 = lay["n_ctx"] // tm if final else 0
    nsteps = t // tm - r0
    row = lambda i: (i + r0, 0)
    out_row = lambda i: (i, 0)
    const = lambda i: (0, 0)
    next_layer = layer if final else layer + 1
    n_out = nsteps * tm
    return pl.pallas_call(
        functools.partial(_shared_kernel, d=d, ed=ed, tm=tm, nsteps=nsteps, final=final),
        grid=(nsteps,),
        in_specs=[
            pl.BlockSpec((tm * TOP_K,), lambda i: (i + r0,), memory_space=pltpu.SMEM),
            pl.BlockSpec((tm * TOP_K,), lambda i: (jnp.minimum(i + 1, nsteps - 1) + r0,), memory_space=pltpu.SMEM),
            pl.BlockSpec((tm, TOP_K), row),
            pl.BlockSpec((tm, d), row),
            pl.BlockSpec((d, 2 * ed), const),
            pl.BlockSpec((ed, d), const),
            pl.BlockSpec(memory_space=pl.ANY),
            pl.BlockSpec((tm, d), row),
            _mod_spec(layer, tm, lay, r0),
            pl.BlockSpec((1, d), const),
            _mod_spec(next_layer, tm, lay, r0),
        ],
        out_specs=[pl.BlockSpec((tm, d), out_row), pl.BlockSpec((tm, d), out_row)],
        out_shape=[jax.ShapeDtypeStruct((n_out, d), F32), jax.ShapeDtypeStruct((n_out, d), F32 if final else BF16)],
        scratch_shapes=[pltpu.VMEM((2, TOP_K, tm, d // 2), jnp.uint32), pltpu.SemaphoreType.DMA((2,))],
        compiler_params=_params(("arbitrary",)),
        name="shared_residual",
    )(pos_t, pos_t, wts, h2, sw13, sw2, ys, x, mod, next_g.reshape(1, d), mod)


def _rope_tables(lay):
    half = MLA_ROPE // 4
    s = lay["S"]
    freq = ROPE_BASE ** (-np.arange(half, dtype=np.float64) / half)
    tpos = np.arange(s)
    ang_r = (tpos // GRID_W)[:, None] * freq
    ang_c = (tpos % GRID_W)[:, None] * freq
    ang = np.concatenate([ang_r, ang_r, ang_c, ang_c], axis=1)
    cos_l = np.tile(np.cos(ang), (lay["B"], 2))
    sin_l = np.tile(np.sin(ang), (lay["B"], 2))
    cos = np.concatenate([np.ones((lay["n_ctx"], 2 * MLA_ROPE)), cos_l], axis=0).astype(np.float32)
    sin = np.concatenate([np.zeros((lay["n_ctx"], 2 * MLA_ROPE)), sin_l], axis=0).astype(np.float32)
    return jnp.asarray(cos), jnp.asarray(sin)


def _rot_cols(w):
    q = MLA_ROPE // 4
    a, b, c, e = w[..., 0:q], w[..., q:2 * q], w[..., 2 * q:3 * q], w[..., 3 * q:4 * q]
    return jnp.concatenate([-b, a, -e, c], axis=-1)


def _split_w_in_t_kernel(x_ref, main_ref, mla_ref, small_ref, *, o_small, o_cq, o_kr, o_gate, n_gate):
    ch = 2 * LANES
    for a in range(0, o_small, ch):
        main_ref[:, a:a + ch] = x_ref[a:a + ch, :].T.astype(main_ref.dtype)
    for a in range(0, n_gate, ch):
        main_ref[:, o_small + a:o_small + a + ch] = x_ref[o_gate + a:o_gate + a + ch, :].T.astype(main_ref.dtype)
    for a in range(0, o_kr - o_cq, ch):
        mla_ref[:, a:a + ch] = x_ref[o_cq + a:o_cq + a + ch, :].T.astype(mla_ref.dtype)
    n_ab = o_cq - o_small
    w0 = o_kr - n_ab
    t_ab = x_ref[o_small:o_small + LANES, :].T
    t_kr = x_ref[w0:w0 + LANES, :].T
    lane = lax.broadcasted_iota(jnp.int32, t_ab.shape, 1)
    small_ref[...] = jnp.where(lane < n_ab, t_ab, jnp.where(lane < n_ab + MLA_ROPE, t_kr, 0.0))


def split_w_in(w_in_all, layer, offs):
    o_small, o_cq, o_kr, o_gate = offs
    _, d, d_in = w_in_all.shape
    n_gate = d_in - o_gate
    tk = _pick(d, 2 * LANES)
    row = lambda i: (i, 0)
    return pl.pallas_call(
        functools.partial(_split_w_in_t_kernel, o_small=o_small, o_cq=o_cq, o_kr=o_kr, o_gate=o_gate,
                          n_gate=n_gate),
        grid=(d // tk,),
        in_specs=[pl.BlockSpec((None, d_in, tk), lambda i: (layer, 0, i))],
        out_specs=[pl.BlockSpec((tk, o_small + n_gate), row), pl.BlockSpec((tk, o_kr - o_cq), row),
                   pl.BlockSpec((tk, LANES), row)],
        out_shape=[jax.ShapeDtypeStruct((d, o_small + n_gate), BF16), jax.ShapeDtypeStruct((d, o_kr - o_cq), BF16),
                   jax.ShapeDtypeStruct((d, LANES), F32)],
        compiler_params=_params(("arbitrary",)),
        name="split_w_in",
    )(jnp.swapaxes(w_in_all, 1, 2))


def _prep_layer_weights(w_in_all, layer, w_uq, w_ukv, w_branch, w_out, w_router, sw1, sw3, sw2, d):
    w = BRANCH_W
    o_small = 6 * w
    o_cq = o_small + 4 * DN_HEADS
    o_kr = o_cq + Q_LORA + KV_LORA
    o_gate = o_kr + MLA_ROPE
    w_main, w_mla, w_small = split_w_in(w_in_all, layer, (o_small, o_cq, o_kr, o_gate))
    n_ab = o_cq - o_small
    w_kr = w_small[:, n_ab:n_ab + MLA_ROPE]
    zpad = jnp.zeros((d, LANES - MLA_ROPE), F32)
    w_misc = jnp.concatenate(
        [w_kr, zpad, _rot_cols(w_kr), zpad, w_small[:, 0:n_ab], jnp.zeros((d, LANES - n_ab), F32)],
        axis=1).astype(BF16)
    uq = w_uq.reshape(Q_LORA, MLA_HEADS, MLA_QK)
    uq_r = uq[..., MLA_NOPE:]
    wq_ext = jnp.concatenate(
        [uq[..., :MLA_NOPE].reshape(Q_LORA, -1), uq_r.reshape(Q_LORA, -1), _rot_cols(uq_r).reshape(Q_LORA, -1)],
        axis=1).astype(BF16)
    wr_pad = jnp.zeros((d, LANES), F32).at[:, :N_EXPERTS].set(w_router.astype(F32))
    return dict(w_main=w_main, w_mla=w_mla, w_misc=w_misc, wq_ext=wq_ext, w_ukv=w_ukv.astype(BF16),
                w_branch=w_branch.astype(BF16), w_out=w_out.astype(BF16), wr_pad=wr_pad,
                sw13=jnp.concatenate([sw1, sw3], axis=1).astype(BF16), sw2=sw2.astype(BF16))


def kernel(x, c, ctx, c_ctx, w_mod, b_mod, norm1, norm2, w_in, gm_norm, gm_ws, gm_bs, dn_conv, dn_a_log,
           dn_dt_bias, dn_norm, mla_q_norm, mla_kv_norm, mla_w_uq, mla_w_ukv, w_branch, w_out, moe_router,
           moe_bias, moe_w1, moe_w3, moe_w2, shared_w1, shared_w3, shared_w2, final_norm):
    nb, s, d = x.shape
    cl = ctx.shape[1]
    depth = w_mod.shape[0]
    n_ctx = nb * cl
    t = n_ctx + nb * s
    lay = dict(B=nb, S=s, C=cl, D=d, n_ctx=n_ctx, T=t)
    assert nb + 1 <= SUBLANES and s % GRID_W == 0

    xs = jnp.concatenate([ctx.reshape(n_ctx, d), x.reshape(nb * s, d)], axis=0).astype(F32)
    cvec = jnp.zeros((SUBLANES, d), F32).at[0].set(c_ctx.astype(F32)).at[1:1 + nb].set(c.astype(F32))
    mod = modulation(cvec, w_mod, b_mod)
    cos2, sin2 = _rope_tables(lay)
    scale = MLA_QK ** -0.5 * LOG2E

    h = prenorm(xs, norm1[0], mod, 0, lay)
    out = None
    for i in range(depth):
        last = i == depth - 1
        wts = _prep_layer_weights(w_in, i, mla_w_uq[i], mla_w_ukv[i], w_branch[i], w_out[i], moe_router[i],
                                  shared_w1[i], shared_w3[i], shared_w2[i], d)
        z_main = matmul(h, wts["w_main"], BF16, "in_proj_main")
        z_mla = matmul(h, wts["w_mla"], BF16, "in_proj_mla")
        z_misc = matmul(h, wts["w_misc"], F32, "in_proj_misc")

        br_a = gmlp(z_main, gm_norm[i], gm_ws[i], gm_bs[i], lay)

        qkv_c = dn_short_conv(z_main, dn_conv[i], lay)
        o_f, o_b = dn_scan(qkv_c, z_misc, dn_a_log[i], dn_dt_bias[i], lay)

        kc, vc = mla_kv(z_mla, z_misc, mla_kv_norm[i], wts["w_ukv"], cos2, sin2, 0, n_ctx)
        kl, vl = mla_kv(z_mla, z_misc, mla_kv_norm[i], wts["w_ukv"], cos2, sin2, n_ctx, nb * s)
        ql = mla_q(z_mla, mla_q_norm[i], wts["wq_ext"], cos2, sin2, n_ctx, nb * s, scale)
        attn_l = attention(ql, kc, vc, kl, vl, s, lay)
        qc = mla_q(z_mla, mla_q_norm[i], wts["wq_ext"], cos2, sin2, 0, n_ctx, scale)
        attn_c = attention(qc, kc, vc, None, None, cl, lay)

        y = merge_branches(br_a, o_f, o_b, z_main, attn_c, attn_l, dn_norm[i], wts["w_branch"], lay)
        xs, h2, logits = out_proj_residual(y, wts["w_out"], xs, mod, i, norm2[i], wts["wr_pad"], lay)

        idx_t, wt_t, rank_t, counts = route_topk(logits, moe_bias[i])
        bm = MOE_ROW_TILE
        pos_t, blk_e, n_used, run_par, next_e, zero_flag = moe_layout(idx_t, rank_t, counts, bm)
        x_sorted = moe_dispatch(h2, pos_t, zero_flag, bm)
        y_sorted = moe_experts(x_sorted, blk_e, n_used, run_par, next_e, moe_w1, moe_w3, moe_w2, i, bm)

        next_g = final_norm if last else norm1[i + 1]
        xs, h = shared_residual(h2, wts["sw13"], wts["sw2"], y_sorted, pos_t, wt_t.T, xs, mod, i, next_g, last, lay)
        out = h
    return out.reshape(nb, s, d).astype(x.dtype)
```

```python
import functools
import math

import numpy as np
import jax
import jax.numpy as jnp
from jax import lax
from jax.experimental import pallas as pl
from jax.experimental.pallas import tpu as pltpu

F32 = jnp.float32
BF16 = jnp.bfloat16

GRID_W = 64
EPS = 1e-6
GM_CHUNK = 128
GM_GROUPS = 8
DN_HEADS = 8
DN_HEAD_DIM = 128
DN_CHUNK = 64
DN_CONV = 5
MLA_HEADS = 8
MLA_NOPE = 128
MLA_ROPE = 64
MLA_V = 128
MLA_QK = MLA_NOPE + MLA_ROPE
MLA_VE = 2 * MLA_V
Q_LORA = 512
KV_LORA = 512
ROPE_BASE = 10000.0
BRANCH_W = 1024
N_EXPERTS = 64
TOP_K = 8
N_GROUPS = 8
TOPK_GROUPS = 4
ROUTED_SCALE = 2.5
LOG2E = 1.4426950408889634

LANES = 128
SUBLANES = 8
BF16_SUBLANES = 16
VMEM_LIMIT_MB = 56

ROW_TILE = 512
WIDE_ROW_TILE = 1408
RES_ROW_TILE = 256
CONV_ROW_TILE = 256
COL_TILE = 1024
MOE_ROW_TILE = 512
MOE_TOKEN_TILE = 256
ATTN_Q_TILE = 1024
ATTN_KV_TILE = 1024


def _pick(n, pref, mult=LANES):
    if n <= pref:
        return n
    for t in range(pref - pref % mult, 0, -mult):
        if n % t == 0:
            return t
    return n


def _params(sem, mb=VMEM_LIMIT_MB):
    return pltpu.CompilerParams(dimension_semantics=sem, vmem_limit_bytes=mb * 1024 * 1024)


def _silu(x):
    return x * jax.nn.sigmoid(x)


def _gelu(x):
    return 0.5 * x * (1.0 + jnp.tanh(0.7978845608028654 * (x + 0.044715 * x * x * x)))


def _dot(a, b):
    return jnp.dot(a, b, preferred_element_type=F32)


def _dot_nt(a, b):
    return lax.dot_general(a, b, (((1,), (1,)), ((), ())), preferred_element_type=F32)


def _dot_tn(a, b):
    return lax.dot_general(a, b, (((0,), (0,)), ((), ())), preferred_element_type=F32)


def _mod_kernel(c_ref, w_ref, b_ref, o_ref):
    s = _silu(c_ref[...])
    o_ref[...] = _dot(s.astype(BF16), w_ref[...].astype(BF16)) + b_ref[...]


def modulation(cvec, w_mod, b_mod):
    nl, d, n6 = w_mod.shape
    tn = _pick(n6, COL_TILE)
    out = pl.pallas_call(
        _mod_kernel,
        grid=(nl, n6 // tn),
        in_specs=[
            pl.BlockSpec((SUBLANES, d), lambda l, n: (0, 0)),
            pl.BlockSpec((None, d, tn), lambda l, n: (l, 0, n)),
            pl.BlockSpec((None, 1, tn), lambda l, n: (l, 0, n)),
        ],
        out_specs=pl.BlockSpec((None, SUBLANES, tn), lambda l, n: (l, 0, n)),
        out_shape=jax.ShapeDtypeStruct((nl, SUBLANES, n6), F32),
        compiler_params=_params(("arbitrary", "arbitrary")),
        name="modulation",
    )(cvec, w_mod, b_mod.reshape(nl, 1, n6))
    return out.reshape(nl, SUBLANES, 1, n6)


def _seg_of_block(i, tm, lay):
    nctx_blk = lay["n_ctx"] // tm
    lat_bps = lay["S"] // tm
    return jnp.where(i < nctx_blk, 0, 1 + (i - nctx_blk) // lat_bps)


def _mod_spec(layer, tm, lay, row0_blk=0):
    n6 = 6 * lay["D"]
    return pl.BlockSpec((None, None, 1, n6), lambda i: (layer, _seg_of_block(i + row0_blk, tm, lay), 0, 0))


def _mod_part(m, k, d):
    return m[:, k * d:(k + 1) * d]


def _rms(x, g):
    return x * lax.rsqrt(jnp.mean(x * x, axis=-1, keepdims=True) + EPS) * g


def _prenorm_kernel(x_ref, g_ref, mod_ref, o_ref, *, d):
    m = mod_ref[...]
    y = _rms(x_ref[...], g_ref[...])
    o_ref[...] = (y * (1.0 + _mod_part(m, 1, d)) + _mod_part(m, 0, d)).astype(o_ref.dtype)


def prenorm(x, g, mod, layer, lay):
    t, d = x.shape
    tm = _pick(lay["n_ctx"], ROW_TILE, SUBLANES)
    return pl.pallas_call(
        functools.partial(_prenorm_kernel, d=d),
        grid=(t // tm,),
        in_specs=[
            pl.BlockSpec((tm, d), lambda i: (i, 0)),
            pl.BlockSpec((1, d), lambda i: (0, 0)),
            _mod_spec(layer, tm, lay),
        ],
        out_specs=pl.BlockSpec((tm, d), lambda i: (i, 0)),
        out_shape=jax.ShapeDtypeStruct((t, d), BF16),
        compiler_params=_params(("arbitrary",)),
        name="prenorm",
    )(x, g.reshape(1, d), mod)


def _mm_kernel(x_ref, w_ref, o_ref):
    o_ref[...] = _dot(x_ref[...], w_ref[...]).astype(o_ref.dtype)


def matmul(x, w, out_dtype, name, row_tile=ROW_TILE):
    t, k = x.shape
    n = w.shape[1]
    tm = _pick(t, row_tile, BF16_SUBLANES)
    tn = _pick(n, COL_TILE)
    return pl.pallas_call(
        _mm_kernel,
        grid=(n // tn, t // tm),
        in_specs=[
            pl.BlockSpec((tm, k), lambda j, i: (i, 0)),
            pl.BlockSpec((k, tn), lambda j, i: (0, j)),
        ],
        out_specs=pl.BlockSpec((tm, tn), lambda j, i: (i, j)),
        out_shape=jax.ShapeDtypeStruct((t, n), out_dtype),
        compiler_params=_params(("arbitrary", "arbitrary")),
        name=name,
    )(x, w)


def _gmlp_kernel(u_ref, v_ref, gn_ref, ws_ref, bias_ref, o_ref, *, nchunk):
    for j in range(nchunk):
        rs = slice(j * GM_CHUNK, (j + 1) * GM_CHUNK)
        v = _gelu(v_ref[rs, :].astype(F32))
        vb = _rms(v, gn_ref[...]).astype(BF16)
        u = _gelu(u_ref[rs, :].astype(F32))
        for g in range(GM_GROUPS):
            cs = slice(g * LANES, (g + 1) * LANES)
            mixed = _dot(ws_ref[g], vb[:, cs]) + bias_ref[:, cs]
            o_ref[rs, cs] = (u[:, cs] * mixed).astype(o_ref.dtype)


def gmlp(z_main, gm_norm, gm_ws, gm_bs, lay):
    t = z_main.shape[0]
    w = BRANCH_W
    tm = _pick(lay["n_ctx"], ROW_TILE, GM_CHUNK)
    bias = jnp.repeat(gm_bs.T.astype(F32), w // GM_GROUPS, axis=1)
    return pl.pallas_call(
        functools.partial(_gmlp_kernel, nchunk=tm // GM_CHUNK),
        grid=(t // tm,),
        in_specs=[
            pl.BlockSpec((tm, w), lambda i: (i, 0)),
            pl.BlockSpec((tm, w), lambda i: (i, 1)),
            pl.BlockSpec((1, w), lambda i: (0, 0)),
            pl.BlockSpec((GM_GROUPS, GM_CHUNK, GM_CHUNK), lambda i: (0, 0, 0)),
            pl.BlockSpec((GM_CHUNK, w), lambda i: (0, 0)),
        ],
        out_specs=pl.BlockSpec((tm, w), lambda i: (i, 0)),
        out_shape=jax.ShapeDtypeStruct((t, w), BF16),
        compiler_params=_params(("arbitrary",)),
        name="gmlp",
    )(z_main, z_main, gm_norm.reshape(1, w), gm_ws.astype(BF16), bias)


def _conv_kernel(x_ref, hp_ref, hn_ref, w_ref, o_ref, xe_ref, *, tm, nctx_blk, ctx_bps, lat_bps):
    j = pl.program_id(0)
    i = pl.program_id(1)
    li = i - nctx_blk
    seg_start = jnp.where(i < nctx_blk, (i % ctx_bps) == 0, (li % lat_bps) == 0)
    seg_end = jnp.where(i < nctx_blk, ((i + 1) % ctx_bps) == 0, ((li + 1) % lat_bps) == 0)
    halo = BF16_SUBLANES
    xe_ref[0:halo, :] = jnp.where(seg_start, 0.0, hp_ref[...].astype(F32))
    xe_ref[halo:halo + tm, :] = x_ref[...].astype(F32)
    xe_ref[halo + tm:2 * halo + tm, :] = jnp.where(seg_end, 0.0, hn_ref[...].astype(F32))
    base = halo - DN_CONV // 2
    acc = w_ref[0:1, :] * xe_ref[base:base + tm, :]
    for tap in range(1, DN_CONV):
        acc = acc + w_ref[tap:tap + 1, :] * xe_ref[base + tap:base + tap + tm, :]
    y = _silu(acc)
    unit = j < 2
    for h in range(DN_HEADS):
        cs = slice(h * DN_HEAD_DIM, (h + 1) * DN_HEAD_DIM)
        yh = y[:, cs]
        nrm = yh * lax.rsqrt(jnp.sum(yh * yh, axis=-1, keepdims=True) + EPS)
        o_ref[:, cs] = jnp.where(unit, nrm, yh).astype(o_ref.dtype)


def dn_short_conv(z_main, conv_w, lay):
    t = z_main.shape[0]
    w = BRANCH_W
    tm = _pick(lay["C"], CONV_ROW_TILE, BF16_SUBLANES)
    halo = BF16_SUBLANES
    hb = tm // halo
    nhalo = t // halo
    wpad = jnp.zeros((SUBLANES, 3 * w), F32).at[:DN_CONV].set(conv_w.astype(F32))
    kern = functools.partial(_conv_kernel, tm=tm, nctx_blk=lay["n_ctx"] // tm, ctx_bps=lay["C"] // tm,
                             lat_bps=lay["S"] // tm)
    return pl.pallas_call(
        kern,
        grid=(3, t // tm),
        in_specs=[
            pl.BlockSpec((tm, w), lambda j, i: (i, 2 + j)),
            pl.BlockSpec((halo, w), lambda j, i: (jnp.maximum(i * hb - 1, 0), 2 + j)),
            pl.BlockSpec((halo, w), lambda j, i: (jnp.minimum((i + 1) * hb, nhalo - 1), 2 + j)),
            pl.BlockSpec((SUBLANES, w), lambda j, i: (0, j)),
        ],
        out_specs=pl.BlockSpec((tm, w), lambda j, i: (i, j)),
        out_shape=jax.ShapeDtypeStruct((t, 3 * w), BF16),
        scratch_shapes=[pltpu.VMEM((tm + 2 * halo, w), F32)],
        compiler_params=_params(("arbitrary", "arbitrary")),
        name="dn_conv",
    )(z_main, z_main, z_main, wpad)


def _dn_masks():
    c = DN_CHUNK
    r = np.arange(c)[:, None]
    s = np.arange(c)[None, :]
    tri = np.stack([r >= s, r <= s]).astype(np.float32)
    strict = np.stack([r > s, r < s]).astype(np.float32)
    lv = []
    b = 1
    while b < c:
        same = (r // (2 * b)) == (s // (2 * b))
        lo = same & ((r // b) % 2 == 1) & ((s // b) % 2 == 0)
        up = same & ((r // b) % 2 == 0) & ((s // b) % 2 == 1)
        lv.append(np.stack([lo, up]))
        b *= 2
    lvl = np.stack(lv, axis=1).astype(np.float32)
    return tri, strict, lvl, np.eye(c, dtype=np.float32)


def _dn_kernel(qf_ref, qb_ref, abf_ref, abb_ref, alog_ref, dt_ref, tri_ref, strict_ref, lvl_ref, eye_ref,
               of_ref, ob_ref, s_ref, *, nlevels):
    step = pl.program_id(1)

    @pl.when(step == 0)
    def _():
        s_ref[...] = jnp.zeros(s_ref.shape, F32)

    hd = DN_HEAD_DIM
    w = DN_HEADS * hd
    nh = DN_HEADS
    cc = DN_CHUNK
    eye = eye_ref[...]
    inst = []
    for d, (x_ref, ab_ref, o_ref) in enumerate(((qf_ref, abf_ref, of_ref), (qb_ref, abb_ref, ob_ref))):
        ab = ab_ref[...]
        g_all = -jnp.exp(alog_ref[...]) * jax.nn.softplus(ab + dt_ref[...])
        beta_all = jax.nn.sigmoid(ab)
        gam_c = jnp.dot(tri_ref[d], g_all, precision=lax.Precision.HIGHEST, preferred_element_type=F32)
        gam_r = gam_c.T
        for h in range(nh):
            lane = d * nh + h
            inst.append(dict(
                d=d, lane=lane, h=h, slot=lane, o_ref=o_ref,
                q=x_ref[:, h * hd:(h + 1) * hd].astype(F32),
                k=x_ref[:, w + h * hd:w + (h + 1) * hd].astype(F32),
                v=x_ref[:, 2 * w + h * hd:2 * w + (h + 1) * hd].astype(F32),
                gc=gam_c[:, lane:lane + 1],
                gr=gam_r[lane:lane + 1, :],
                bc=beta_all[:, 2 * nh + lane:2 * nh + lane + 1],
                st=s_ref[lane]))
    for it in inst:
        tri = tri_ref[it["d"]]
        diff = it["gc"] - it["gr"]
        it["dec"] = jnp.where(tri > 0, jnp.exp(jnp.where(tri > 0, diff, 0.0)), 0.0)
        it["kb"] = it["k"] * it["bc"]
        it["qs"] = it["q"] * (hd ** -0.5)
    for it in inst:
        kq = _dot_nt(jnp.concatenate([it["kb"], it["qs"]], axis=0).astype(BF16), it["k"].astype(BF16))
        it["a"] = kq[:cc] * it["dec"] * strict_ref[it["d"]]
        it["attn"] = (kq[cc:] * it["dec"]).astype(BF16)
    for it in inst:
        it["x"] = eye - it["a"] * lvl_ref[it["d"], 0]
    for lv in range(1, nlevels):
        for it in inst:
            it["xb"] = it["x"].astype(BF16)
            it["p"] = _dot(it["xb"], (it["a"] * lvl_ref[it["d"], lv]).astype(BF16)).astype(BF16)
        for it in inst:
            it["x"] = it["x"] - _dot(it["p"], it["xb"])
    for it in inst:
        eg = jnp.exp(it["gc"])
        rhs = jnp.concatenate([it["kb"] * eg, it["v"] * it["bc"]], axis=1).astype(BF16)
        it["sol"] = _dot(it["x"].astype(BF16), rhs)
        last = cc - 1 if it["d"] == 0 else 0
        g_last = it["gc"][last:last + 1, :]
        it["gtot"] = jnp.exp(g_last)
        it["kd"] = (it["k"] * jnp.exp(g_last - it["gc"])).astype(BF16)
        it["qd"] = it["qs"] * eg
    for it in inst:
        wq = jnp.concatenate([it["sol"][:, :hd], it["qd"]], axis=0).astype(BF16)
        it["r"] = _dot(wq, it["st"].astype(BF16))
    for it in inst:
        it["vn"] = (it["sol"][:, hd:] - it["r"][:cc]).astype(BF16)
    for it in inst:
        it["o"] = it["r"][cc:] + _dot(it["attn"], it["vn"])
        it["sn"] = it["st"] * it["gtot"] + _dot_tn(it["kd"], it["vn"])
    for it in inst:
        it["o_ref"][:, it["h"] * hd:(it["h"] + 1) * hd] = it["o"]
        s_ref[it["slot"]] = it["sn"]


def dn_scan(qkv_c, z_misc, a_log, dt_bias, lay):
    t = qkv_c.shape[0]
    w = BRANCH_W
    c = DN_CHUNK
    nb = lay["B"]
    ncc = lay["C"] // c
    ncl = lay["S"] // c
    nsteps = ncc + ncl

    def fidx(b, s):
        return jnp.where(s < ncc, b * ncc + s, nb * ncc + b * ncl + (s - ncc))

    def bidx(b, s):
        return jnp.where(s < ncc, b * ncc + (ncc - 1 - s), nb * ncc + b * ncl + (ncl - 1 - (s - ncc)))

    tri, strict, lvl, eye = _dn_masks()
    nlevels = lvl.shape[1]
    nh = DN_HEADS
    alog_row = jnp.zeros((1, LANES), F32).at[0, :2 * nh].set(a_log.reshape(-1).astype(F32))
    dt_row = jnp.zeros((1, LANES), F32).at[0, :2 * nh].set(dt_bias.reshape(-1).astype(F32))
    const2 = lambda b, s: (0, 0)
    const3 = lambda b, s: (0, 0, 0)
    const4 = lambda b, s: (0, 0, 0, 0)
    return pl.pallas_call(
        functools.partial(_dn_kernel, nlevels=nlevels),
        grid=(nb, nsteps),
        in_specs=[
            pl.BlockSpec((c, 3 * w), lambda b, s: (fidx(b, s), 0)),
            pl.BlockSpec((c, 3 * w), lambda b, s: (bidx(b, s), 0)),
            pl.BlockSpec((c, LANES), lambda b, s: (fidx(b, s), 2)),
            pl.BlockSpec((c, LANES), lambda b, s: (bidx(b, s), 2)),
            pl.BlockSpec((1, LANES), const2),
            pl.BlockSpec((1, LANES), const2),
            pl.BlockSpec((2, c, c), const3),
            pl.BlockSpec((2, c, c), const3),
            pl.BlockSpec((2, nlevels, c, c), const4),
            pl.BlockSpec((c, c), const2),
        ],
        out_specs=[
            pl.BlockSpec((c, w), lambda b, s: (fidx(b, s), 0)),
            pl.BlockSpec((c, w), lambda b, s: (bidx(b, s), 0)),
        ],
        out_shape=[jax.ShapeDtypeStruct((t, w), F32), jax.ShapeDtypeStruct((t, w), F32)],
        scratch_shapes=[pltpu.VMEM((2 * nh, DN_HEAD_DIM, DN_HEAD_DIM), F32)],
        compiler_params=_params(("arbitrary", "arbitrary")),
        name="dn_scan",
    )(qkv_c, qkv_c, z_misc, z_misc, alog_row, dt_row, jnp.asarray(tri), jnp.asarray(strict), jnp.asarray(lvl),
      jnp.asarray(eye))


def _mla_q_kernel(z_ref, qn_ref, w_ref, cos_ref, sin_ref, o_ref, *, scale):
    nh = MLA_HEADS
    xn = _rms(z_ref[...].astype(F32), qn_ref[...]).astype(BF16)
    na = nh * MLA_NOPE
    nr = nh * MLA_ROPE
    qa = _dot(xn, w_ref[:, 0:na])
    qr = _dot(xn, w_ref[:, na:na + nr])
    qt = _dot(xn, w_ref[:, na + nr:na + 2 * nr])
    reps = nr // LANES
    cos = jnp.concatenate([cos_ref[...]] * reps, axis=1)
    sin = jnp.concatenate([sin_ref[...]] * reps, axis=1)
    qrr = qr * cos + qt * sin
    for h in range(nh):
        o_ref[h, :, 0:MLA_NOPE] = (qa[:, h * MLA_NOPE:(h + 1) * MLA_NOPE] * scale).astype(o_ref.dtype)
        o_ref[h, :, MLA_NOPE:MLA_QK] = (qrr[:, h * MLA_ROPE:(h + 1) * MLA_ROPE] * scale).astype(o_ref.dtype)


def mla_q(z_mla, q_norm, wq_ext, cos2, sin2, row0, nrows, scale):
    tm = _pick(nrows, ROW_TILE, BF16_SUBLANES)
    assert row0 % tm == 0
    r0 = row0 // tm
    return pl.pallas_call(
        functools.partial(_mla_q_kernel, scale=scale),
        grid=(nrows // tm,),
        in_specs=[
            pl.BlockSpec((tm, Q_LORA), lambda i: (i + r0, 0)),
            pl.BlockSpec((1, Q_LORA), lambda i: (0, 0)),
            pl.BlockSpec(wq_ext.shape, lambda i: (0, 0)),
            pl.BlockSpec((tm, LANES), lambda i: (i + r0, 0)),
            pl.BlockSpec((tm, LANES), lambda i: (i + r0, 0)),
        ],
        out_specs=pl.BlockSpec((MLA_HEADS, tm, MLA_QK), lambda i: (0, i, 0)),
        out_shape=jax.ShapeDtypeStruct((MLA_HEADS, nrows, MLA_QK), BF16),
        compiler_params=_params(("arbitrary",)),
        name="mla_q",
    )(z_mla, q_norm.reshape(1, Q_LORA), wq_ext, cos2, sin2)


def _mla_kv_kernel(z_ref, kvn_ref, w_ref, kr_ref, kt_ref, cos_ref, sin_ref, k_ref, v_ref):
    nh = MLA_HEADS
    xn = _rms(z_ref[...].astype(F32), kvn_ref[...]).astype(BF16)
    kv = _dot(xn, w_ref[...])
    r = MLA_ROPE
    kr = (kr_ref[:, 0:r] * cos_ref[:, 0:r] + kt_ref[:, 0:r] * sin_ref[:, 0:r]).astype(k_ref.dtype)
    per = MLA_NOPE + MLA_V
    ones_col = (lax.broadcasted_iota(jnp.int32, (kv.shape[0], MLA_VE - MLA_V), 1) == 0).astype(v_ref.dtype)
    for h in range(nh):
        k_ref[h, :, 0:MLA_NOPE] = kv[:, h * per:h * per + MLA_NOPE].astype(k_ref.dtype)
        k_ref[h, :, MLA_NOPE:MLA_QK] = kr
        v_ref[h, :, 0:MLA_V] = kv[:, h * per + MLA_NOPE:(h + 1) * per].astype(v_ref.dtype)
        v_ref[h, :, MLA_V:MLA_VE] = ones_col


def mla_kv(z_mla, z_misc, kv_norm, w_ukv, cos2, sin2, row0, nrows):
    tm = _pick(nrows, ROW_TILE, BF16_SUBLANES)
    assert row0 % tm == 0
    r0 = row0 // tm
    return pl.pallas_call(
        _mla_kv_kernel,
        grid=(nrows // tm,),
        in_specs=[
            pl.BlockSpec((tm, KV_LORA), lambda i: (i + r0, 1)),
            pl.BlockSpec((1, KV_LORA), lambda i: (0, 0)),
            pl.BlockSpec(w_ukv.shape, lambda i: (0, 0)),
            pl.BlockSpec((tm, LANES), lambda i: (i + r0, 0)),
            pl.BlockSpec((tm, LANES), lambda i: (i + r0, 1)),
            pl.BlockSpec((tm, LANES), lambda i: (i + r0, 0)),
            pl.BlockSpec((tm, LANES), lambda i: (i + r0, 0)),
        ],
        out_specs=[
            pl.BlockSpec((MLA_HEADS, tm, MLA_QK), lambda i: (0, i, 0)),
            pl.BlockSpec((MLA_HEADS, tm, MLA_VE), lambda i: (0, i, 0)),
        ],
        out_shape=[jax.ShapeDtypeStruct((MLA_HEADS, nrows, MLA_QK), BF16),
                   jax.ShapeDtypeStruct((MLA_HEADS, nrows, MLA_VE), BF16)],
        compiler_params=_params(("arbitrary",)),
        name="mla_kv",
    )(z_mla, kv_norm.reshape(1, KV_LORA), w_ukv, z_misc, z_misc, cos2, sin2)


def _softmax_update(carry, q, k, v):
    m, acc = carry
    s = _dot_nt(q, k)
    m_new = jnp.maximum(m, jnp.max(s, axis=-1, keepdims=True))
    p = jnp.exp2(s - m_new)
    acc = jnp.exp2(m - m_new) * acc + _dot(p.astype(BF16), v)
    return m_new, acc


def _attn_kernel(q_ref, kc_ref, vc_ref, *rest, tk, nkl):
    if nkl:
        kl_ref, vl_ref, o_ref = rest
    else:
        (o_ref,) = rest
    q = q_ref[...]
    tq = q.shape[0]
    init = (jnp.full((tq, 1), -jnp.inf, F32), jnp.zeros((tq, MLA_VE), F32))
    carry = _softmax_update(init, q, kc_ref[...], vc_ref[...])
    if nkl:
        def body(j, c):
            off = pl.multiple_of(j * tk, tk)
            return _softmax_update(c, q, kl_ref[pl.ds(off, tk), :], vl_ref[pl.ds(off, tk), :])
        carry = lax.fori_loop(0, nkl, body, carry, unroll=True)
    _, acc = carry
    o_ref[...] = (acc[:, 0:MLA_V] / acc[:, MLA_V:MLA_V + 1]).astype(o_ref.dtype)


def attention(q, kc, vc, kl, vl, nq_per_batch, lay):
    nb, c, s = lay["B"], lay["C"], lay["S"]
    tq = _pick(nq_per_batch, ATTN_Q_TILE, BF16_SUBLANES)
    nqb = nq_per_batch // tq
    in_specs = [
        pl.BlockSpec((None, tq, MLA_QK), lambda b, h, i: (h, b * nqb + i, 0)),
        pl.BlockSpec((None, c, MLA_QK), lambda b, h, i: (h, b, 0)),
        pl.BlockSpec((None, c, MLA_VE), lambda b, h, i: (h, b, 0)),
    ]
    args = [q, kc, vc]
    tk = 0
    nkl = 0
    if kl is not None:
        tk = _pick(s, ATTN_KV_TILE, BF16_SUBLANES)
        nkl = s // tk
        in_specs += [
            pl.BlockSpec((None, s, MLA_QK), lambda b, h, i: (h, b, 0)),
            pl.BlockSpec((None, s, MLA_VE), lambda b, h, i: (h, b, 0)),
        ]
        args += [kl, vl]
    return pl.pallas_call(
        functools.partial(_attn_kernel, tk=tk, nkl=nkl),
        grid=(nb, MLA_HEADS, nqb),
        in_specs=in_specs,
        out_specs=pl.BlockSpec((tq, MLA_V), lambda b, h, i: (b * nqb + i, h)),
        out_shape=jax.ShapeDtypeStruct((nb * nq_per_batch, MLA_HEADS * MLA_V), BF16),
        compiler_params=_params(("arbitrary", "arbitrary", "arbitrary")),
        name="attention_lat" if kl is not None else "attention_ctx",
    )(*args)


def _merge1_kernel(bra_ref, of_ref, ob_ref, gate_ref, attc_ref, attl_ref, g0_ref, g1_ref, g2_ref, dnn_ref, wb_ref,
                   y_ref, brb_ref, *, nctx_blk):
    hd = DN_HEAD_DIM
    att = jnp.where(pl.program_id(1) < nctx_blk, attc_ref[...], attl_ref[...])
    o = of_ref[...] + ob_ref[...]
    for h in range(DN_HEADS):
        cs = slice(h * hd, (h + 1) * hd)
        oh = o[:, cs]
        yn = oh * lax.rsqrt(jnp.mean(oh * oh, axis=-1, keepdims=True) + EPS) * dnn_ref[:, cs]
        brb_ref[:, cs] = (yn * _silu(gate_ref[:, cs].astype(F32))).astype(brb_ref.dtype)
    y = jax.nn.sigmoid(g0_ref[...].astype(F32)) * _dot(bra_ref[...], wb_ref[0])
    y = y + jax.nn.sigmoid(g1_ref[...].astype(F32)) * _dot(brb_ref[...], wb_ref[1])
    y = y + jax.nn.sigmoid(g2_ref[...].astype(F32)) * _dot(att, wb_ref[2])
    y_ref[...] = y.astype(y_ref.dtype)


def merge_branches(br_a, o_f, o_b, z_main, attn_c, attn_l, dn_norm, w_branch, lay):
    t = br_a.shape[0]
    d = lay["D"]
    w = BRANCH_W
    tm = _pick(lay["n_ctx"], ROW_TILE, BF16_SUBLANES)
    tn = _pick(d, COL_TILE)
    nn = d // tn
    gate0 = 6 * w // tn
    nctx_blk = lay["n_ctx"] // tm

    def gspec(j):
        return pl.BlockSpec((tm, tn), lambda n, i: (i, gate0 + j * nn + n))

    row = lambda n, i: (i, 0)
    return pl.pallas_call(
        functools.partial(_merge1_kernel, nctx_blk=nctx_blk),
        grid=(nn, t // tm),
        in_specs=[
            pl.BlockSpec((tm, w), row),
            pl.BlockSpec((tm, w), row),
            pl.BlockSpec((tm, w), row),
            pl.BlockSpec((tm, w), lambda n, i: (i, 5)),
            pl.BlockSpec((tm, w), lambda n, i: (jnp.minimum(i, nctx_blk - 1), 0)),
            pl.BlockSpec((tm, w), lambda n, i: (jnp.maximum(i - nctx_blk, 0), 0)),
            gspec(0), gspec(1), gspec(2),
            pl.BlockSpec((1, w), lambda n, i: (0, 0)),
            pl.BlockSpec((3, w, tn), lambda n, i: (0, 0, n)),
        ],
        out_specs=pl.BlockSpec((tm, tn), lambda n, i: (i, n)),
        out_shape=jax.ShapeDtypeStruct((t, d), BF16),
        scratch_shapes=[pltpu.VMEM((tm, w), BF16)],
        compiler_params=_params(("arbitrary", "arbitrary")),
        name="merge_branches",
    )(br_a, o_f, o_b, z_main, attn_c, attn_l, z_main, z_main, z_main,
      jnp.tile(dn_norm.astype(F32), DN_HEADS).reshape(1, w), w_branch)


def _merge2_kernel(y_ref, wo_ref, x_ref, mod_ref, n2_ref, wrh_ref, wrl_ref, xo_ref, h2_ref, lg_ref, *, d):
    m = mod_ref[...]
    mix = _dot(y_ref[...], wo_ref[...])
    xn = x_ref[...] + _mod_part(m, 2, d) * mix
    xo_ref[...] = xn
    h2 = _rms(xn, n2_ref[...]) * (1.0 + _mod_part(m, 4, d)) + _mod_part(m, 3, d)
    h2_ref[...] = h2
    hh = h2.astype(BF16)
    hl = (h2 - hh.astype(F32)).astype(BF16)
    lg_ref[...] = _dot(hh, wrh_ref[...]) + (_dot(hl, wrh_ref[...]) + _dot(hh, wrl_ref[...]))


def out_proj_residual(y, w_out, x, mod, layer, norm2, w_router_pad, lay):
    t, d = x.shape
    tm = _pick(lay["n_ctx"], RES_ROW_TILE, BF16_SUBLANES)
    row = lambda i: (i, 0)
    const = lambda i: (0, 0)
    wr_hi = w_router_pad.astype(BF16)
    wr_lo = (w_router_pad - wr_hi.astype(F32)).astype(BF16)
    return pl.pallas_call(
        functools.partial(_merge2_kernel, d=d),
        grid=(t // tm,),
        in_specs=[
            pl.BlockSpec((tm, d), row),
            pl.BlockSpec((d, d), const),
            pl.BlockSpec((tm, d), row),
            _mod_spec(layer, tm, lay),
            pl.BlockSpec((1, d), const),
            pl.BlockSpec((d, LANES), const),
            pl.BlockSpec((d, LANES), const),
        ],
        out_specs=[pl.BlockSpec((tm, d), row), pl.BlockSpec((tm, d), row), pl.BlockSpec((tm, LANES), row)],
        out_shape=[jax.ShapeDtypeStruct((t, d), F32), jax.ShapeDtypeStruct((t, d), F32),
                   jax.ShapeDtypeStruct((t, LANES), F32)],
        compiler_params=_params(("arbitrary",)),
        name="out_proj_residual",
    )(y, w_out, x, mod, norm2.reshape(1, d), wr_hi, wr_lo)


def _topk_kernel(lg_ref, bias_ref, tri_ref, idx_ref, wt_ref, rank_ref, cnt_ref, run_ref):
    ne, ng = N_EXPERTS, N_GROUPS
    per = ne // ng
    lt = lg_ref[...].T
    tm = lt.shape[1]
    sc = jax.nn.sigmoid(lt[0:ne])
    ch = sc + bias_ref[...]
    ch3 = ch.reshape(ng, per, tm)
    neg = -jnp.inf
    sub = lax.broadcasted_iota(jnp.int32, (ng, per, tm), 1)
    m1 = jnp.max(ch3, axis=1, keepdims=True)
    i1 = jnp.min(jnp.where(ch3 == m1, sub, per), axis=1, keepdims=True)
    m2 = jnp.max(jnp.where(sub == i1, neg, ch3), axis=1, keepdims=True)
    gs = (m1 + m2).reshape(ng, tm)
    giota = lax.broadcasted_iota(jnp.int32, (ng, tm), 0)
    sel = jnp.zeros((ng, tm), F32)
    cur = gs
    for _ in range(TOPK_GROUPS):
        m = jnp.max(cur, axis=0, keepdims=True)
        ix = jnp.min(jnp.where(cur == m, giota, ng), axis=0, keepdims=True)
        hit = giota == ix
        sel = jnp.where(hit, 1.0, sel)
        cur = jnp.where(hit, neg, cur)
    masked = jnp.where(sel.reshape(ng, 1, tm) > 0, ch3, neg).reshape(ne, tm)
    eiota = lax.broadcasted_iota(jnp.int32, (ne, tm), 0)
    idxs, ws, hits = [], [], []
    for _ in range(TOP_K):
        m = jnp.max(masked, axis=0, keepdims=True)
        ix = jnp.min(jnp.where(masked == m, eiota, ne), axis=0, keepdims=True)
        hit = eiota == ix
        ws.append(jnp.sum(jnp.where(hit, sc, 0.0), axis=0, keepdims=True))
        idxs.append(ix)
        hits.append(hit)
        masked = jnp.where(hit, neg, masked)
    wall = jnp.concatenate(ws, axis=0)
    idx_ref[...] = jnp.concatenate(idxs, axis=0)
    wt_ref[...] = wall / jnp.sum(wall, axis=0, keepdims=True) * ROUTED_SCALE

    @pl.when(pl.program_id(0) == 0)
    def _():
        run_ref[...] = jnp.zeros(run_ref.shape, F32)

    chosen = hits[0].astype(F32)
    for hit in hits[1:]:
        chosen = chosen + hit.astype(F32)
    before = run_ref[:, 0:1] + _dot(chosen.astype(BF16), tri_ref[...])
    ranks = [jnp.sum(jnp.where(hit, before, 0.0), axis=0, keepdims=True) for hit in hits]
    rank_ref[...] = jnp.concatenate(ranks, axis=0).astype(jnp.int32)
    run_ref[...] = run_ref[...] + jnp.sum(chosen, axis=1, keepdims=True)
    cnt_ref[...] = run_ref[...]


def route_topk(logits, bias):
    t = logits.shape[0]
    tm = _pick(t, ROW_TILE)
    strict_upper = jnp.asarray(np.triu(np.ones((tm, tm), np.float32), 1), BF16)
    col = lambda i: (0, i)
    idx_t, wt_t, rank_t, cnt = pl.pallas_call(
        _topk_kernel,
        grid=(t // tm,),
        in_specs=[pl.BlockSpec((tm, LANES), lambda i: (i, 0)), pl.BlockSpec((N_EXPERTS, 1), lambda i: (0, 0)),
                  pl.BlockSpec((tm, tm), lambda i: (0, 0))],
        out_specs=[pl.BlockSpec((TOP_K, tm), col), pl.BlockSpec((TOP_K, tm), col), pl.BlockSpec((TOP_K, tm), col),
                   pl.BlockSpec((N_EXPERTS, LANES), lambda i: (0, 0))],
        out_shape=[jax.ShapeDtypeStruct((TOP_K, t), jnp.int32), jax.ShapeDtypeStruct((TOP_K, t), F32),
                   jax.ShapeDtypeStruct((TOP_K, t), jnp.int32), jax.ShapeDtypeStruct((N_EXPERTS, LANES), F32)],
        scratch_shapes=[pltpu.VMEM((N_EXPERTS, LANES), F32)],
        compiler_params=_params(("arbitrary",)),
        name="route_topk",
    )(logits, bias.astype(F32).reshape(N_EXPERTS, 1), strict_upper)
    return idx_t, wt_t, rank_t, cnt[:, 0].astype(jnp.int32)


def moe_layout(idx_t, rank_t, counts, bm):
    k, t = idx_t.shape
    n_blocks = -(-(t * k) // bm) + N_EXPERTS
    padded = (counts + bm - 1) // bm * bm
    pend = jnp.cumsum(padded)
    pstart = pend - padded
    experts = jnp.arange(N_EXPERTS, dtype=jnp.int32)[:, None, None]
    pos_t = rank_t + jnp.sum(jnp.where(idx_t[None] == experts, pstart[:, None, None], 0), axis=0)
    n_used = (pend[-1] // bm).astype(jnp.int32)
    blk = jnp.minimum(jnp.arange(n_blocks, dtype=jnp.int32), n_used - 1) * bm
    blk_e = jnp.minimum(jnp.sum((pend[None, :] <= blk[:, None]).astype(jnp.int32), axis=1), N_EXPERTS - 1)
    bidx = jnp.arange(n_blocks, dtype=jnp.int32)
    last_of_expert = jnp.any((pend[None, :] == (bidx[:, None] + 1) * bm) & (counts[None, :] > 0), axis=1)
    zero_flag = (last_of_expert | (bidx >= n_used)).astype(jnp.int32)
    blk_e = blk_e.astype(jnp.int32)
    fresh = jnp.concatenate([jnp.ones((1,), jnp.int32), (blk_e[1:] != blk_e[:-1]).astype(jnp.int32)])
    run_par = (jnp.cumsum(fresh) - 1) % 2
    active = jnp.where(counts > 0, jnp.arange(N_EXPERTS, dtype=jnp.int32), N_EXPERTS)
    later = lax.cummin(jnp.concatenate([active[1:], jnp.full((1,), N_EXPERTS, jnp.int32)]), reverse=True)
    next_e = jnp.where(later < N_EXPERTS, later, -1)[blk_e]
    pos_flat = pos_t.astype(jnp.int32).T.reshape(-1)
    return pos_flat, blk_e, n_used.reshape(1), run_par.astype(jnp.int32), next_e.astype(jnp.int32), zero_flag


def _pack_pairs(x):
    n = x.shape[1] // 2
    lo = lax.bitcast_convert_type(x[:, :n].astype(BF16).astype(F32), jnp.uint32)
    hi = lax.bitcast_convert_type(x[:, n:].astype(BF16).astype(F32), jnp.uint32)
    return (lo >> 16) | (hi & jnp.uint32(0xFFFF0000))


def _unpack_pairs(w):
    lo = lax.bitcast_convert_type(w << 16, F32)
    hi = lax.bitcast_convert_type(w & jnp.uint32(0xFFFF0000), F32)
    return lo, hi


def _row_copy(src_ref, src_row, dst_ref, dst_row, sem):
    return pltpu.make_async_copy(src_ref.at[pl.ds(src_row, 1)], dst_ref.at[pl.ds(dst_row, 1)], sem)


def _dispatch_kernel(pos_ref, zf_ref, h_ref, xs_ref, buf_ref, zero_ref, sem_ref, zsem_ref, *, tm, bm, n_blocks,
                     nsteps):
    i = pl.program_id(0)
    slot = i % 2

    def wait_slot(s):
        for _ in range(TOP_K):
            pltpu.make_async_copy(buf_ref.at[s], xs_ref.at[pl.ds(0, tm)], sem_ref.at[s]).wait()

    @pl.when(i == 0)
    def _():
        zero_ref[...] = jnp.zeros(zero_ref.shape, zero_ref.dtype)

        def zero_copy(b):
            return pltpu.make_async_copy(zero_ref, xs_ref.at[pl.ds(pl.multiple_of(b * bm, bm), bm)], zsem_ref.at[0])

        def start(b, carry):
            @pl.when(zf_ref[b] != 0)
            def _():
                zero_copy(b).start()
            return carry

        def wait(b, carry):
            @pl.when(zf_ref[b] != 0)
            def _():
                zero_copy(b).wait()
            return carry

        lax.fori_loop(0, n_blocks, start, 0)
        lax.fori_loop(0, n_blocks, wait, 0)

    @pl.when(i >= 2)
    def _():
        wait_slot(slot)

    buf_ref[slot] = _pack_pairs(h_ref[...])

    def body(tok, carry):
        for k in range(TOP_K):
            _row_copy(buf_ref.at[slot], tok, xs_ref, pos_ref[tok * TOP_K + k], sem_ref.at[slot]).start(
                priority=k % 2)
        return carry

    lax.fori_loop(0, tm, body, 0)

    @pl.when(i == nsteps - 1)
    def _():
        if nsteps > 1:
            wait_slot(1 - slot)
        wait_slot(slot)


def moe_dispatch(h2, pos_t, zero_flag, bm):
    t, d = h2.shape
    tm = _pick(t, MOE_TOKEN_TILE)
    nsteps = t // tm
    n_blocks = zero_flag.shape[0]
    return pl.pallas_call(
        functools.partial(_dispatch_kernel, tm=tm, bm=bm, n_blocks=n_blocks, nsteps=nsteps),
        grid=(nsteps,),
        in_specs=[
            pl.BlockSpec((tm * TOP_K,), lambda i: (i,), memory_space=pltpu.SMEM),
            pl.BlockSpec(memory_space=pltpu.SMEM),
            pl.BlockSpec((tm, d), lambda i: (i, 0)),
        ],
        out_specs=pl.BlockSpec(memory_space=pl.ANY),
        out_shape=jax.ShapeDtypeStruct((n_blocks * bm, d // 2), jnp.uint32),
        scratch_shapes=[pltpu.VMEM((2, tm, d // 2), jnp.uint32), pltpu.VMEM((bm, d // 2), jnp.uint32),
                        pltpu.SemaphoreType.DMA((2,)), pltpu.SemaphoreType.DMA((1,))],
        compiler_params=_params(("arbitrary",)),
        name="moe_dispatch",
    )(pos_t, zero_flag, h2)


def _moe_kernel(be_ref, nu_ref, par_ref, nx_ref, x_ref, w1_hbm, w3_hbm, w2_hbm, y_ref, f1_s, f3_s, f2_s, w13_s, w2_s,
                sem_ref, *, ed, half, layer):
    i = pl.program_id(0)
    prev = be_ref[jnp.maximum(i - 1, 0)]
    fresh = jnp.logical_or(i == 0, be_ref[i] != prev)
    slot = par_ref[i]

    def weight_copies(e, s):
        return (pltpu.make_async_copy(w1_hbm.at[layer, e], f1_s.at[s], sem_ref.at[s, 0]),
                pltpu.make_async_copy(w3_hbm.at[layer, e], f3_s.at[s], sem_ref.at[s, 1]),
                pltpu.make_async_copy(w2_hbm.at[layer, e], f2_s.at[s], sem_ref.at[s, 2]))

    @pl.when(i == 0)
    def _():
        for cp in weight_copies(be_ref[0], slot):
            cp.start()

    @pl.when(fresh)
    def _():
        for cp in weight_copies(be_ref[i], slot):
            cp.wait()

        @pl.when(nx_ref[i] >= 0)
        def _():
            for cp in weight_copies(nx_ref[i], 1 - slot):
                cp.start()

        w13_s[:, 0:ed] = f1_s[slot].astype(BF16)
        w13_s[:, ed:2 * ed] = f3_s[slot].astype(BF16)
        w2_s[...] = f2_s[slot].astype(BF16)

    @pl.when(i < nu_ref[0])
    def _():
        lo, hi = _unpack_pairs(x_ref[...])
        a = _dot(lo.astype(BF16), w13_s[0:half, :]) + _dot(hi.astype(BF16), w13_s[half:2 * half, :])
        act = (_silu(a[:, 0:ed]) * a[:, ed:2 * ed]).astype(BF16)
        y_ref[...] = _pack_pairs(_dot(act, w2_s[...]))

    @pl.when(i >= nu_ref[0])
    def _():
        y_ref[...] = jnp.zeros(y_ref.shape, y_ref.dtype)


def moe_experts(xs, blk_e, n_used, run_par, next_e, w1, w3, w2, layer, bm):
    n_rows, half = xs.shape
    d = 2 * half
    ed = w1.shape[-1]
    n_blocks = n_rows // bm
    any_spec = pl.BlockSpec(memory_space=pl.ANY)
    grid_spec = pltpu.PrefetchScalarGridSpec(
        num_scalar_prefetch=4,
        grid=(n_blocks,),
        in_specs=[pl.BlockSpec((bm, half), lambda i, *_: (i, 0)), any_spec, any_spec, any_spec],
        out_specs=pl.BlockSpec((bm, half), lambda i, *_: (i, 0)),
        scratch_shapes=[pltpu.VMEM((2, d, ed), F32), pltpu.VMEM((2, d, ed), F32), pltpu.VMEM((2, ed, d), F32),
                        pltpu.VMEM((d, 2 * ed), BF16), pltpu.VMEM((ed, d), BF16),
                        pltpu.SemaphoreType.DMA((2, 3))],
    )
    return pl.pallas_call(
        functools.partial(_moe_kernel, ed=ed, half=half, layer=layer),
        grid_spec=grid_spec,
        out_shape=jax.ShapeDtypeStruct(xs.shape, jnp.uint32),
        compiler_params=_params(("arbitrary",)),
        name="moe_experts",
    )(blk_e, n_used, run_par, next_e, xs, w1, w3, w2)


def _shared_kernel(pos_ref, posn_ref, wt_ref, h2_ref, w13_ref, w2_ref, ys_ref, x_ref, mod_ref, ng_ref, nmod_ref,
                   xo_ref, ho_ref, ybuf_ref, sem_ref, *, d, ed, tm, nsteps, final):
    i = pl.program_id(0)
    slot = i % 2

    def issue(p_ref, s):
        def body(tok, carry):
            for k in range(TOP_K):
                _row_copy(ys_ref, p_ref[tok * TOP_K + k], ybuf_ref.at[s, k], tok, sem_ref.at[s]).start(
                    priority=k % 2)
            return carry
        lax.fori_loop(0, tm, body, 0)

    @pl.when(i == 0)
    def _():
        issue(pos_ref, 0)

    @pl.when(i + 1 < nsteps)
    def _():
        issue(posn_ref, 1 - slot)

    m = mod_ref[...]
    a = _dot(h2_ref[...].astype(BF16), w13_ref[...])
    act = (_silu(a[:, 0:ed]) * a[:, ed:2 * ed]).astype(BF16)
    f = _dot(act, w2_ref[...])

    for k in range(TOP_K):
        pltpu.make_async_copy(ys_ref.at[pl.ds(0, tm)], ybuf_ref.at[slot, k], sem_ref.at[slot]).wait()
    half = d // 2
    wks = [wt_ref[:, k:k + 1] for k in range(TOP_K)]
    r_lo, r_hi = [], []
    for c in range(half // LANES):
        cs = slice(c * LANES, (c + 1) * LANES)
        acc_lo = acc_hi = None
        for k in range(TOP_K):
            lo, hi = _unpack_pairs(ybuf_ref[slot, k, :, cs])
            acc_lo = wks[k] * lo if acc_lo is None else acc_lo + wks[k] * lo
            acc_hi = wks[k] * hi if acc_hi is None else acc_hi + wks[k] * hi
        r_lo.append(acc_lo)
        r_hi.append(acc_hi)
    f = f + jnp.concatenate(r_lo + r_hi, axis=1)
    xn = x_ref[...] + _mod_part(m, 5, d) * f
    xo_ref[...] = xn
    y = _rms(xn, ng_ref[...])
    if not final:
        nm = nmod_ref[...]
        y = y * (1.0 + _mod_part(nm, 1, d)) + _mod_part(nm, 0, d)
    ho_ref[...] = y.astype(ho_ref.dtype)


def shared_residual(h2, sw13, sw2, ys, pos_t, wts, x, mod, layer, next_g, final, lay):
    t, d = x.shape
    ed = sw2.shape[0]
    tm = _pick(lay["n_ctx"], RES_ROW_TILE, BF16_SUBLANES)
    r0 = lay["n_ctx"] // tm if final else 0
    nsteps = t // tm - r0
    row = lambda i: (i + r0, 0)
    out_row = lambda i: (i, 0)
    const = lambda i: (0, 0)
    next_layer = layer if final else layer + 1
    n_out = nsteps * tm
    return pl.pallas_call(
        functools.partial(_shared_kernel, d=d, ed=ed, tm=tm, nsteps=nsteps, final=final),
        grid=(nsteps,),
        in_specs=[
            pl.BlockSpec((tm * TOP_K,), lambda i: (i + r0,), memory_space=pltpu.SMEM),
            pl.BlockSpec((tm * TOP_K,), lambda i: (jnp.minimum(i + 1, nsteps - 1) + r0,), memory_space=pltpu.SMEM),
            pl.BlockSpec((tm, TOP_K), row),
            pl.BlockSpec((tm, d), row),
            pl.BlockSpec((d, 2 * ed), const),
            pl.BlockSpec((ed, d), const),
            pl.BlockSpec(memory_space=pl.ANY),
            pl.BlockSpec((tm, d), row),
            _mod_spec(layer, tm, lay, r0),
            pl.BlockSpec((1, d), const),
            _mod_spec(next_layer, tm, lay, r0),
        ],
        out_specs=[pl.BlockSpec((tm, d), out_row), pl.BlockSpec((tm, d), out_row)],
        out_shape=[jax.ShapeDtypeStruct((n_out, d), F32), jax.ShapeDtypeStruct((n_out, d), F32 if final else BF16)],
        scratch_shapes=[pltpu.VMEM((2, TOP_K, tm, d // 2), jnp.uint32), pltpu.SemaphoreType.DMA((2,))],
        compiler_params=_params(("arbitrary",)),
        name="shared_residual",
    )(pos_t, pos_t, wts, h2, sw13, sw2, ys, x, mod, next_g.reshape(1, d), mod)


def _rope_tables(lay):
    half = MLA_ROPE // 4
    s = lay["S"]
    freq = ROPE_BASE ** (-np.arange(half, dtype=np.float64) / half)
    tpos = np.arange(s)
    ang_r = (tpos // GRID_W)[:, None] * freq
    ang_c = (tpos % GRID_W)[:, None] * freq
    ang = np.concatenate([ang_r, ang_r, ang_c, ang_c], axis=1)
    cos_l = np.tile(np.cos(ang), (lay["B"], 2))
    sin_l = np.tile(np.sin(ang), (lay["B"], 2))
    cos = np.concatenate([np.ones((lay["n_ctx"], 2 * MLA_ROPE)), cos_l], axis=0).astype(np.float32)
    sin = np.concatenate([np.zeros((lay["n_ctx"], 2 * MLA_ROPE)), sin_l], axis=0).astype(np.float32)
    return jnp.asarray(cos), jnp.asarray(sin)


def _rot_cols(w):
    q = MLA_ROPE // 4
    a, b, c, e = w[..., 0:q], w[..., q:2 * q], w[..., 2 * q:3 * q], w[..., 3 * q:4 * q]
    return jnp.concatenate([-b, a, -e, c], axis=-1)


def _split_w_in_t_kernel(x_ref, main_ref, mla_ref, small_ref, *, o_small, o_cq, o_kr, o_gate, n_gate):
    ch = 2 * LANES
    for a in range(0, o_small, ch):
        main_ref[:, a:a + ch] = x_ref[a:a + ch, :].T.astype(main_ref.dtype)
    for a in range(0, n_gate, ch):
        main_ref[:, o_small + a:o_small + a + ch] = x_ref[o_gate + a:o_gate + a + ch, :].T.astype(main_ref.dtype)
    for a in range(0, o_kr - o_cq, ch):
        mla_ref[:, a:a + ch] = x_ref[o_cq + a:o_cq + a + ch, :].T.astype(mla_ref.dtype)
    n_ab = o_cq - o_small
    w0 = o_kr - n_ab
    t_ab = x_ref[o_small:o_small + LANES, :].T
    t_kr = x_ref[w0:w0 + LANES, :].T
    lane = lax.broadcasted_iota(jnp.int32, t_ab.shape, 1)
    small_ref[...] = jnp.where(lane < n_ab, t_ab, jnp.where(lane < n_ab + MLA_ROPE, t_kr, 0.0))


def split_w_in(w_in_all, layer, offs):
    o_small, o_cq, o_kr, o_gate = offs
    _, d, d_in = w_in_all.shape
    n_gate = d_in - o_gate
    tk = _pick(d, 2 * LANES)
    row = lambda i: (i, 0)
    return pl.pallas_call(
        functools.partial(_split_w_in_t_kernel, o_small=o_small, o_cq=o_cq, o_kr=o_kr, o_gate=o_gate,
                          n_gate=n_gate),
        grid=(d // tk,),
        in_specs=[pl.BlockSpec((None, d_in, tk), lambda i: (layer, 0, i))],
        out_specs=[pl.BlockSpec((tk, o_small + n_gate), row), pl.BlockSpec((tk, o_kr - o_cq), row),
                   pl.BlockSpec((tk, LANES), row)],
        out_shape=[jax.ShapeDtypeStruct((d, o_small + n_gate), BF16), jax.ShapeDtypeStruct((d, o_kr - o_cq), BF16),
                   jax.ShapeDtypeStruct((d, LANES), F32)],
        compiler_params=_params(("arbitrary",)),
        name="split_w_in",
    )(jnp.swapaxes(w_in_all, 1, 2))


def _prep_layer_weights(w_in_all, layer, w_uq, w_ukv, w_branch, w_out, w_router, sw1, sw3, sw2, d):
    w = BRANCH_W
    o_small = 6 * w
    o_cq = o_small + 4 * DN_HEADS
    o_kr = o_cq + Q_LORA + KV_LORA
    o_gate = o_kr + MLA_ROPE
    w_main, w_mla, w_small = split_w_in(w_in_all, layer, (o_small, o_cq, o_kr, o_gate))
    n_ab = o_cq - o_small
    w_kr = w_small[:, n_ab:n_ab + MLA_ROPE]
    zpad = jnp.zeros((d, LANES - MLA_ROPE), F32)
    w_misc = jnp.concatenate(
        [w_kr, zpad, _rot_cols(w_kr), zpad, w_small[:, 0:n_ab], jnp.zeros((d, LANES - n_ab), F32)],
        axis=1).astype(BF16)
    uq = w_uq.reshape(Q_LORA, MLA_HEADS, MLA_QK)
    uq_r = uq[..., MLA_NOPE:]
    wq_ext = jnp.concatenate(
        [uq[..., :MLA_NOPE].reshape(Q_LORA, -1), uq_r.reshape(Q_LORA, -1), _rot_cols(uq_r).reshape(Q_LORA, -1)],
        axis=1).astype(BF16)
    wr_pad = jnp.zeros((d, LANES), F32).at[:, :N_EXPERTS].set(w_router.astype(F32))
    return dict(w_main=w_main, w_mla=w_mla, w_misc=w_misc, wq_ext=wq_ext, w_ukv=w_ukv.astype(BF16),
                w_branch=w_branch.astype(BF16), w_out=w_out.astype(BF16), wr_pad=wr_pad,
                sw13=jnp.concatenate([sw1, sw3], axis=1).astype(BF16), sw2=sw2.astype(BF16))


def kernel(x, c, ctx, c_ctx, w_mod, b_mod, norm1, norm2, w_in, gm_norm, gm_ws, gm_bs, dn_conv, dn_a_log,
           dn_dt_bias, dn_norm, mla_q_norm, mla_kv_norm, mla_w_uq, mla_w_ukv, w_branch, w_out, moe_router,
           moe_bias, moe_w1, moe_w3, moe_w2, shared_w1, shared_w3, shared_w2, final_norm):
    nb, s, d = x.shape
    cl = ctx.shape[1]
    depth = w_mod.shape[0]
    n_ctx = nb * cl
    t = n_ctx + nb * s
    lay = dict(B=nb, S=s, C=cl, D=d, n_ctx=n_ctx, T=t)
    assert nb + 1 <= SUBLANES and s % GRID_W == 0

    xs = jnp.concatenate([ctx.reshape(n_ctx, d), x.reshape(nb * s, d)], axis=0).astype(F32)
    cvec = jnp.zeros((SUBLANES, d), F32).at[0].set(c_ctx.astype(F32)).at[1:1 + nb].set(c.astype(F32))
    mod = modulation(cvec, w_mod, b_mod)
    cos2, sin2 = _rope_tables(lay)
    scale = MLA_QK ** -0.5 * LOG2E

    h = prenorm(xs, norm1[0], mod, 0, lay)
    out = None
    for i in range(depth):
        last = i == depth - 1
        wts = _prep_layer_weights(w_in, i, mla_w_uq[i], mla_w_ukv[i], w_branch[i], w_out[i], moe_router[i],
                                  shared_w1[i], shared_w3[i], shared_w2[i], d)
        z_main = matmul(h, wts["w_main"], BF16, "in_proj_main", WIDE_ROW_TILE)
        z_mla = matmul(h, wts["w_mla"], BF16, "in_proj_mla")
        z_misc = matmul(h, wts["w_misc"], F32, "in_proj_misc")

        br_a = gmlp(z_main, gm_norm[i], gm_ws[i], gm_bs[i], lay)

        qkv_c = dn_short_conv(z_main, dn_conv[i], lay)
        o_f, o_b = dn_scan(qkv_c, z_misc, dn_a_log[i], dn_dt_bias[i], lay)

        kc, vc = mla_kv(z_mla, z_misc, mla_kv_norm[i], wts["w_ukv"], cos2, sin2, 0, n_ctx)
        kl, vl = mla_kv(z_mla, z_misc, mla_kv_norm[i], wts["w_ukv"], cos2, sin2, n_ctx, nb * s)
        ql = mla_q(z_mla, mla_q_norm[i], wts["wq_ext"], cos2, sin2, n_ctx, nb * s, scale)
        attn_l = attention(ql, kc, vc, kl, vl, s, lay)
        qc = mla_q(z_mla, mla_q_norm[i], wts["wq_ext"], cos2, sin2, 0, n_ctx, scale)
        attn_c = attention(qc, kc, vc, None, None, cl, lay)

        y = merge_branches(br_a, o_f, o_b, z_main, attn_c, attn_l, dn_norm[i], wts["w_branch"], lay)
        xs, h2, logits = out_proj_residual(y, wts["w_out"], xs, mod, i, norm2[i], wts["wr_pad"], lay)

        idx_t, wt_t, rank_t, counts = route_topk(logits, moe_bias[i])
        bm = MOE_ROW_TILE
        pos_t, blk_e, n_used, run_par, next_e, zero_flag = moe_layout(idx_t, rank_t, counts, bm)
        x_sorted = moe_dispatch(h2, pos_t, zero_flag, bm)
        y_sorted = moe_experts(x_sorted, blk_e, n_used, run_par, next_e, moe_w1, moe_w3, moe_w2, i, bm)

        next_g = final_norm if last else norm1[i + 1]
        xs, h = shared_residual(h2, wts["sw13"], wts["sw2"], y_sorted, pos_t, wt_t.T, xs, mod, i, next_g, last, lay)
        out = h
    return out.reshape(nb, s, d).astype(x.dtype)
```
